```python
import math
import jax, jax.numpy as jnp
from jax import lax
import numpy as np

D_MODEL = 1024
BATCH = 2
SEQ = 16384
DEPTH = 2
DEC_BATCH = 8
DEC_SEQ = 8192
PAST_LEN = 128

HEAD_DIM = 64
POOL_WINDOWS = (2, 4, 8, 16)
POOL_GROUPS = len(POOL_WINDOWS)
POOL_WIDTH = D_MODEL // 4
POOL_GROUP_DIM = POOL_WIDTH // POOL_GROUPS
ATTN_WIDTH = D_MODEL // 2
ATTN_HEADS = ATTN_WIDTH // HEAD_DIM
DILATED_PATTERNS = ((128, 1), (512, 4), (2048, 16))
ROT_DIM = HEAD_DIM // 4
ROPE_THETA = 500000.0
SGU_WIDTH = D_MODEL // 4
SGU_GROUPS = 4
SGU_GROUP_DIM = SGU_WIDTH // SGU_GROUPS
SGU_CHUNK = 128
MIX_WIDTH = POOL_WIDTH + ATTN_WIDTH + SGU_WIDTH
PROJ_WIDTH = POOL_WIDTH + 3 * ATTN_WIDTH + 2 * SGU_WIDTH
D_FF = 4 * D_MODEL
N_MOD = 6
EPS = 1e-6
MASK_VALUE = -1e30

kernel_name = "hybrid_pool_dilated_sgu_encoder"


def rms_norm(x, g):
    xf = x.astype(jnp.float32)
    y = xf * lax.rsqrt(jnp.mean(xf * xf, axis=-1, keepdims=True) + EPS)
    return (y * g.astype(jnp.float32)).astype(x.dtype)


def pool_mixer(h, pool_w, pool_scale):
    B, S, _ = h.shape
    hf = h.astype(jnp.float32).reshape(B, S, POOL_GROUPS, POOL_GROUP_DIM)
    cs = jnp.concatenate([jnp.zeros((B, 1, POOL_GROUPS, POOL_GROUP_DIM), jnp.float32),
                          jnp.cumsum(hf, axis=1)], axis=1)
    pos = jnp.arange(S)
    outs = []
    for g, win in enumerate(POOL_WINDOWS):
        lo = jnp.clip(pos - win // 2, 0, S)
        hi = jnp.clip(pos + win // 2, 0, S)
        win_sum = cs[:, hi, g] - cs[:, lo, g]
        cnt = (hi - lo).astype(jnp.float32)[None, :, None]
        outs.append(win_sum / cnt - hf[:, :, g])
    p = jnp.stack(outs, axis=2).astype(h.dtype)
    y = jnp.einsum('bsgc,gcd->bsgd', p, pool_w).reshape(B, S, POOL_WIDTH)
    return y * pool_scale


def partial_rotary(x, pos):
    inv_freq = ROPE_THETA ** (-jnp.arange(0, ROT_DIM, 2, dtype=jnp.float32) / ROT_DIM)
    ang = pos.astype(jnp.float32)[:, None] * inv_freq[None, :]
    cos = jnp.cos(ang)[None, :, None, :]
    sin = jnp.sin(ang)[None, :, None, :]
    xf = x.astype(jnp.float32)
    x1 = xf[..., :ROT_DIM // 2]
    x2 = xf[..., ROT_DIM // 2:ROT_DIM]
    out = jnp.concatenate([x1 * cos - x2 * sin, x2 * cos + x1 * sin, xf[..., ROT_DIM:]], axis=-1)
    return out.astype(x.dtype)


def band_attention(q, k, v, half):
    N, L, H, dh = q.shape
    blk = half
    nb = -(-L // blk)
    Lp = nb * blk
    qb = jnp.pad(q, ((0, 0), (0, Lp - L), (0, 0), (0, 0))).reshape(N, nb, blk, H, dh)

    def neighbourhood(t):
        tb = jnp.pad(t, ((0, 0), (blk, Lp - L + blk), (0, 0), (0, 0))).reshape(N, nb + 2, blk, H, dh)
        return jnp.concatenate([tb[:, :-2], tb[:, 1:-1], tb[:, 2:]], axis=2)

    kb = neighbourhood(k)
    vb = neighbourhood(v)
    qpos = jnp.arange(Lp).reshape(nb, blk)
    kpos = jnp.arange(nb)[:, None] * blk - blk + jnp.arange(3 * blk)[None, :]
    valid = ((jnp.abs(qpos[:, :, None] - kpos[:, None, :]) <= half)
             & (kpos[:, None, :] >= 0) & (kpos[:, None, :] < L))
    s = jnp.einsum('nbqhd,nbkhd->nbhqk', qb, kb).astype(jnp.float32) * (dh ** -0.5)
    s = jnp.where(valid[None, :, None], s, MASK_VALUE)
    m = jnp.max(s, axis=-1)
    p = jnp.exp(s - m[..., None])
    l = jnp.sum(p, axis=-1)
    o = jnp.einsum('nbhqk,nbkhd->nbqhd', p, vb.astype(jnp.float32))
    o = o / jnp.transpose(l, (0, 1, 3, 2))[..., None]
    o = o.reshape(N, Lp, H, dh)[:, :L]
    m = jnp.transpose(m, (0, 1, 3, 2)).reshape(N, Lp, H)[:, :L]
    l = jnp.transpose(l, (0, 1, 3, 2)).reshape(N, Lp, H)[:, :L]
    return o, m, l


def dilated_attention(q, k, v):
    B, S, H, dh = q.shape
    outs, ms, ls = [], [], []
    for win, d in DILATED_PATTERNS:
        half = win // (2 * d)
        Ld = S // d

        def fold(t):
            return t.reshape(B, Ld, d, H, dh).transpose(0, 2, 1, 3, 4).reshape(B * d, Ld, H, dh)

        o, m, l = band_attention(fold(q), fold(k), fold(v), half)
        outs.append(o.reshape(B, d, Ld, H, dh).transpose(0, 2, 1, 3, 4).reshape(B, S, H, dh))
        ms.append(m.reshape(B, d, Ld, H).transpose(0, 2, 1, 3).reshape(B, S, H))
        ls.append(l.reshape(B, d, Ld, H).transpose(0, 2, 1, 3).reshape(B, S, H))
    m_all = jnp.stack(ms)
    wts = jnp.stack(ls) * jnp.exp(m_all - jnp.max(m_all, axis=0, keepdims=True))
    o = jnp.sum(wts[..., None] * jnp.stack(outs), axis=0) / jnp.sum(wts, axis=0)[..., None]
    return o


def spatial_gating(z, sgu_w, sgu_b):
    B, S, _ = z.shape
    u, v = z[..., :SGU_WIDTH], z[..., SGU_WIDTH:]
    vf = v.astype(jnp.float32).reshape(B, S // SGU_CHUNK, SGU_CHUNK, SGU_GROUPS, SGU_GROUP_DIM)
    mu = jnp.mean(vf, axis=-1, keepdims=True)
    var = jnp.mean(jnp.square(vf - mu), axis=-1, keepdims=True)
    vn = ((vf - mu) * lax.rsqrt(var + EPS)).astype(z.dtype)
    vm = jnp.einsum('gpq,bnqgc->bnpgc', sgu_w, vn) + jnp.transpose(sgu_b)[None, None, :, :, None]
    return u * vm.reshape(B, S, SGU_WIDTH)


def mixer(h, w_in, pool_w, pool_scale, sgu_w, sgu_b, w_out):
    B, S, _ = h.shape
    z = h @ w_in
    o1 = POOL_WIDTH
    o2 = o1 + ATTN_WIDTH
    o3 = o2 + ATTN_WIDTH
    o4 = o3 + ATTN_WIDTH
    za, zq, zk, zv, zc = z[..., :o1], z[..., o1:o2], z[..., o2:o3], z[..., o3:o4], z[..., o4:]
    ya = pool_mixer(za, pool_w, pool_scale)
    pos = jnp.arange(S)
    q = partial_rotary(zq.reshape(B, S, ATTN_HEADS, HEAD_DIM), pos)
    k = partial_rotary(zk.reshape(B, S, ATTN_HEADS, HEAD_DIM), pos)
    v = zv.reshape(B, S, ATTN_HEADS, HEAD_DIM)
    yb = dilated_attention(q, k, v).astype(h.dtype).reshape(B, S, ATTN_WIDTH)
    yc = spatial_gating(jax.nn.gelu(zc), sgu_w, sgu_b)
    return jnp.concatenate([ya, yb, yc], axis=-1) @ w_out


def trunk(x, c, w_ada, b_ada, g_mix, g_mlp, w_in, pool_w, pool_scale, sgu_w, sgu_b,
          w_out, w_up, w_down, g_final):
    c_act = jax.nn.silu(c)
    for l in range(DEPTH):
        mod = (c_act @ w_ada[l] + b_ada[l])[:, None, :]
        sh1, sc1, gt1, sh2, sc2, gt2 = jnp.split(mod, N_MOD, axis=-1)
        h = rms_norm(x, g_mix[l]) * (1 + sc1) + sh1
        x = x + gt1 * mixer(h, w_in[l], pool_w[l], pool_scale[l], sgu_w[l], sgu_b[l], w_out[l])
        h = rms_norm(x, g_mlp[l]) * (1 + sc2) + sh2
        x = x + gt2 * (jnp.square(jax.nn.relu(h @ w_up[l])) @ w_down[l])
    return rms_norm(x, g_final)


def setup_inputs(seed: int = 0) -> dict:
    key = jax.random.key(seed)
    ks = jax.random.split(key, 20)
    f32 = jnp.float32
    nrm = lambda k, shape, s: jax.random.normal(k, shape, f32) * s
    return {
        "x_prompt": nrm(ks[0], (BATCH, SEQ, D_MODEL), 1.0),
        "x_sample": nrm(ks[1], (DEC_BATCH, DEC_SEQ, D_MODEL), 1.0),
        "c_prompt": nrm(ks[2], (BATCH, D_MODEL), 1.0),
        "c_sample": nrm(ks[3], (DEC_BATCH, D_MODEL), 1.0),
        "w_ada": nrm(ks[4], (DEPTH, D_MODEL, N_MOD * D_MODEL), 0.5 * D_MODEL ** -0.5),
        "b_ada": nrm(ks[5], (DEPTH, N_MOD * D_MODEL), 0.02),
        "g_mix": 1.0 + nrm(ks[6], (DEPTH, D_MODEL), 0.02),
        "g_mlp": 1.0 + nrm(ks[7], (DEPTH, D_MODEL), 0.02),
        "w_in": nrm(ks[8], (DEPTH, D_MODEL, PROJ_WIDTH), D_MODEL ** -0.5),
        "pool_w": nrm(ks[9], (DEPTH, POOL_GROUPS, POOL_GROUP_DIM, POOL_GROUP_DIM), POOL_GROUP_DIM ** -0.5),
        "pool_scale": 1.0 + nrm(ks[10], (DEPTH, POOL_WIDTH), 0.02),
        "sgu_w": nrm(ks[11], (DEPTH, SGU_GROUPS, SGU_CHUNK, SGU_CHUNK), SGU_CHUNK ** -0.5),
        "sgu_b": 1.0 + nrm(ks[12], (DEPTH, SGU_GROUPS, SGU_CHUNK), 0.02),
        "w_out": nrm(ks[13], (DEPTH, MIX_WIDTH, D_MODEL), MIX_WIDTH ** -0.5),
        "w_up": nrm(ks[14], (DEPTH, D_MODEL, D_FF), D_MODEL ** -0.5),
        "w_down": nrm(ks[15], (DEPTH, D_FF, D_MODEL), D_FF ** -0.5),
        "g_final": 1.0 + nrm(ks[16], (D_MODEL,), 0.02),
    }


def reference(x_prompt, x_sample, c_prompt, c_sample, w_ada, b_ada, g_mix, g_mlp, w_in, pool_w,
              pool_scale, sgu_w, sgu_b, w_out, w_up, w_down, g_final):
    y_prompt = trunk(x_prompt, c_prompt, w_ada, b_ada, g_mix, g_mlp, w_in, pool_w, pool_scale,
                     sgu_w, sgu_b, w_out, w_up, w_down, g_final)
    y_sample = trunk(x_sample, c_sample, w_ada, b_ada, g_mix, g_mlp, w_in, pool_w, pool_scale,
                     sgu_w, sgu_b, w_out, w_up, w_down, g_final)
    return (y_prompt, y_sample)
```

```python
import functools
import math

import jax
import jax.numpy as jnp
import numpy as np
from jax import lax
from jax.experimental import pallas as pl
from jax.experimental.pallas import tpu as pltpu

F32 = jnp.float32
BF16 = jnp.bfloat16

D_MODEL = 1024
DEPTH = 2
HEAD_DIM = 64
POOL_WINDOWS = (2, 4, 8, 16)
POOL_WIDTH = D_MODEL // 4
POOL_GROUP_DIM = POOL_WIDTH // len(POOL_WINDOWS)
ATTN_WIDTH = D_MODEL // 2
ATTN_HEADS = ATTN_WIDTH // HEAD_DIM
DILATIONS = (16, 4, 1)
BAND_HALF = 64
ROT_DIM = HEAD_DIM // 4
ROPE_THETA = 500000.0
SGU_WIDTH = D_MODEL // 4
SGU_GROUPS = 4
SGU_GROUP_DIM = SGU_WIDTH // SGU_GROUPS
SGU_CHUNK = 128
D_FF = 4 * D_MODEL
N_MOD = 6
EPS = 1e-6
MASK_VALUE = -1e30

OFF_Q = POOL_WIDTH
OFF_K = OFF_Q + ATTN_WIDTH
OFF_V = OFF_K + ATTN_WIDTH
OFF_C = OFF_V + ATTN_WIDTH
PROJ_WIDTH = OFF_C + 2 * SGU_WIDTH

LANES = 128
POOL_HALO = 8
ROW_TILE = 512
ATTN_TILE = 512
Q_BLOCK = 128
FF_CHUNK = 1024
ML_WIDTH = LANES
VMEM_LIMIT = 56 * 1024 * 1024


def _const_spec(shape):
    nd = len(shape)
    return pl.BlockSpec(shape, lambda *_: (0,) * nd, pipeline_mode=pl.Buffered(1))


def _rms_mod(x, g, shift, scale):
    y = x * lax.rsqrt(jnp.mean(x * x, axis=-1, keepdims=True) + EPS)
    return (y * g) * (1.0 + scale) + shift


def _ada_kernel(c_ref, w_ref, b_ref, o_ref):
    c = c_ref[...]
    act = c * (1.0 / (1.0 + jnp.exp(-c)))
    o_ref[...] = jnp.dot(act.astype(BF16), w_ref[...].astype(BF16),
                         preferred_element_type=F32) + b_ref[...]


def _ada_mod(c_all, w_ada, b_ada):
    rows = c_all.shape[0]
    n_col = N_MOD * D_MODEL // D_MODEL
    return pl.pallas_call(
        _ada_kernel,
        grid=(DEPTH, n_col),
        in_specs=[
            pl.BlockSpec((rows, D_MODEL), lambda l, j: (0, 0)),
            pl.BlockSpec((None, D_MODEL, D_MODEL), lambda l, j: (l, 0, j)),
            pl.BlockSpec((None, 1, D_MODEL), lambda l, j: (l, 0, j)),
        ],
        out_specs=pl.BlockSpec((None, rows, D_MODEL), lambda l, j: (l, 0, j)),
        out_shape=jax.ShapeDtypeStruct((DEPTH, rows, N_MOD * D_MODEL), F32),
        compiler_params=pltpu.CompilerParams(vmem_limit_bytes=VMEM_LIMIT),
        name="ada_mod",
    )(c_all, w_ada, b_ada.reshape(DEPTH, 1, N_MOD * D_MODEL))


def _rope_cols(z, cos, sin, low_half):
    up = pltpu.roll(z, LANES - ROT_DIM // 2, axis=1)
    down = pltpu.roll(z, ROT_DIM // 2, axis=1)
    return z * cos + jnp.where(low_half, up, down) * sin


def _inproj_kernel(x_ref, mod_ref, g_ref, cos_ref, sin_ref, w_ref, sguw_ref, sgub_ref,
                   za_ref, q_ref, k_ref, v_ref, yc_ref):
    rows = x_ref.shape[0]
    h = _rms_mod(x_ref[...], g_ref[...], mod_ref[0:1, :], mod_ref[1:2, :]).astype(BF16)

    za_ref[...] = jnp.dot(h, w_ref[:, 0:OFF_Q], preferred_element_type=F32)

    cos = cos_ref[...]
    sin = sin_ref[...]
    lane = lax.broadcasted_iota(jnp.int32, (1, LANES), 1)
    low_half = (lane % HEAD_DIM) < (ROT_DIM // 2)
    for c in range(ATTN_WIDTH // LANES):
        zq = jnp.dot(h, w_ref[:, OFF_Q + c * LANES:OFF_Q + (c + 1) * LANES],
                     preferred_element_type=F32)
        q_ref[:, c * LANES:(c + 1) * LANES] = (
            _rope_cols(zq, cos, sin, low_half) * (HEAD_DIM ** -0.5)).astype(BF16)
        zk = jnp.dot(h, w_ref[:, OFF_K + c * LANES:OFF_K + (c + 1) * LANES],
                     preferred_element_type=F32)
        k_ref[:, c * LANES:(c + 1) * LANES] = _rope_cols(zk, cos, sin, low_half).astype(BF16)
    v_ref[...] = jnp.dot(h, w_ref[:, OFF_V:OFF_C], preferred_element_type=F32).astype(BF16)

    gate = jax.nn.gelu(jnp.dot(h, w_ref[:, OFF_C:PROJ_WIDTH], preferred_element_type=F32))
    u = gate[:, :SGU_WIDTH]
    vparts = []
    for g in range(SGU_GROUPS):
        vg = gate[:, SGU_WIDTH + g * SGU_GROUP_DIM:SGU_WIDTH + (g + 1) * SGU_GROUP_DIM]
        mu = jnp.mean(vg, axis=-1, keepdims=True)
        dv = vg - mu
        var = jnp.mean(dv * dv, axis=-1, keepdims=True)
        vparts.append(dv * lax.rsqrt(var + EPS))
    vn = jnp.concatenate(vparts, axis=-1).astype(BF16)

    lane_group = lax.broadcasted_iota(jnp.int32, (1, SGU_WIDTH), 1) // SGU_GROUP_DIM
    bias = sgub_ref[...]
    for ch in range(rows // SGU_CHUNK):
        sl = slice(ch * SGU_CHUNK, (ch + 1) * SGU_CHUNK)
        mixed = jnp.dot(sguw_ref[...], vn[sl], preferred_element_type=F32)
        vm = mixed[0:SGU_CHUNK]
        for g in range(1, SGU_GROUPS):
            vm = jnp.where(lane_group == g, mixed[g * SGU_CHUNK:(g + 1) * SGU_CHUNK], vm)
        yc_ref[sl, :] = (u[sl] * (vm + bias)).astype(BF16)


def _inproj(x, mod, g_mix, cos_t, sin_t, w_in, sgu_w, sgu_b):
    B, S, _ = x.shape
    tm = min(ROW_TILE, S)
    grid = (B, S // tm)
    tok = lambda width: pl.BlockSpec((None, tm, width), lambda b, i: (b, i, 0))
    return pl.pallas_call(
        _inproj_kernel,
        grid=grid,
        in_specs=[
            tok(D_MODEL),
            pl.BlockSpec((None, N_MOD, D_MODEL), lambda b, i: (b, 0, 0)),
            _const_spec((1, D_MODEL)),
            pl.BlockSpec((tm, LANES), lambda b, i: (i, 0)),
            pl.BlockSpec((tm, LANES), lambda b, i: (i, 0)),
            _const_spec((D_MODEL, PROJ_WIDTH)),
            _const_spec((SGU_GROUPS * SGU_CHUNK, SGU_CHUNK)),
            _const_spec((SGU_CHUNK, SGU_WIDTH)),
        ],
        out_specs=[tok(POOL_WIDTH), tok(ATTN_WIDTH), tok(ATTN_WIDTH), tok(ATTN_WIDTH),
                   tok(SGU_WIDTH)],
        out_shape=[
            jax.ShapeDtypeStruct((B, S, POOL_WIDTH), F32),
            jax.ShapeDtypeStruct((B, S, ATTN_WIDTH), BF16),
            jax.ShapeDtypeStruct((B, S, ATTN_WIDTH), BF16),
            jax.ShapeDtypeStruct((B, S, ATTN_WIDTH), BF16),
            jax.ShapeDtypeStruct((B, S, SGU_WIDTH), BF16),
        ],
        compiler_params=pltpu.CompilerParams(
            dimension_semantics=("parallel", "parallel"), vmem_limit_bytes=VMEM_LIMIT),
        name="inproj",
    )(x, mod, g_mix, cos_t, sin_t, w_in, sgu_w, sgu_b)


def _attn_kernel(*refs, first, last, seq_len):
    if first:
        (q_ref, kp_ref, kc_ref, kn_ref, vp_ref, vc_ref, vn_ref,
         acc_out, ml_out, kwin, vwin) = refs
    elif last:
        (q_ref, kp_ref, kc_ref, kn_ref, vp_ref, vc_ref, vn_ref, acc_in, ml_in,
         o_out, kwin, vwin) = refs
    else:
        (q_ref, kp_ref, kc_ref, kn_ref, vp_ref, vc_ref, vn_ref, acc_in, ml_in,
         acc_out, ml_out, kwin, vwin) = refs
    tq = q_ref.shape[0]
    tile = pl.program_id(2)

    kwin[0:BAND_HALF, :] = kp_ref[...]
    kwin[BAND_HALF:BAND_HALF + tq, :] = kc_ref[...]
    kwin[BAND_HALF + tq:, :] = kn_ref[...]
    vwin[0:BAND_HALF, :] = vp_ref[...]
    vwin[BAND_HALF:BAND_HALF + tq, :] = vc_ref[...]
    vwin[BAND_HALF + tq:, :] = vn_ref[...]

    kw = Q_BLOCK + 2 * BAND_HALF
    qi = lax.broadcasted_iota(jnp.int32, (Q_BLOCK, kw), 0)
    ci = lax.broadcasted_iota(jnp.int32, (Q_BLOCK, kw), 1)
    rel = ci - BAND_HALF - qi
    band = jnp.abs(rel) <= BAND_HALF
    ml_lane = lax.broadcasted_iota(jnp.int32, (1, ML_WIDTH), 1)

    for j in range(tq // Q_BLOCK):
        q0 = j * Q_BLOCK
        kpos = tile * tq + q0 - BAND_HALF + ci
        valid = band & (kpos >= 0) & (kpos < seq_len)
        if not first:
            ml_prev = ml_in[q0:q0 + Q_BLOCK, :]
        ml_new = jnp.zeros((Q_BLOCK, ML_WIDTH), F32)
        for h in range(ATTN_HEADS):
            hs = slice(h * HEAD_DIM, (h + 1) * HEAD_DIM)
            qh = q_ref[q0:q0 + Q_BLOCK, hs]
            kh = kwin[q0:q0 + kw, hs]
            vh = vwin[q0:q0 + kw, hs]
            s = lax.dot_general(qh, kh, (((1,), (1,)), ((), ())), preferred_element_type=F32)
            s = jnp.where(valid, s, MASK_VALUE)
            m_cur = jnp.max(s, axis=-1, keepdims=True)
            if first:
                m_new = m_cur
            else:
                m_prev = ml_prev[:, h:h + 1]
                l_prev = ml_prev[:, ATTN_HEADS + h:ATTN_HEADS + h + 1]
                m_new = jnp.maximum(m_prev, m_cur)
                alpha = jnp.exp(m_prev - m_new)
            p = jnp.exp(s - m_new)
            l_new = jnp.sum(p, axis=-1, keepdims=True)
            pv = jnp.dot(p.astype(BF16), vh, preferred_element_type=F32)
            if not first:
                l_new = l_new + alpha * l_prev
                pv = pv + alpha * acc_in[q0:q0 + Q_BLOCK, hs]
            if last:
                o_out[q0:q0 + Q_BLOCK, hs] = (pv / l_new).astype(BF16)
            else:
                acc_out[q0:q0 + Q_BLOCK, hs] = pv
                ml_new = jnp.where(ml_lane == h, m_new, ml_new)
                ml_new = jnp.where(ml_lane == ATTN_HEADS + h, l_new, ml_new)
        if not last:
            ml_out[q0:q0 + Q_BLOCK, :] = ml_new


def _attn_pass(q, k, v, state, dilation, first, last):
    B, S, W = q.shape
    d = dilation
    ld = S // d
    tq = min(ATTN_TILE, ld)
    n_tiles = ld // tq
    halo_per_tile = tq // BAND_HALF
    n_halo = ld // BAND_HALF
    fold = lambda a: a.reshape(B, ld, d * a.shape[-1])

    cur = lambda width: pl.BlockSpec((None, tq, width), lambda b, r, i: (b, i, r))
    prev = pl.BlockSpec((None, BAND_HALF, W),
                        lambda b, r, i: (b, jnp.maximum(i * halo_per_tile - 1, 0), r))
    nxt = pl.BlockSpec((None, BAND_HALF, W),
                       lambda b, r, i: (b, jnp.minimum((i + 1) * halo_per_tile, n_halo - 1), r))
    in_specs = [cur(W), prev, cur(W), nxt, prev, cur(W), nxt]
    args = [fold(q), fold(k), fold(k), fold(k), fold(v), fold(v), fold(v)]
    aliases = {}
    if not first:
        acc, ml = state
        in_specs += [cur(W), cur(ML_WIDTH)]
        args += [fold(acc), fold(ml)]
        if not last:
            aliases = {7: 0, 8: 1}
    if last:
        out_specs = cur(W)
        out_shape = jax.ShapeDtypeStruct((B, ld, d * W), BF16)
    else:
        out_specs = [cur(W), cur(ML_WIDTH)]
        out_shape = [jax.ShapeDtypeStruct((B, ld, d * W), F32),
                     jax.ShapeDtypeStruct((B, ld, d * ML_WIDTH), F32)]
    out = pl.pallas_call(
        functools.partial(_attn_kernel, first=first, last=last, seq_len=ld),
        grid=(B, d, n_tiles),
        in_specs=in_specs,
        out_specs=out_specs,
        out_shape=out_shape,
        scratch_shapes=[pltpu.VMEM((tq + 2 * BAND_HALF, W), BF16),
                        pltpu.VMEM((tq + 2 * BAND_HALF, W), BF16)],
        input_output_aliases=aliases,
        compiler_params=pltpu.CompilerParams(
            dimension_semantics=("parallel", "parallel", "parallel"),
            vmem_limit_bytes=VMEM_LIMIT),
        name=f"attn_d{d}",
    )(*args)
    if last:
        return out.reshape(B, S, W)
    return out[0].reshape(B, S, W), out[1].reshape(B, S, ML_WIDTH)


def _dilated_attention(q, k, v):
    state = None
    n = len(DILATIONS)
    for idx, d in enumerate(DILATIONS):
        state = _attn_pass(q, k, v, state, d, first=(idx == 0), last=(idx == n - 1))
    return state


def _pool_mixer(za_ref, zprev_ref, znext_ref, tile, n_tiles, seq_len):
    rows = za_ref.shape[0]
    za = za_ref[...]
    before = jnp.where(tile > 0, zprev_ref[...], 0.0)
    after = jnp.where(tile < n_tiles - 1, znext_ref[...], 0.0)
    ext = jnp.concatenate([before, za, after], axis=0)
    n_ext = rows + 2 * POOL_HALO
    back = lambda a, s: pltpu.roll(a, s, axis=0)
    fwd = lambda a, s: pltpu.roll(a, n_ext - s, axis=0)
    sums = [ext + back(ext, 1)]
    for step in (1, 2, 4):
        sums.append(back(sums[-1], step) + fwd(sums[-1], step))
    pos = tile * rows + lax.broadcasted_iota(jnp.int32, (rows, 1), 0)
    lane_group = lax.broadcasted_iota(jnp.int32, (1, POOL_WIDTH), 1) // POOL_GROUP_DIM
    mean = None
    for g, win in enumerate(POOL_WINDOWS):
        cnt = (jnp.minimum(pos + win // 2, seq_len) - jnp.maximum(pos - win // 2, 0)).astype(F32)
        m = sums[g][POOL_HALO:POOL_HALO + rows] / cnt
        mean = m if mean is None else jnp.where(lane_group == g, m, mean)
    return mean - za


def _outmlp_kernel(x_ref, mod_ref, za_ref, zprev_ref, znext_ref, yb_ref, yc_ref,
                   poolw_ref, pools_ref, wout_ref, gmlp_ref, wup_ref, wdown_ref, gfin_ref,
                   o_ref, *, seq_len, final):
    tile = pl.program_id(1)
    n_tiles = pl.num_programs(1)
    p = _pool_mixer(za_ref, zprev_ref, znext_ref, tile, n_tiles, seq_len)
    ya = jnp.dot(p.astype(BF16), poolw_ref[...], preferred_element_type=F32) * pools_ref[...]
    o1 = POOL_WIDTH
    o2 = o1 + ATTN_WIDTH
    y = jnp.dot(ya.astype(BF16), wout_ref[0:o1, :], preferred_element_type=F32)
    y = y + jnp.dot(yb_ref[...], wout_ref[o1:o2, :], preferred_element_type=F32)
    y = y + jnp.dot(yc_ref[...], wout_ref[o2:, :], preferred_element_type=F32)
    x1 = x_ref[...] + mod_ref[2:3, :] * y

    h = _rms_mod(x1, gmlp_ref[...], mod_ref[3:4, :], mod_ref[4:5, :]).astype(BF16)
    acc = None
    for c in range(D_FF // FF_CHUNK):
        cs = slice(c * FF_CHUNK, (c + 1) * FF_CHUNK)
        up = jnp.dot(h, wup_ref[:, cs], preferred_element_type=F32)
        act = jnp.square(jnp.maximum(up, 0.0)).astype(BF16)
        part = jnp.dot(act, wdown_ref[cs, :], preferred_element_type=F32)
        acc = part if acc is None else acc + part
    x2 = x1 + mod_ref[5:6, :] * acc
    if final:
        x2 = x2 * lax.rsqrt(jnp.mean(x2 * x2, axis=-1, keepdims=True) + EPS) * gfin_ref[...]
    o_ref[...] = x2


def _outmlp(x, mod, za, yb, yc, pool_w, pool_scale, w_out, g_mlp, w_up, w_down, g_final, final):
    B, S, _ = x.shape
    tm = min(ROW_TILE, S)
    n_tiles = S // tm
    halo_per_tile = tm // POOL_HALO
    n_halo = S // POOL_HALO
    tok = lambda width: pl.BlockSpec((None, tm, width), lambda b, i: (b, i, 0))
    return pl.pallas_call(
        functools.partial(_outmlp_kernel, seq_len=S, final=final),
        grid=(B, n_tiles),
        in_specs=[
            tok(D_MODEL),
            pl.BlockSpec((None, N_MOD, D_MODEL), lambda b, i: (b, 0, 0)),
            tok(POOL_WIDTH),
            pl.BlockSpec((None, POOL_HALO, POOL_WIDTH),
                         lambda b, i: (b, jnp.maximum(i * halo_per_tile - 1, 0), 0)),
            pl.BlockSpec((None, POOL_HALO, POOL_WIDTH),
                         lambda b, i: (b, jnp.minimum((i + 1) * halo_per_tile, n_halo - 1), 0)),
            tok(ATTN_WIDTH),
            tok(SGU_WIDTH),
            _const_spec((POOL_WIDTH, POOL_WIDTH)),
            _const_spec((1, POOL_WIDTH)),
            _const_spec((D_MODEL, D_MODEL)),
            _const_spec((1, D_MODEL)),
            _const_spec((D_MODEL, D_FF)),
            _const_spec((D_FF, D_MODEL)),
            _const_spec((1, D_MODEL)),
        ],
        out_specs=tok(D_MODEL),
        out_shape=jax.ShapeDtypeStruct((B, S, D_MODEL), F32),
        compiler_params=pltpu.CompilerParams(
            dimension_semantics=("parallel", "parallel"), vmem_limit_bytes=VMEM_LIMIT),
        name="outmlp",
    )(x, mod, za, za, za, yb, yc, pool_w, pool_scale, w_out, g_mlp, w_up, w_down, g_final)


def _rope_tables(seq_len):
    inv_freq = ROPE_THETA ** (-jnp.arange(0, ROT_DIM, 2, dtype=F32) / ROT_DIM)
    ang = jnp.arange(seq_len).astype(F32)[:, None] * inv_freq[None, :]
    cos, sin = jnp.cos(ang), jnp.sin(ang)
    pad = HEAD_DIM - ROT_DIM
    cos_head = jnp.concatenate([cos, cos, jnp.ones((seq_len, pad), F32)], axis=1)
    sin_head = jnp.concatenate([-sin, sin, jnp.zeros((seq_len, pad), F32)], axis=1)
    reps = LANES // HEAD_DIM
    return jnp.tile(cos_head, (1, reps)), jnp.tile(sin_head, (1, reps))


def _block_diag(blocks):
    g, n, m = blocks.shape
    eye = jnp.eye(g, dtype=blocks.dtype)
    return (eye[:, None, :, None] * blocks[:, :, None, :]).reshape(g * n, g * m)


def _prepare_layer(l, g_mix, g_mlp, w_in, pool_w, pool_scale, sgu_w, sgu_b, w_out, w_up, w_down):
    return dict(
        g_mix=g_mix[l].reshape(1, D_MODEL),
        g_mlp=g_mlp[l].reshape(1, D_MODEL),
        w_in=w_in[l].astype(BF16),
        pool_w=_block_diag(pool_w[l]).astype(BF16),
        pool_scale=pool_scale[l].reshape(1, POOL_WIDTH),
        sgu_w=sgu_w[l].reshape(SGU_GROUPS * SGU_CHUNK, SGU_CHUNK).astype(BF16),
        sgu_b=jnp.repeat(jnp.transpose(sgu_b[l]), SGU_GROUP_DIM, axis=1),
        w_out=w_out[l].astype(BF16),
        w_up=w_up[l].astype(BF16),
        w_down=w_down[l].astype(BF16),
    )


def _trunk(x, mods, layers, g_final):
    B, S, _ = x.shape
    cos_t, sin_t = _rope_tables(S)
    for l, lw in enumerate(layers):
        mod = mods[l].reshape(B, N_MOD, D_MODEL)
        za, q, k, v, yc = _inproj(x, mod, lw["g_mix"], cos_t, sin_t, lw["w_in"],
                                  lw["sgu_w"], lw["sgu_b"])
        yb = _dilated_attention(q, k, v)
        x = _outmlp(x, mod, za, yb, yc, lw["pool_w"], lw["pool_scale"], lw["w_out"],
                    lw["g_mlp"], lw["w_up"], lw["w_down"], g_final, final=(l == DEPTH - 1))
    return x


def kernel(x_prompt, x_sample, c_prompt, c_sample, w_ada, b_ada, g_mix, g_mlp, w_in, pool_w,
           pool_scale, sgu_w, sgu_b, w_out, w_up, w_down, g_final):
    nb_p = c_prompt.shape[0]
    nb_s = c_sample.shape[0]
    rows = -(-(nb_p + nb_s) // 8) * 8
    c_all = jnp.concatenate(
        [c_prompt, c_sample, jnp.zeros((rows - nb_p - nb_s, D_MODEL), F32)], axis=0)
    mods = _ada_mod(c_all, w_ada, b_ada)
    layers = [_prepare_layer(l, g_mix, g_mlp, w_in, pool_w, pool_scale, sgu_w, sgu_b,
                             w_out, w_up, w_down) for l in range(DEPTH)]
    gf = g_final.reshape(1, D_MODEL)
    y_prompt = _trunk(x_prompt, mods[:, :nb_p], layers, gf)
    y_sample = _trunk(x_sample, mods[:, nb_p:nb_p + nb_s], layers, gf)
    return (y_prompt, y_sample)
```

```python
import functools

import jax
import jax.numpy as jnp
from jax import lax
from jax.experimental import pallas as pl
from jax.experimental.pallas import tpu as pltpu

F32 = jnp.float32
BF16 = jnp.bfloat16

D_MODEL = 1024
DEPTH = 2
HEAD_DIM = 64
POOL_WINDOWS = (2, 4, 8, 16)
POOL_WIDTH = D_MODEL // 4
POOL_GROUP_DIM = POOL_WIDTH // len(POOL_WINDOWS)
ATTN_WIDTH = D_MODEL // 2
BAND_HALF = 64
FOLD = 16
MID_DILATION = 4
ROT_DIM = HEAD_DIM // 4
ROPE_THETA = 500000.0
SGU_WIDTH = D_MODEL // 4
SGU_GROUPS = 4
SGU_GROUP_DIM = SGU_WIDTH // SGU_GROUPS
SGU_CHUNK = 128
D_FF = 4 * D_MODEL
N_MOD = 6
EPS = 1e-6
MASK_VALUE = -1e30

OFF_Q = POOL_WIDTH
OFF_K = OFF_Q + ATTN_WIDTH
OFF_V = OFF_K + ATTN_WIDTH
OFF_C = OFF_V + ATTN_WIDTH
PROJ_WIDTH = OFF_C + 2 * SGU_WIDTH

LANES = 128
HEAD_PAIRS = ATTN_WIDTH // LANES
POOL_HALO = 8
ROW_TILE = 512
ATTN_TILE = 512
Q_BLOCK = 128
K_BLOCK = Q_BLOCK + 2 * BAND_HALF
FF_CHUNK = 1024
VMEM_LIMIT = 56 * 1024 * 1024


def _const_spec(shape):
    nd = len(shape)
    return pl.BlockSpec(shape, lambda *_: (0,) * nd, pipeline_mode=pl.Buffered(1))


def _rms_mod(x, g, shift, scale):
    y = x * lax.rsqrt(jnp.mean(x * x, axis=-1, keepdims=True) + EPS)
    return (y * g) * (1.0 + scale) + shift


def _ada_kernel(c_ref, w_ref, b_ref, o_ref):
    c = c_ref[...]
    act = c * (1.0 / (1.0 + jnp.exp(-c)))
    o_ref[...] = jnp.dot(act.astype(BF16), w_ref[...].astype(BF16),
                         preferred_element_type=F32) + b_ref[...]


def _ada_mod(c_all, w_ada, b_ada):
    rows = c_all.shape[0]
    return pl.pallas_call(
        _ada_kernel,
        grid=(DEPTH, N_MOD),
        in_specs=[
            pl.BlockSpec((rows, D_MODEL), lambda l, j: (0, 0)),
            pl.BlockSpec((None, D_MODEL, D_MODEL), lambda l, j: (l, 0, j)),
            pl.BlockSpec((None, 1, D_MODEL), lambda l, j: (l, 0, j)),
        ],
        out_specs=pl.BlockSpec((None, rows, D_MODEL), lambda l, j: (l, 0, j)),
        out_shape=jax.ShapeDtypeStruct((DEPTH, rows, N_MOD * D_MODEL), F32),
        compiler_params=pltpu.CompilerParams(vmem_limit_bytes=VMEM_LIMIT),
        name="ada_mod",
    )(c_all, w_ada, b_ada.reshape(DEPTH, 1, N_MOD * D_MODEL))


def _rope_cols(z, cos, sin, low_half):
    up = pltpu.roll(z, LANES - ROT_DIM // 2, axis=1)
    down = pltpu.roll(z, ROT_DIM // 2, axis=1)
    return z * cos + jnp.where(low_half, up, down) * sin


def _store_both(val, nat_ref, fold_ref, shuf_ref, c):
    cols = slice(c * LANES, (c + 1) * LANES)
    nat_ref[:, cols] = val.astype(BF16)
    shuf_ref[...] = val
    per_res = val.shape[0] // FOLD
    for r in range(FOLD):
        fold_ref[r, :, cols] = shuf_ref[pl.ds(r, per_res, stride=FOLD), :].astype(BF16)


def _inproj_kernel(x_ref, mod_ref, g_ref, cos_ref, sin_ref, w_ref, sguw_ref, sgub_ref,
                   za_ref, q_ref, k_ref, v_ref, qf_ref, kf_ref, vf_ref, yc_ref, shuf_ref):
    rows = x_ref.shape[0]
    h = _rms_mod(x_ref[...], g_ref[...], mod_ref[0:1, :], mod_ref[1:2, :]).astype(BF16)

    za_ref[...] = jnp.dot(h, w_ref[:, 0:OFF_Q], preferred_element_type=F32)

    cos = cos_ref[...]
    sin = sin_ref[...]
    lane = lax.broadcasted_iota(jnp.int32, (1, LANES), 1)
    low_half = (lane % HEAD_DIM) < (ROT_DIM // 2)
    for c in range(HEAD_PAIRS):
        zq = jnp.dot(h, w_ref[:, OFF_Q + c * LANES:OFF_Q + (c + 1) * LANES],
                     preferred_element_type=F32)
        _store_both(_rope_cols(zq, cos, sin, low_half) * (HEAD_DIM ** -0.5),
                    q_ref, qf_ref, shuf_ref, c)
        zk = jnp.dot(h, w_ref[:, OFF_K + c * LANES:OFF_K + (c + 1) * LANES],
                     preferred_element_type=F32)
        _store_both(_rope_cols(zk, cos, sin, low_half), k_ref, kf_ref, shuf_ref, c)
        zv = jnp.dot(h, w_ref[:, OFF_V + c * LANES:OFF_V + (c + 1) * LANES],
                     preferred_element_type=F32)
        _store_both(zv, v_ref, vf_ref, shuf_ref, c)

    gate = jax.nn.gelu(jnp.dot(h, w_ref[:, OFF_C:PROJ_WIDTH], preferred_element_type=F32))
    u = gate[:, :SGU_WIDTH]
    vparts = []
    for g in range(SGU_GROUPS):
        vg = gate[:, SGU_WIDTH + g * SGU_GROUP_DIM:SGU_WIDTH + (g + 1) * SGU_GROUP_DIM]
        mu = jnp.mean(vg, axis=-1, keepdims=True)
        dv = vg - mu
        var = jnp.mean(dv * dv, axis=-1, keepdims=True)
        vparts.append(dv * lax.rsqrt(var + EPS))
    vn = jnp.concatenate(vparts, axis=-1).astype(BF16)

    lane_group = lax.broadcasted_iota(jnp.int32, (1, SGU_WIDTH), 1) // SGU_GROUP_DIM
    bias = sgub_ref[...]
    for ch in range(rows // SGU_CHUNK):
        sl = slice(ch * SGU_CHUNK, (ch + 1) * SGU_CHUNK)
        mixed = jnp.dot(sguw_ref[...], vn[sl], preferred_element_type=F32)
        vm = mixed[0:SGU_CHUNK]
        for g in range(1, SGU_GROUPS):
            vm = jnp.where(lane_group == g, mixed[g * SGU_CHUNK:(g + 1) * SGU_CHUNK], vm)
        yc_ref[sl, :] = (u[sl] * (vm + bias)).astype(BF16)


def _inproj(x, mod, g_mix, cos_t, sin_t, w_in, sgu_w, sgu_b):
    B, S, _ = x.shape
    tm = ROW_TILE
    assert S % tm == 0 and tm % (FOLD * 16) == 0
    grid = (B, S // tm)
    tok = lambda width: pl.BlockSpec((None, tm, width), lambda b, i: (b, i, 0))
    folded = pl.BlockSpec((None, FOLD, tm // FOLD, ATTN_WIDTH), lambda b, i: (b, 0, i, 0))
    nat_shape = jax.ShapeDtypeStruct((B, S, ATTN_WIDTH), BF16)
    fold_shape = jax.ShapeDtypeStruct((B, FOLD, S // FOLD, ATTN_WIDTH), BF16)
    return pl.pallas_call(
        _inproj_kernel,
        grid=grid,
        in_specs=[
            tok(D_MODEL),
            pl.BlockSpec((None, N_MOD, D_MODEL), lambda b, i: (b, 0, 0)),
            _const_spec((1, D_MODEL)),
            pl.BlockSpec((tm, LANES), lambda b, i: (i, 0)),
            pl.BlockSpec((tm, LANES), lambda b, i: (i, 0)),
            _const_spec((D_MODEL, PROJ_WIDTH)),
            _const_spec((SGU_GROUPS * SGU_CHUNK, SGU_CHUNK)),
            _const_spec((SGU_CHUNK, SGU_WIDTH)),
        ],
        out_specs=[tok(POOL_WIDTH), tok(ATTN_WIDTH), tok(ATTN_WIDTH), tok(ATTN_WIDTH),
                   folded, folded, folded, tok(SGU_WIDTH)],
        out_shape=[jax.ShapeDtypeStruct((B, S, POOL_WIDTH), F32),
                   nat_shape, nat_shape, nat_shape, fold_shape, fold_shape, fold_shape,
                   jax.ShapeDtypeStruct((B, S, SGU_WIDTH), BF16)],
        scratch_shapes=[pltpu.VMEM((tm, LANES), F32)],
        compiler_params=pltpu.CompilerParams(
            dimension_semantics=("parallel", "parallel"), vmem_limit_bytes=VMEM_LIMIT),
        name="inproj",
    )(x, mod, g_mix, cos_t, sin_t, w_in, sgu_w, sgu_b)


def _attn_kernel(*refs, pieces, first, last, seq_len):
    T = pieces
    n_in = 7 + (0 if first else 2)
    q_ref, kp_ref, kc_ref, kn_ref, vp_ref, vc_ref, vn_ref = refs[:7]
    if not first:
        o_in, lse_in = refs[7:9]
    outs = refs[n_in:]
    if last:
        y_out, kwin, vwin, s_scr, o_nat, lse_nat = outs
    else:
        o_out, lse_out, kwin, vwin, s_scr = outs
    tl = q_ref.shape[1]
    halo = BAND_HALF // T
    pq = Q_BLOCK // T
    pk = K_BLOCK // T
    tile = pl.program_id(2)

    kwin[:, 0:halo, :] = kp_ref[...]
    kwin[:, halo:halo + tl, :] = kc_ref[...]
    kwin[:, halo + tl:, :] = kn_ref[...]
    vwin[:, 0:halo, :] = vp_ref[...]
    vwin[:, halo:halo + tl, :] = vc_ref[...]
    vwin[:, halo + tl:, :] = vn_ref[...]

    if last:
        per_res = tl // FOLD
        for r in range(FOLD):
            for c in range(HEAD_PAIRS):
                cols = slice(c * LANES, (c + 1) * LANES)
                o_nat[c, pl.ds(r, per_res, stride=FOLD), :] = o_in[r, :, cols]
                lse_nat[c, pl.ds(r, per_res, stride=FOLD), :] = lse_in[r, :, cols]

    row = lax.broadcasted_iota(jnp.int32, (Q_BLOCK, K_BLOCK), 0)
    col = lax.broadcasted_iota(jnp.int32, (Q_BLOCK, K_BLOCK), 1)
    rel = T * (col % pk - halo - row % pq) + (col // pk - row // pq)
    band = jnp.abs(rel) <= BAND_HALF
    lane = lax.broadcasted_iota(jnp.int32, (1, LANES), 1)
    first_head = lane < HEAD_DIM
    one = jnp.ones((), BF16)
    zero = jnp.zeros((), BF16)

    def gather(ref, start, size, cols):
        parts = [ref[t, start:start + size, cols] for t in range(T)]
        return parts[0] if T == 1 else jnp.concatenate(parts, axis=0)

    for a in range(tl // pq):
        la = a * pq
        slot = a % 2
        kidx = tile * tl + la - halo + col % pk
        valid = band & (kidx >= 0) & (kidx < seq_len)

        maxima = []
        for c in range(HEAD_PAIRS):
            cols = slice(c * LANES, (c + 1) * LANES)
            qp = gather(q_ref, la, pq, cols)
            kp = gather(kwin, la, pk, cols)
            for hh in range(2):
                qh = jnp.where(first_head if hh == 0 else ~first_head, qp, zero)
                s = lax.dot_general(qh, kp, (((1,), (1,)), ((), ())),
                                    preferred_element_type=F32)
                s = jnp.where(valid, s, MASK_VALUE)
                maxima.append(jnp.max(s, axis=-1, keepdims=True))
                s_scr[slot, 2 * c + hh] = s

        results = []
        for c in range(HEAD_PAIRS):
            cols = slice(c * LANES, (c + 1) * LANES)
            vp = gather(vwin, la, pk, cols)
            for hh in range(2):
                p = jnp.exp(s_scr[slot, 2 * c + hh] - maxima[2 * c + hh]).astype(BF16)
                vh = jnp.where(first_head if hh == 0 else ~first_head, vp, one)
                results.append(jnp.dot(p, vh, preferred_element_type=F32))

        for c in range(HEAD_PAIRS):
            cols = slice(c * LANES, (c + 1) * LANES)
            r0, r1 = results[2 * c], results[2 * c + 1]
            pv = jnp.where(first_head, r0, r1)
            den = pltpu.roll(jnp.where(first_head, r1, r0), HEAD_DIM, axis=1)
            m = jnp.where(first_head, maxima[2 * c], maxima[2 * c + 1])
            if first:
                o_new = pv / den
                lse_new = m + jnp.log(den)
            else:
                if last:
                    o_prev = o_nat[c, la:la + pq, :]
                    lse_prev = lse_nat[c, la:la + pq, :]
                else:
                    o_prev = gather(o_in, la, pq, cols)
                    lse_prev = gather(lse_in, la, pq, cols)
                top = jnp.maximum(lse_prev, m)
                w_prev = jnp.exp(lse_prev - top)
                w_cur = jnp.exp(m - top)
                total = w_prev + w_cur * den
                o_new = (w_prev * o_prev + w_cur * pv) / total
                if not last:
                    lse_new = top + jnp.log(total)
            if last:
                y_out[la:la + pq, cols] = o_new.astype(BF16)
            else:
                for t in range(T):
                    o_out[t, la:la + pq, cols] = o_new[t * pq:(t + 1) * pq]
                    lse_out[t, la:la + pq, cols] = lse_new[t * pq:(t + 1) * pq]


def _attn_pass(q, k, v, state, pieces, first, last):
    B, T, R, L, W = q.shape
    assert T == pieces
    tl = min(ATTN_TILE // T, L)
    halo = BAND_HALF // T
    assert L % tl == 0 and tl % (Q_BLOCK // T) == 0
    n_tiles = L // tl
    halo_per_tile = tl // halo
    n_halo = L // halo

    cur = lambda: pl.BlockSpec((None, T, None, tl, W), lambda b, r, i: (b, 0, r, i, 0))
    prev = lambda: pl.BlockSpec(
        (None, T, None, halo, W),
        lambda b, r, i: (b, 0, r, jnp.maximum(i * halo_per_tile - 1, 0), 0))
    nxt = lambda: pl.BlockSpec(
        (None, T, None, halo, W),
        lambda b, r, i: (b, 0, r, jnp.minimum((i + 1) * halo_per_tile, n_halo - 1), 0))
    in_specs = [cur(), prev(), cur(), nxt(), prev(), cur(), nxt()]
    args = [q, k, k, k, v, v, v]
    scratch = [pltpu.VMEM((T, tl + 2 * halo, W), BF16), pltpu.VMEM((T, tl + 2 * halo, W), BF16),
               pltpu.VMEM((2, 2 * HEAD_PAIRS, Q_BLOCK, K_BLOCK), F32)]
    if not first:
        if last:
            assert T == 1 and R == 1 and tl == ATTN_TILE
            st = lambda: pl.BlockSpec((None, FOLD, tl // FOLD, W), lambda b, r, i: (b, 0, i, 0))
            scratch += [pltpu.VMEM((HEAD_PAIRS, tl, LANES), F32)] * 2
        else:
            st = cur
        in_specs += [st(), st()]
        args += list(state)
    if last:
        out_specs = pl.BlockSpec((None, tl, W), lambda b, r, i: (b, i, 0))
        out_shape = jax.ShapeDtypeStruct((B, L, W), BF16)
    else:
        out_specs = [cur(), cur()]
        out_shape = [jax.ShapeDtypeStruct((B, T, R, L, W), F32)] * 2
    return pl.pallas_call(
        functools.partial(_attn_kernel, pieces=T, first=first, last=last, seq_len=L),
        grid=(B, R, n_tiles),
        in_specs=in_specs,
        out_specs=out_specs,
        out_shape=out_shape,
        scratch_shapes=scratch,
        compiler_params=pltpu.CompilerParams(
            dimension_semantics=("parallel", "parallel", "parallel"),
            vmem_limit_bytes=VMEM_LIMIT),
        name=f"attn_t{T}_{'first' if first else ('last' if last else 'mid')}",
    )(*args)


def _dilated_attention(nat, folded):
    B, S, W = nat[0].shape
    l16 = S // FOLD
    view16 = [a.reshape(B, 1, FOLD, l16, W) for a in folded]
    state = _attn_pass(*view16, None, pieces=1, first=True, last=False)
    t4 = FOLD // MID_DILATION
    view4 = [a.reshape(B, t4, MID_DILATION, l16, W) for a in folded]
    state = [a.reshape(B, t4, MID_DILATION, l16, W) for a in state]
    state = _attn_pass(*view4, state, pieces=t4, first=False, last=False)
    view1 = [a.reshape(B, 1, 1, S, W) for a in nat]
    state = [a.reshape(B, FOLD, l16, W) for a in state]
    return _attn_pass(*view1, state, pieces=1, first=False, last=True)


def _pool_mixer(za_ref, zprev_ref, znext_ref, tile, n_tiles, seq_len):
    rows = za_ref.shape[0]
    za = za_ref[...]
    before = jnp.where(tile > 0, zprev_ref[...], 0.0)
    after = jnp.where(tile < n_tiles - 1, znext_ref[...], 0.0)
    ext = jnp.concatenate([before, za, after], axis=0)
    n_ext = rows + 2 * POOL_HALO
    back = lambda a, s: pltpu.roll(a, s, axis=0)
    fwd = lambda a, s: pltpu.roll(a, n_ext - s, axis=0)
    sums = [ext + back(ext, 1)]
    for step in (1, 2, 4):
        sums.append(back(sums[-1], step) + fwd(sums[-1], step))
    pos = tile * rows + lax.broadcasted_iota(jnp.int32, (rows, 1), 0)
    lane_group = lax.broadcasted_iota(jnp.int32, (1, POOL_WIDTH), 1) // POOL_GROUP_DIM
    mean = None
    for g, win in enumerate(POOL_WINDOWS):
        cnt = (jnp.minimum(pos + win // 2, seq_len) - jnp.maximum(pos - win // 2, 0)).astype(F32)
        m = sums[g][POOL_HALO:POOL_HALO + rows] / cnt
        mean = m if mean is None else jnp.where(lane_group == g, m, mean)
    return mean - za


def _outmlp_kernel(x_ref, mod_ref, za_ref, zprev_ref, znext_ref, yb_ref, yc_ref,
                   poolw_ref, pools_ref, wout_ref, gmlp_ref, wup_ref, wdown_ref, gfin_ref,
                   o_ref, *, seq_len, final):
    tile = pl.program_id(1)
    n_tiles = pl.num_programs(1)
    p = _pool_mixer(za_ref, zprev_ref, znext_ref, tile, n_tiles, seq_len)
    ya = jnp.dot(p.astype(BF16), poolw_ref[...], preferred_element_type=F32) * pools_ref[...]
    o1 = POOL_WIDTH
    o2 = o1 + ATTN_WIDTH
    y = jnp.dot(ya.astype(BF16), wout_ref[0:o1, :], preferred_element_type=F32)
    y = y + jnp.dot(yb_ref[...], wout_ref[o1:o2, :], preferred_element_type=F32)
    y = y + jnp.dot(yc_ref[...], wout_ref[o2:, :], preferred_element_type=F32)
    x1 = x_ref[...] + mod_ref[2:3, :] * y

    h = _rms_mod(x1, gmlp_ref[...], mod_ref[3:4, :], mod_ref[4:5, :]).astype(BF16)
    acc = None
    for c in range(D_FF // FF_CHUNK):
        cs = slice(c * FF_CHUNK, (c + 1) * FF_CHUNK)
        up = jnp.dot(h, wup_ref[:, cs], preferred_element_type=F32)
        act = jnp.square(jnp.maximum(up, 0.0)).astype(BF16)
        part = jnp.dot(act, wdown_ref[cs, :], preferred_element_type=F32)
        acc = part if acc is None else acc + part
    x2 = x1 + mod_ref[5:6, :] * acc
    if final:
        x2 = x2 * lax.rsqrt(jnp.mean(x2 * x2, axis=-1, keepdims=True) + EPS) * gfin_ref[...]
    o_ref[...] = x2


def _outmlp(x, mod, za, yb, yc, pool_w, pool_scale, w_out, g_mlp, w_up, w_down, g_final, final):
    B, S, _ = x.shape
    tm = ROW_TILE
    assert S % tm == 0
    n_tiles = S // tm
    halo_per_tile = tm // POOL_HALO
    n_halo = S // POOL_HALO
    tok = lambda width: pl.BlockSpec((None, tm, width), lambda b, i: (b, i, 0))
    return pl.pallas_call(
        functools.partial(_outmlp_kernel, seq_len=S, final=final),
        grid=(B, n_tiles),
        in_specs=[
            tok(D_MODEL),
            pl.BlockSpec((None, N_MOD, D_MODEL), lambda b, i: (b, 0, 0)),
            tok(POOL_WIDTH),
            pl.BlockSpec((None, POOL_HALO, POOL_WIDTH),
                         lambda b, i: (b, jnp.maximum(i * halo_per_tile - 1, 0), 0)),
            pl.BlockSpec((None, POOL_HALO, POOL_WIDTH),
                         lambda b, i: (b, jnp.minimum((i + 1) * halo_per_tile, n_halo - 1), 0)),
            tok(ATTN_WIDTH),
            tok(SGU_WIDTH),
            _const_spec((POOL_WIDTH, POOL_WIDTH)),
            _const_spec((1, POOL_WIDTH)),
            _const_spec((D_MODEL, D_MODEL)),
            _const_spec((1, D_MODEL)),
            _const_spec((D_MODEL, D_FF)),
            _const_spec((D_FF, D_MODEL)),
            _const_spec((1, D_MODEL)),
        ],
        out_specs=tok(D_MODEL),
        out_shape=jax.ShapeDtypeStruct((B, S, D_MODEL), F32),
        compiler_params=pltpu.CompilerParams(
            dimension_semantics=("parallel", "parallel"), vmem_limit_bytes=VMEM_LIMIT),
        name="outmlp",
    )(x, mod, za, za, za, yb, yc, pool_w, pool_scale, w_out, g_mlp, w_up, w_down, g_final)


def _rope_tables(seq_len):
    inv_freq = ROPE_THETA ** (-jnp.arange(0, ROT_DIM, 2, dtype=F32) / ROT_DIM)
    ang = jnp.arange(seq_len).astype(F32)[:, None] * inv_freq[None, :]
    cos, sin = jnp.cos(ang), jnp.sin(ang)
    pad = HEAD_DIM - ROT_DIM
    cos_head = jnp.concatenate([cos, cos, jnp.ones((seq_len, pad), F32)], axis=1)
    sin_head = jnp.concatenate([-sin, sin, jnp.zeros((seq_len, pad), F32)], axis=1)
    reps = LANES // HEAD_DIM
    return jnp.tile(cos_head, (1, reps)), jnp.tile(sin_head, (1, reps))


def _block_diag(blocks):
    g, n, m = blocks.shape
    eye = jnp.eye(g, dtype=blocks.dtype)
    return (eye[:, None, :, None] * blocks[:, :, None, :]).reshape(g * n, g * m)


def _prepare_layer(l, g_mix, g_mlp, w_in, pool_w, pool_scale, sgu_w, sgu_b, w_out, w_up, w_down):
    return dict(
        g_mix=g_mix[l].reshape(1, D_MODEL),
        g_mlp=g_mlp[l].reshape(1, D_MODEL),
        w_in=w_in[l].astype(BF16),
        pool_w=_block_diag(pool_w[l]).astype(BF16),
        pool_scale=pool_scale[l].reshape(1, POOL_WIDTH),
        sgu_w=sgu_w[l].reshape(SGU_GROUPS * SGU_CHUNK, SGU_CHUNK).astype(BF16),
        sgu_b=jnp.repeat(jnp.transpose(sgu_b[l]), SGU_GROUP_DIM, axis=1),
        w_out=w_out[l].astype(BF16),
        w_up=w_up[l].astype(BF16),
        w_down=w_down[l].astype(BF16),
    )


def _trunk(x, mods, layers, g_final):
    B, S, _ = x.shape
    cos_t, sin_t = _rope_tables(S)
    for l, lw in enumerate(layers):
        mod = mods[l].reshape(B, N_MOD, D_MODEL)
        za, q, k, v, qf, kf, vf, yc = _inproj(x, mod, lw["g_mix"], cos_t, sin_t, lw["w_in"],
                                              lw["sgu_w"], lw["sgu_b"])
        yb = _dilated_attention((q, k, v), (qf, kf, vf))
        x = _outmlp(x, mod, za, yb, yc, lw["pool_w"], lw["pool_scale"], lw["w_out"],
                    lw["g_mlp"], lw["w_up"], lw["w_down"], g_final, final=(l == DEPTH - 1))
    return x


def kernel(x_prompt, x_sample, c_prompt, c_sample, w_ada, b_ada, g_mix, g_mlp, w_in, pool_w,
           pool_scale, sgu_w, sgu_b, w_out, w_up, w_down, g_final):
    nb_p = c_prompt.shape[0]
    nb_s = c_sample.shape[0]
    rows = -(-(nb_p + nb_s) // 8) * 8
    c_all = jnp.concatenate(
        [c_prompt, c_sample, jnp.zeros((rows - nb_p - nb_s, D_MODEL), F32)], axis=0)
    mods = _ada_mod(c_all, w_ada, b_ada)
    layers = [_prepare_layer(l, g_mix, g_mlp, w_in, pool_w, pool_scale, sgu_w, sgu_b,
                             w_out, w_up, w_down) for l in range(DEPTH)]
    gf = g_final.reshape(1, D_MODEL)
    y_prompt = _trunk(x_prompt, mods[:, :nb_p], layers, gf)
    y_sample = _trunk(x_sample, mods[:, nb_p:nb_p + nb_s], layers, gf)
    return (y_prompt, y_sample)
```

```python
import functools

import jax
import jax.numpy as jnp
from jax import lax
from jax.experimental import pallas as pl
from jax.experimental.pallas import tpu as pltpu

F32 = jnp.float32
BF16 = jnp.bfloat16

D_MODEL = 1024
DEPTH = 2
HEAD_DIM = 64
POOL_WINDOWS = (2, 4, 8, 16)
POOL_WIDTH = D_MODEL // 4
POOL_GROUP_DIM = POOL_WIDTH // len(POOL_WINDOWS)
ATTN_WIDTH = D_MODEL // 2
BAND_HALF = 64
FOLD = 16
MID_DILATION = 4
ROT_DIM = HEAD_DIM // 4
ROPE_THETA = 500000.0
SGU_WIDTH = D_MODEL // 4
SGU_GROUPS = 4
SGU_GROUP_DIM = SGU_WIDTH // SGU_GROUPS
SGU_CHUNK = 128
D_FF = 4 * D_MODEL
N_MOD = 6
EPS = 1e-6
MASK_VALUE = -1e30
Q_SCALE = HEAD_DIM ** -0.5 * 1.4426950408889634

OFF_Q = POOL_WIDTH
OFF_K = OFF_Q + ATTN_WIDTH
OFF_V = OFF_K + ATTN_WIDTH
OFF_C = OFF_V + ATTN_WIDTH
PROJ_WIDTH = OFF_C + 2 * SGU_WIDTH

LANES = 128
HEAD_PAIRS = ATTN_WIDTH // LANES
POOL_HALO = 8
ROW_TILE = 512
ATTN_TILE = 512
Q_BLOCK = 128
K_BLOCK = Q_BLOCK + 2 * BAND_HALF
FF_CHUNK = 1024
SUB_ROWS = 256
N_SHUF = 6
VMEM_LIMIT = 56 * 1024 * 1024


def _const_spec(shape):
    nd = len(shape)
    return pl.BlockSpec(shape, lambda *_: (0,) * nd, pipeline_mode=pl.Buffered(1))


def _rms_mod(x, g, shift, scale):
    y = x * lax.rsqrt(jnp.mean(x * x, axis=-1, keepdims=True) + EPS)
    return (y * g) * (1.0 + scale) + shift


def _ada_kernel(c_ref, w_ref, b_ref, o_ref):
    c = c_ref[...]
    act = c * (1.0 / (1.0 + jnp.exp(-c)))
    o_ref[...] = jnp.dot(act.astype(BF16), w_ref[...].astype(BF16),
                         preferred_element_type=F32) + b_ref[...]


def _ada_mod(c_all, w_ada, b_ada):
    rows = c_all.shape[0]
    return pl.pallas_call(
        _ada_kernel,
        grid=(DEPTH, N_MOD),
        in_specs=[
            pl.BlockSpec((rows, D_MODEL), lambda l, j: (0, 0)),
            pl.BlockSpec((None, D_MODEL, D_MODEL), lambda l, j: (l, 0, j)),
            pl.BlockSpec((None, 1, D_MODEL), lambda l, j: (l, 0, j)),
        ],
        out_specs=pl.BlockSpec((None, rows, D_MODEL), lambda l, j: (l, 0, j)),
        out_shape=jax.ShapeDtypeStruct((DEPTH, rows, N_MOD * D_MODEL), F32),
        compiler_params=pltpu.CompilerParams(vmem_limit_bytes=VMEM_LIMIT),
        name="ada_mod",
    )(c_all, w_ada, b_ada.reshape(DEPTH, 1, N_MOD * D_MODEL))


def _rope_cols(z, cos, sin, low_half):
    up = pltpu.roll(z, LANES - ROT_DIM // 2, axis=1)
    down = pltpu.roll(z, ROT_DIM // 2, axis=1)
    return z * cos + jnp.where(low_half, up, down) * sin


def _store_both(val, nat_ref, fold_ref, scr_a, scr_b, row0, c):
    sub = val.shape[0]
    cols = slice(c * LANES, (c + 1) * LANES)
    nat_ref[row0:row0 + sub, cols] = val.astype(BF16)
    scr_a[...] = val
    quarter = sub // 4
    for r1 in range(4):
        scr_b[r1 * quarter:(r1 + 1) * quarter, :] = scr_a[pl.ds(r1, quarter, stride=4), :]
    per_res = sub // FOLD
    f0 = row0 // FOLD
    for r1 in range(4):
        for r2 in range(4):
            piece = scr_b[pl.ds(r1 * quarter + r2, per_res, stride=4), :]
            fold_ref[4 * r2 + r1, f0:f0 + per_res, cols] = piece.astype(BF16)


def _group_sums(v, ones_bd):
    hi = v.astype(BF16)
    lo = (v - hi.astype(F32)).astype(BF16)
    return (jnp.dot(hi, ones_bd, preferred_element_type=F32)
            + jnp.dot(lo, ones_bd, preferred_element_type=F32))


def _inproj_kernel(x_ref, mod_ref, g_ref, cos_ref, sin_ref, w_ref, sguw_ref, sgub_ref,
                   za_ref, q_ref, k_ref, v_ref, qf_ref, kf_ref, vf_ref, yc_ref, scr_a, scr_b):
    rows = x_ref.shape[0]
    lane = lax.broadcasted_iota(jnp.int32, (1, LANES), 1)
    low_half = (lane % HEAD_DIM) < (ROT_DIM // 2)
    gi = lax.broadcasted_iota(jnp.int32, (SGU_WIDTH, SGU_WIDTH), 0) // SGU_GROUP_DIM
    gj = lax.broadcasted_iota(jnp.int32, (SGU_WIDTH, SGU_WIDTH), 1) // SGU_GROUP_DIM
    ones_bd = jnp.where(gi == gj, 1.0, 0.0).astype(BF16)
    lane_group = lax.broadcasted_iota(jnp.int32, (1, SGU_WIDTH), 1) // SGU_GROUP_DIM
    bias = sgub_ref[...]
    gain = g_ref[...] * (1.0 + mod_ref[1:2, :])
    shift = mod_ref[0:1, :]

    for st in range(rows // SUB_ROWS):
        row0 = st * SUB_ROWS
        rs = slice(row0, row0 + SUB_ROWS)
        x = x_ref[rs, :]
        xn = x * lax.rsqrt(jnp.mean(x * x, axis=-1, keepdims=True) + EPS)
        h = (xn * gain + shift).astype(BF16)

        za_ref[rs, :] = jnp.dot(h, w_ref[:, 0:OFF_Q], preferred_element_type=F32)

        cos = cos_ref[rs, :]
        sin = sin_ref[rs, :]
        zq = jnp.dot(h, w_ref[:, OFF_Q:OFF_K], preferred_element_type=F32)
        zk = jnp.dot(h, w_ref[:, OFF_K:OFF_V], preferred_element_type=F32)
        zv = jnp.dot(h, w_ref[:, OFF_V:OFF_C], preferred_element_type=F32)
        for c in range(HEAD_PAIRS):
            slot = (st * HEAD_PAIRS + c) * 3
            cols = slice(c * LANES, (c + 1) * LANES)
            _store_both(_rope_cols(zq[:, cols], cos, sin, low_half) * Q_SCALE,
                        q_ref, qf_ref, scr_a.at[slot % N_SHUF], scr_b.at[slot % N_SHUF], row0, c)
            _store_both(_rope_cols(zk[:, cols], cos, sin, low_half), k_ref, kf_ref,
                        scr_a.at[(slot + 1) % N_SHUF], scr_b.at[(slot + 1) % N_SHUF], row0, c)
            _store_both(zv[:, cols], v_ref, vf_ref,
                        scr_a.at[(slot + 2) % N_SHUF], scr_b.at[(slot + 2) % N_SHUF], row0, c)

        gate = jax.nn.gelu(jnp.dot(h, w_ref[:, OFF_C:PROJ_WIDTH], preferred_element_type=F32))
        u = gate[:, :SGU_WIDTH]
        v = gate[:, SGU_WIDTH:]
        dv = v - _group_sums(v, ones_bd) * (1.0 / SGU_GROUP_DIM)
        var = _group_sums(dv * dv, ones_bd) * (1.0 / SGU_GROUP_DIM)
        vn = (dv * lax.rsqrt(var + EPS)).astype(BF16)

        for ch in range(SUB_ROWS // SGU_CHUNK):
            sl = slice(ch * SGU_CHUNK, (ch + 1) * SGU_CHUNK)
            mixed = jnp.dot(sguw_ref[...], vn[sl], preferred_element_type=F32)
            vm = mixed[0:SGU_CHUNK]
            for g in range(1, SGU_GROUPS):
                vm = jnp.where(lane_group == g, mixed[g * SGU_CHUNK:(g + 1) * SGU_CHUNK], vm)
            yc_ref[row0 + ch * SGU_CHUNK:row0 + (ch + 1) * SGU_CHUNK, :] = (
                u[sl] * (vm + bias)).astype(BF16)


def _inproj(x, mod, g_mix, cos_t, sin_t, w_in, sgu_w, sgu_b):
    B, S, _ = x.shape
    tm = ROW_TILE
    assert S % tm == 0 and tm % SUB_ROWS == 0 and SUB_ROWS % (FOLD * 16) == 0
    grid = (B, S // tm)
    tok = lambda width: pl.BlockSpec((None, tm, width), lambda b, i: (b, i, 0))
    folded = pl.BlockSpec((None, FOLD, tm // FOLD, ATTN_WIDTH), lambda b, i: (b, 0, i, 0))
    nat_shape = jax.ShapeDtypeStruct((B, S, ATTN_WIDTH), BF16)
    fold_shape = jax.ShapeDtypeStruct((B, FOLD, S // FOLD, ATTN_WIDTH), BF16)
    return pl.pallas_call(
        _inproj_kernel,
        grid=grid,
        in_specs=[
            tok(D_MODEL),
            pl.BlockSpec((None, N_MOD, D_MODEL), lambda b, i: (b, 0, 0)),
            _const_spec((1, D_MODEL)),
            pl.BlockSpec((tm, LANES), lambda b, i: (i, 0)),
            pl.BlockSpec((tm, LANES), lambda b, i: (i, 0)),
            _const_spec((D_MODEL, PROJ_WIDTH)),
            _const_spec((SGU_GROUPS * SGU_CHUNK, SGU_CHUNK)),
            _const_spec((SGU_CHUNK, SGU_WIDTH)),
        ],
        out_specs=[tok(POOL_WIDTH), tok(ATTN_WIDTH), tok(ATTN_WIDTH), tok(ATTN_WIDTH),
                   folded, folded, folded, tok(SGU_WIDTH)],
        out_shape=[jax.ShapeDtypeStruct((B, S, POOL_WIDTH), F32),
                   nat_shape, nat_shape, nat_shape, fold_shape, fold_shape, fold_shape,
                   jax.ShapeDtypeStruct((B, S, SGU_WIDTH), BF16)],
        scratch_shapes=[pltpu.VMEM((N_SHUF, SUB_ROWS, LANES), F32)] * 2,
        compiler_params=pltpu.CompilerParams(
            dimension_semantics=("parallel", "parallel"), vmem_limit_bytes=VMEM_LIMIT),
        name="inproj",
    )(x, mod, g_mix, cos_t, sin_t, w_in, sgu_w, sgu_b)


def _attn_kernel(*refs, pieces, first, last, seq_len):
    T = pieces
    n_in = 7 + (0 if first else 2)
    q_ref, kp_ref, kc_ref, kn_ref, vp_ref, vc_ref, vn_ref = refs[:7]
    if not first:
        o_in, lse_in = refs[7:9]
    outs = refs[n_in:]
    if last:
        y_out, s_scr, m_scr, bias_scr, o_nat, lse_nat, o_mid, lse_mid = outs
    else:
        o_out, lse_out, s_scr, m_scr, bias_scr = outs
    tl = q_ref.shape[1]
    halo = BAND_HALF // T
    pq = Q_BLOCK // T
    pk = K_BLOCK // T
    tile = pl.program_id(2)

    if last:
        per_res = tl // FOLD
        for c in range(HEAD_PAIRS):
            cols = slice(c * LANES, (c + 1) * LANES)
            for src, mid, dst in ((o_in, o_mid, o_nat), (lse_in, lse_mid, lse_nat)):
                for r1 in range(4):
                    for r2 in range(4):
                        mid[c, r1, pl.ds(r2, per_res, stride=4), :] = src[4 * r2 + r1, :, cols]
                for r1 in range(4):
                    dst[c, pl.ds(r1, 4 * per_res, stride=4), :] = mid[c, r1]

    row = lax.broadcasted_iota(jnp.int32, (Q_BLOCK, K_BLOCK), 0)
    col = lax.broadcasted_iota(jnp.int32, (Q_BLOCK, K_BLOCK), 1)
    rel = T * (col % pk - halo - row % pq) + (col // pk - row // pq)
    band = jnp.abs(rel) <= BAND_HALF
    lane = lax.broadcasted_iota(jnp.int32, (1, LANES), 1)
    first_head = lane < HEAD_DIM
    zero = jnp.zeros((), BF16)
    ones_block = jnp.ones((K_BLOCK, LANES), BF16)

    def gather(ref, start, size, cols):
        parts = [ref[t, start:start + size, cols] for t in range(T)]
        return parts[0] if T == 1 else jnp.concatenate(parts, axis=0)

    def window(prev_ref, cur_ref, next_ref, la, cols):
        lo, hi = la - halo, la + pq + halo
        parts = []
        for t in range(T):
            if lo < 0:
                parts.append(prev_ref[t, :, cols])
            parts.append(cur_ref[t, max(lo, 0):min(hi, tl), cols])
            if hi > tl:
                parts.append(next_ref[t, :, cols])
        return parts[0] if len(parts) == 1 else jnp.concatenate(parts, axis=0)

    for a in range(tl // pq):
        la = a * pq
        slot = a % 2
        kidx = tile * tl + la - halo + col % pk
        valid = band & (kidx >= 0) & (kidx < seq_len)
        bias_scr[slot] = jnp.where(valid, 0.0, MASK_VALUE)

        for c in range(HEAD_PAIRS):
            cols = slice(c * LANES, (c + 1) * LANES)
            qp = gather(q_ref, la, pq, cols)
            kp = window(kp_ref, kc_ref, kn_ref, la, cols)
            q2 = jnp.concatenate([jnp.where(first_head, qp, zero),
                                  jnp.where(first_head, zero, qp)], axis=0)
            s = lax.dot_general(q2, kp, (((1,), (1,)), ((), ())), preferred_element_type=F32)
            bias = bias_scr[slot]
            s = s + jnp.concatenate([bias, bias], axis=0)
            m = jnp.max(s, axis=-1, keepdims=True)
            s_scr[slot, c] = s - m
            m_scr[slot, c] = jnp.where(first_head, m[:Q_BLOCK], m[Q_BLOCK:])

        for c in range(HEAD_PAIRS):
            cols = slice(c * LANES, (c + 1) * LANES)
            vp = window(vp_ref, vc_ref, vn_ref, la, cols)
            p = jnp.exp2(s_scr[slot, c]).astype(BF16)
            res = jnp.dot(p, jnp.concatenate([vp, ones_block], axis=1),
                          preferred_element_type=F32)
            pv = jnp.where(first_head, res[:Q_BLOCK, :LANES], res[Q_BLOCK:, :LANES])
            den = jnp.where(first_head, res[:Q_BLOCK, LANES:], res[Q_BLOCK:, LANES:])
            m = m_scr[slot, c]
            if first:
                o_new = pv / den
                lse_new = m + jnp.log2(den)
            else:
                if last:
                    o_prev = o_nat[c, la:la + pq, :]
                    lse_prev = lse_nat[c, la:la + pq, :]
                else:
                    o_prev = gather(o_in, la, pq, cols)
                    lse_prev = gather(lse_in, la, pq, cols)
                top = jnp.maximum(lse_prev, m)
                w_prev = jnp.exp2(lse_prev - top)
                w_cur = jnp.exp2(m - top)
                total = w_prev + w_cur * den
                o_new = (w_prev * o_prev + w_cur * pv) / total
                if not last:
                    lse_new = top + jnp.log2(total)
            if last:
                y_out[la:la + pq, cols] = o_new.astype(BF16)
            else:
                for t in range(T):
                    o_out[t, la:la + pq, cols] = o_new[t * pq:(t + 1) * pq]
                    lse_out[t, la:la + pq, cols] = lse_new[t * pq:(t + 1) * pq]


def _attn_pass(q, k, v, state, pieces, first, last):
    B, T, R, L, W = q.shape
    assert T == pieces
    tl = min(ATTN_TILE // T, L)
    halo = BAND_HALF // T
    assert L % tl == 0 and tl % (Q_BLOCK // T) == 0
    n_tiles = L // tl
    halo_per_tile = tl // halo
    n_halo = L // halo

    cur = lambda: pl.BlockSpec((None, T, None, tl, W), lambda b, r, i: (b, 0, r, i, 0))
    prev = lambda: pl.BlockSpec(
        (None, T, None, halo, W),
        lambda b, r, i: (b, 0, r, jnp.maximum(i * halo_per_tile - 1, 0), 0))
    nxt = lambda: pl.BlockSpec(
        (None, T, None, halo, W),
        lambda b, r, i: (b, 0, r, jnp.minimum((i + 1) * halo_per_tile, n_halo - 1), 0))
    in_specs = [cur(), prev(), cur(), nxt(), prev(), cur(), nxt()]
    args = [q, k, k, k, v, v, v]
    scratch = [pltpu.VMEM((2, HEAD_PAIRS, 2 * Q_BLOCK, K_BLOCK), F32),
               pltpu.VMEM((2, HEAD_PAIRS, Q_BLOCK, LANES), F32),
               pltpu.VMEM((2, Q_BLOCK, K_BLOCK), F32)]
    if not first:
        if last:
            assert T == 1 and R == 1 and tl == ATTN_TILE
            st = lambda: pl.BlockSpec((None, FOLD, tl // FOLD, W), lambda b, r, i: (b, 0, i, 0))
            scratch += [pltpu.VMEM((HEAD_PAIRS, tl, LANES), F32)] * 2
            scratch += [pltpu.VMEM((HEAD_PAIRS, 4, tl // 4, LANES), F32)] * 2
        else:
            st = cur
        in_specs += [st(), st()]
        args += list(state)
    if last:
        out_specs = pl.BlockSpec((None, tl, W), lambda b, r, i: (b, i, 0))
        out_shape = jax.ShapeDtypeStruct((B, L, W), BF16)
    else:
        out_specs = [cur(), cur()]
        out_shape = [jax.ShapeDtypeStruct((B, T, R, L, W), F32)] * 2
    return pl.pallas_call(
        functools.partial(_attn_kernel, pieces=T, first=first, last=last, seq_len=L),
        grid=(B, R, n_tiles),
        in_specs=in_specs,
        out_specs=out_specs,
        out_shape=out_shape,
        scratch_shapes=scratch,
        compiler_params=pltpu.CompilerParams(
            dimension_semantics=("parallel", "parallel", "parallel"),
            vmem_limit_bytes=VMEM_LIMIT),
        name=f"attn_t{T}_{'first' if first else ('last' if last else 'mid')}",
    )(*args)


def _dilated_attention(nat, folded):
    B, S, W = nat[0].shape
    l16 = S // FOLD
    view16 = [a.reshape(B, 1, FOLD, l16, W) for a in folded]
    state = _attn_pass(*view16, None, pieces=1, first=True, last=False)
    t4 = FOLD // MID_DILATION
    view4 = [a.reshape(B, t4, MID_DILATION, l16, W) for a in folded]
    state = [a.reshape(B, t4, MID_DILATION, l16, W) for a in state]
    state = _attn_pass(*view4, state, pieces=t4, first=False, last=False)
    view1 = [a.reshape(B, 1, 1, S, W) for a in nat]
    state = [a.reshape(B, FOLD, l16, W) for a in state]
    return _attn_pass(*view1, state, pieces=1, first=False, last=True)


def _pool_mixer(za_ref, zprev_ref, znext_ref, tile, n_tiles, seq_len):
    rows = za_ref.shape[0]
    za = za_ref[...]
    before = jnp.where(tile > 0, zprev_ref[...], 0.0)
    after = jnp.where(tile < n_tiles - 1, znext_ref[...], 0.0)
    ext = jnp.concatenate([before, za, after], axis=0)
    n_ext = rows + 2 * POOL_HALO
    back = lambda a, s: pltpu.roll(a, s, axis=0)
    fwd = lambda a, s: pltpu.roll(a, n_ext - s, axis=0)
    sums = [ext + back(ext, 1)]
    for step in (1, 2, 4):
        sums.append(back(sums[-1], step) + fwd(sums[-1], step))
    pos = tile * rows + lax.broadcasted_iota(jnp.int32, (rows, 1), 0)
    lane_group = lax.broadcasted_iota(jnp.int32, (1, POOL_WIDTH), 1) // POOL_GROUP_DIM
    mean = None
    for g, win in enumerate(POOL_WINDOWS):
        cnt = (jnp.minimum(pos + win // 2, seq_len) - jnp.maximum(pos - win // 2, 0)).astype(F32)
        m = sums[g][POOL_HALO:POOL_HALO + rows] / cnt
        mean = m if mean is None else jnp.where(lane_group == g, m, mean)
    return mean - za


def _outmlp_kernel(x_ref, mod_ref, za_ref, zprev_ref, znext_ref, yb_ref, yc_ref,
                   poolw_ref, pools_ref, wout_ref, gmlp_ref, wup_ref, wdown_ref, gfin_ref,
                   o_ref, *, seq_len, final):
    tile = pl.program_id(1)
    n_tiles = pl.num_programs(1)
    p = _pool_mixer(za_ref, zprev_ref, znext_ref, tile, n_tiles, seq_len)
    ya = jnp.dot(p.astype(BF16), poolw_ref[...], preferred_element_type=F32) * pools_ref[...]
    o1 = POOL_WIDTH
    o2 = o1 + ATTN_WIDTH
    y = jnp.dot(ya.astype(BF16), wout_ref[0:o1, :], preferred_element_type=F32)
    y = y + jnp.dot(yb_ref[...], wout_ref[o1:o2, :], preferred_element_type=F32)
    y = y + jnp.dot(yc_ref[...], wout_ref[o2:, :], preferred_element_type=F32)
    x1 = x_ref[...] + mod_ref[2:3, :] * y

    h = _rms_mod(x1, gmlp_ref[...], mod_ref[3:4, :], mod_ref[4:5, :]).astype(BF16)
    acc = None
    for c in range(D_FF // FF_CHUNK):
        cs = slice(c * FF_CHUNK, (c + 1) * FF_CHUNK)
        up = jnp.dot(h, wup_ref[:, cs], preferred_element_type=F32)
        act = jnp.square(jnp.maximum(up, 0.0)).astype(BF16)
        part = jnp.dot(act, wdown_ref[cs, :], preferred_element_type=F32)
        acc = part if acc is None else acc + part
    x2 = x1 + mod_ref[5:6, :] * acc
    if final:
        x2 = x2 * lax.rsqrt(jnp.mean(x2 * x2, axis=-1, keepdims=True) + EPS) * gfin_ref[...]
    o_ref[...] = x2


def _outmlp(x, mod, za, yb, yc, pool_w, pool_scale, w_out, g_mlp, w_up, w_down, g_final, final):
    B, S, _ = x.shape
    tm = ROW_TILE
    assert S % tm == 0
    n_tiles = S // tm
    halo_per_tile = tm // POOL_HALO
    n_halo = S // POOL_HALO
    tok = lambda width: pl.BlockSpec((None, tm, width), lambda b, i: (b, i, 0))
    return pl.pallas_call(
        functools.partial(_outmlp_kernel, seq_len=S, final=final),
        grid=(B, n_tiles),
        in_specs=[
            tok(D_MODEL),
            pl.BlockSpec((None, N_MOD, D_MODEL), lambda b, i: (b, 0, 0)),
            tok(POOL_WIDTH),
            pl.BlockSpec((None, POOL_HALO, POOL_WIDTH),
                         lambda b, i: (b, jnp.maximum(i * halo_per_tile - 1, 0), 0)),
            pl.BlockSpec((None, POOL_HALO, POOL_WIDTH),
                         lambda b, i: (b, jnp.minimum((i + 1) * halo_per_tile, n_halo - 1), 0)),
            tok(ATTN_WIDTH),
            tok(SGU_WIDTH),
            _const_spec((POOL_WIDTH, POOL_WIDTH)),
            _const_spec((1, POOL_WIDTH)),
            _const_spec((D_MODEL, D_MODEL)),
            _const_spec((1, D_MODEL)),
            _const_spec((D_MODEL, D_FF)),
            _const_spec((D_FF, D_MODEL)),
            _const_spec((1, D_MODEL)),
        ],
        out_specs=tok(D_MODEL),
        out_shape=jax.ShapeDtypeStruct((B, S, D_MODEL), F32),
        compiler_params=pltpu.CompilerParams(
            dimension_semantics=("parallel", "parallel"), vmem_limit_bytes=VMEM_LIMIT),
        name="outmlp",
    )(x, mod, za, za, za, yb, yc, pool_w, pool_scale, w_out, g_mlp, w_up, w_down, g_final)


def _rope_tables(seq_len):
    inv_freq = ROPE_THETA ** (-jnp.arange(0, ROT_DIM, 2, dtype=F32) / ROT_DIM)
    ang = jnp.arange(seq_len).astype(F32)[:, None] * inv_freq[None, :]
    cos, sin = jnp.cos(ang), jnp.sin(ang)
    pad = HEAD_DIM - ROT_DIM
    cos_head = jnp.concatenate([cos, cos, jnp.ones((seq_len, pad), F32)], axis=1)
    sin_head = jnp.concatenate([-sin, sin, jnp.zeros((seq_len, pad), F32)], axis=1)
    reps = LANES // HEAD_DIM
    return jnp.tile(cos_head, (1, reps)), jnp.tile(sin_head, (1, reps))


def _block_diag(blocks):
    g, n, m = blocks.shape
    eye = jnp.eye(g, dtype=blocks.dtype)
    return (eye[:, None, :, None] * blocks[:, :, None, :]).reshape(g * n, g * m)


def _prepare_layer(l, g_mix, g_mlp, w_in, pool_w, pool_scale, sgu_w, sgu_b, w_out, w_up, w_down):
    return dict(
        g_mix=g_mix[l].reshape(1, D_MODEL),
        g_mlp=g_mlp[l].reshape(1, D_MODEL),
        w_in=w_in[l].astype(BF16),
        pool_w=_block_diag(pool_w[l]).astype(BF16),
        pool_scale=pool_scale[l].reshape(1, POOL_WIDTH),
        sgu_w=sgu_w[l].reshape(SGU_GROUPS * SGU_CHUNK, SGU_CHUNK).astype(BF16),
        sgu_b=jnp.repeat(jnp.transpose(sgu_b[l]), SGU_GROUP_DIM, axis=1),
        w_out=w_out[l].astype(BF16),
        w_up=w_up[l].astype(BF16),
        w_down=w_down[l].astype(BF16),
    )


def _trunk(x, mods, layers, g_final):
    B, S, _ = x.shape
    cos_t, sin_t = _rope_tables(S)
    for l, lw in enumerate(layers):
        mod = mods[l].reshape(B, N_MOD, D_MODEL)
        za, q, k, v, qf, kf, vf, yc = _inproj(x, mod, lw["g_mix"], cos_t, sin_t, lw["w_in"],
                                              lw["sgu_w"], lw["sgu_b"])
        yb = _dilated_attention((q, k, v), (qf, kf, vf))
        x = _outmlp(x, mod, za, yb, yc, lw["pool_w"], lw["pool_scale"], lw["w_out"],
                    lw["g_mlp"], lw["w_up"], lw["w_down"], g_final, final=(l == DEPTH - 1))
    return x


def kernel(x_prompt, x_sample, c_prompt, c_sample, w_ada, b_ada, g_mix, g_mlp, w_in, pool_w,
           pool_scale, sgu_w, sgu_b, w_out, w_up, w_down, g_final):
    nb_p = c_prompt.shape[0]
    nb_s = c_sample.shape[0]
    rows = -(-(nb_p + nb_s) // 8) * 8
    c_all = jnp.concatenate(
        [c_prompt, c_sample, jnp.zeros((rows - nb_p - nb_s, D_MODEL), F32)], axis=0)
    mods = _ada_mod(c_all, w_ada, b_ada)
    layers = [_prepare_layer(l, g_mix, g_mlp, w_in, pool_w, pool_scale, sgu_w, sgu_b,
                             w_out, w_up, w_down) for l in range(DEPTH)]
    gf = g_final.reshape(1, D_MODEL)
    y_prompt = _trunk(x_prompt, mods[:, :nb_p], layers, gf)
    y_sample = _trunk(x_sample, mods[:, nb_p:nb_p + nb_s], layers, gf)
    return (y_prompt, y_sample)
```

```python
import functools

import jax
import jax.numpy as jnp
from jax import lax
from jax.experimental import pallas as pl
from jax.experimental.pallas import tpu as pltpu

F32 = jnp.float32
BF16 = jnp.bfloat16

D_MODEL = 1024
DEPTH = 2
HEAD_DIM = 64
POOL_WINDOWS = (2, 4, 8, 16)
POOL_WIDTH = D_MODEL // 4
POOL_GROUP_DIM = POOL_WIDTH // len(POOL_WINDOWS)
ATTN_WIDTH = D_MODEL // 2
BAND_HALF = 64
FOLD = 16
MID_DILATION = 4
ROT_DIM = HEAD_DIM // 4
ROPE_THETA = 500000.0
SGU_WIDTH = D_MODEL // 4
SGU_GROUPS = 4
SGU_GROUP_DIM = SGU_WIDTH // SGU_GROUPS
SGU_CHUNK = 128
D_FF = 4 * D_MODEL
N_MOD = 6
EPS = 1e-6
MASK_VALUE = -1e30
Q_SCALE = HEAD_DIM ** -0.5 * 1.4426950408889634

OFF_Q = POOL_WIDTH
OFF_K = OFF_Q + ATTN_WIDTH
OFF_V = OFF_K + ATTN_WIDTH
OFF_C = OFF_V + ATTN_WIDTH
PROJ_WIDTH = OFF_C + 2 * SGU_WIDTH

LANES = 128
HEAD_PAIRS = ATTN_WIDTH // LANES
POOL_HALO = 8
ROW_TILE = 512
ATTN_TILE = 512
Q_BLOCK = 128
K_BLOCK = Q_BLOCK + 2 * BAND_HALF
FF_CHUNK = 1024
SUB_ROWS = 256
N_SHUF = 6
VMEM_LIMIT = 56 * 1024 * 1024


def _const_spec(shape):
    nd = len(shape)
    return pl.BlockSpec(shape, lambda *_: (0,) * nd, pipeline_mode=pl.Buffered(1))


def _ada_kernel(c_ref, w_ref, b_ref, o_ref):
    c = c_ref[...]
    act = c * (1.0 / (1.0 + jnp.exp(-c)))
    o_ref[...] = jnp.dot(act.astype(BF16), w_ref[...].astype(BF16),
                         preferred_element_type=F32) + b_ref[...]


def _ada_mod(c_all, w_ada, b_ada):
    rows = c_all.shape[0]
    return pl.pallas_call(
        _ada_kernel,
        grid=(DEPTH, N_MOD),
        in_specs=[
            pl.BlockSpec((rows, D_MODEL), lambda l, j: (0, 0)),
            pl.BlockSpec((None, D_MODEL, D_MODEL), lambda l, j: (l, 0, j)),
            pl.BlockSpec((None, 1, D_MODEL), lambda l, j: (l, 0, j)),
        ],
        out_specs=pl.BlockSpec((None, rows, D_MODEL), lambda l, j: (l, 0, j)),
        out_shape=jax.ShapeDtypeStruct((DEPTH, rows, N_MOD * D_MODEL), F32),
        compiler_params=pltpu.CompilerParams(vmem_limit_bytes=VMEM_LIMIT),
        name="ada_mod",
    )(c_all, w_ada, b_ada.reshape(DEPTH, 1, N_MOD * D_MODEL))


def _rope_cols(z, cos, sin, low_half):
    up = pltpu.roll(z, LANES - ROT_DIM // 2, axis=1)
    down = pltpu.roll(z, ROT_DIM // 2, axis=1)
    return z * cos + jnp.where(low_half, up, down) * sin


def _store_both(val, nat_ref, fold_ref, scr_a, scr_b, row0, c):
    sub = val.shape[0]
    cols = slice(c * LANES, (c + 1) * LANES)
    nat_ref[row0:row0 + sub, cols] = val.astype(BF16)
    scr_a[...] = val
    quarter = sub // 4
    for r1 in range(4):
        scr_b[r1 * quarter:(r1 + 1) * quarter, :] = scr_a[pl.ds(r1, quarter, stride=4), :]
    per_res = sub // FOLD
    f0 = row0 // FOLD
    for r1 in range(4):
        for r2 in range(4):
            piece = scr_b[pl.ds(r1 * quarter + r2, per_res, stride=4), :]
            fold_ref[4 * r2 + r1, f0:f0 + per_res, cols] = piece.astype(BF16)


def _group_sums(v, ones_bd):
    hi = v.astype(BF16)
    lo = (v - hi.astype(F32)).astype(BF16)
    return (jnp.dot(hi, ones_bd, preferred_element_type=F32)
            + jnp.dot(lo, ones_bd, preferred_element_type=F32))


def _inproj_kernel(x_ref, mod_ref, g_ref, cos_ref, sin_ref, w_ref, sguw_ref, sgub_ref,
                   za_ref, q_ref, k_ref, v_ref, qf_ref, kf_ref, vf_ref, yc_ref, scr_a, scr_b):
    rows = x_ref.shape[0]
    lane = lax.broadcasted_iota(jnp.int32, (1, LANES), 1)
    low_half = (lane % HEAD_DIM) < (ROT_DIM // 2)
    gi = lax.broadcasted_iota(jnp.int32, (SGU_WIDTH, SGU_WIDTH), 0) // SGU_GROUP_DIM
    gj = lax.broadcasted_iota(jnp.int32, (SGU_WIDTH, SGU_WIDTH), 1) // SGU_GROUP_DIM
    ones_bd = jnp.where(gi == gj, 1.0, 0.0).astype(BF16)
    lane_group = lax.broadcasted_iota(jnp.int32, (1, SGU_WIDTH), 1) // SGU_GROUP_DIM
    bias = sgub_ref[...]
    gain = g_ref[...] * (1.0 + mod_ref[1:2, :])
    shift = mod_ref[0:1, :]

    for st in range(rows // SUB_ROWS):
        row0 = st * SUB_ROWS
        rs = slice(row0, row0 + SUB_ROWS)
        x = x_ref[rs, :]
        xn = x * lax.rsqrt(jnp.mean(x * x, axis=-1, keepdims=True) + EPS)
        h = (xn * gain + shift).astype(BF16)

        za_ref[rs, :] = jnp.dot(h, w_ref[:, 0:OFF_Q], preferred_element_type=F32)

        cos = cos_ref[rs, :]
        sin = sin_ref[rs, :]
        zq = jnp.dot(h, w_ref[:, OFF_Q:OFF_K], preferred_element_type=F32)
        zk = jnp.dot(h, w_ref[:, OFF_K:OFF_V], preferred_element_type=F32)
        zv = jnp.dot(h, w_ref[:, OFF_V:OFF_C], preferred_element_type=F32)
        for c in range(HEAD_PAIRS):
            slot = (st * HEAD_PAIRS + c) * 3
            cols = slice(c * LANES, (c + 1) * LANES)
            _store_both(_rope_cols(zq[:, cols], cos, sin, low_half) * Q_SCALE,
                        q_ref, qf_ref, scr_a.at[slot % N_SHUF], scr_b.at[slot % N_SHUF], row0, c)
            _store_both(_rope_cols(zk[:, cols], cos, sin, low_half), k_ref, kf_ref,
                        scr_a.at[(slot + 1) % N_SHUF], scr_b.at[(slot + 1) % N_SHUF], row0, c)
            _store_both(zv[:, cols], v_ref, vf_ref,
                        scr_a.at[(slot + 2) % N_SHUF], scr_b.at[(slot + 2) % N_SHUF], row0, c)

        gate = jax.nn.gelu(jnp.dot(h, w_ref[:, OFF_C:PROJ_WIDTH], preferred_element_type=F32))
        u = gate[:, :SGU_WIDTH]
        v = gate[:, SGU_WIDTH:]
        dv = v - _group_sums(v, ones_bd) * (1.0 / SGU_GROUP_DIM)
        var = _group_sums(dv * dv, ones_bd) * (1.0 / SGU_GROUP_DIM)
        vn = (dv * lax.rsqrt(var + EPS)).astype(BF16)

        for ch in range(SUB_ROWS // SGU_CHUNK):
            sl = slice(ch * SGU_CHUNK, (ch + 1) * SGU_CHUNK)
            mixed = jnp.dot(sguw_ref[...], vn[sl], preferred_element_type=F32)
            vm = mixed[0:SGU_CHUNK]
            for g in range(1, SGU_GROUPS):
                vm = jnp.where(lane_group == g, mixed[g * SGU_CHUNK:(g + 1) * SGU_CHUNK], vm)
            yc_ref[row0 + ch * SGU_CHUNK:row0 + (ch + 1) * SGU_CHUNK, :] = (
                u[sl] * (vm + bias)).astype(BF16)


def _inproj(x, mod, g_mix, cos_t, sin_t, w_in, sgu_w, sgu_b):
    B, S, _ = x.shape
    tm = ROW_TILE
    assert S % tm == 0 and tm % SUB_ROWS == 0 and SUB_ROWS % (FOLD * 16) == 0
    grid = (B, S // tm)
    tok = lambda width: pl.BlockSpec((None, tm, width), lambda b, i: (b, i, 0))
    folded = pl.BlockSpec((None, FOLD, tm // FOLD, ATTN_WIDTH), lambda b, i: (b, 0, i, 0))
    nat_shape = jax.ShapeDtypeStruct((B, S, ATTN_WIDTH), BF16)
    fold_shape = jax.ShapeDtypeStruct((B, FOLD, S // FOLD, ATTN_WIDTH), BF16)
    return pl.pallas_call(
        _inproj_kernel,
        grid=grid,
        in_specs=[
            tok(D_MODEL),
            pl.BlockSpec((None, N_MOD, D_MODEL), lambda b, i: (b, 0, 0)),
            _const_spec((1, D_MODEL)),
            pl.BlockSpec((tm, LANES), lambda b, i: (i, 0)),
            pl.BlockSpec((tm, LANES), lambda b, i: (i, 0)),
            _const_spec((D_MODEL, PROJ_WIDTH)),
            _const_spec((SGU_GROUPS * SGU_CHUNK, SGU_CHUNK)),
            _const_spec((SGU_CHUNK, SGU_WIDTH)),
        ],
        out_specs=[tok(POOL_WIDTH), tok(ATTN_WIDTH), tok(ATTN_WIDTH), tok(ATTN_WIDTH),
                   folded, folded, folded, tok(SGU_WIDTH)],
        out_shape=[jax.ShapeDtypeStruct((B, S, POOL_WIDTH), F32),
                   nat_shape, nat_shape, nat_shape, fold_shape, fold_shape, fold_shape,
                   jax.ShapeDtypeStruct((B, S, SGU_WIDTH), BF16)],
        scratch_shapes=[pltpu.VMEM((N_SHUF, SUB_ROWS, LANES), F32)] * 2,
        compiler_params=pltpu.CompilerParams(
            dimension_semantics=("parallel", "parallel"), vmem_limit_bytes=VMEM_LIMIT),
        name="inproj",
    )(x, mod, g_mix, cos_t, sin_t, w_in, sgu_w, sgu_b)


def _attn_kernel(*refs, pieces, first, last, seq_len):
    T = pieces
    n_in = 7 + (0 if first else 2)
    q_ref, kp_ref, kc_ref, kn_ref, vp_ref, vc_ref, vn_ref = refs[:7]
    if not first:
        o_in, lse_in = refs[7:9]
    outs = refs[n_in:]
    if last:
        y_out, s_scr, o_nat, lse_nat, o_mid, lse_mid = outs
    else:
        o_out, lse_out, s_scr = outs
    tl = q_ref.shape[1]
    halo = BAND_HALF // T
    pq = Q_BLOCK // T
    pk = K_BLOCK // T
    tile = pl.program_id(2)

    if last:
        per_res = tl // FOLD
        for c in range(HEAD_PAIRS):
            cols = slice(c * LANES, (c + 1) * LANES)
            for src, mid, dst in ((o_in, o_mid, o_nat), (lse_in, lse_mid, lse_nat)):
                for r1 in range(4):
                    for r2 in range(4):
                        mid[c, r1, pl.ds(r2, per_res, stride=4), :] = (
                            src[4 * r2 + r1, :, cols].astype(F32))
                for r1 in range(4):
                    dst[c, pl.ds(r1, 4 * per_res, stride=4), :] = mid[c, r1]

    row = lax.broadcasted_iota(jnp.int32, (Q_BLOCK, K_BLOCK), 0)
    col = lax.broadcasted_iota(jnp.int32, (Q_BLOCK, K_BLOCK), 1)
    rel = T * (col % pk - halo - row % pq) + (col // pk - row // pq)
    band = jnp.abs(rel) <= BAND_HALF
    lane = lax.broadcasted_iota(jnp.int32, (1, LANES), 1)
    first_head = lane < HEAD_DIM
    zero = jnp.zeros((), BF16)
    ones_block = jnp.ones((K_BLOCK, LANES), BF16)

    def gather(ref, start, size, cols):
        parts = [ref[t, start:start + size, cols] for t in range(T)]
        return parts[0] if T == 1 else jnp.concatenate(parts, axis=0)

    def window(prev_ref, cur_ref, next_ref, la, cols):
        lo, hi = la - halo, la + pq + halo
        parts = []
        for t in range(T):
            if lo < 0:
                parts.append(prev_ref[t, :, cols])
            parts.append(cur_ref[t, max(lo, 0):min(hi, tl), cols])
            if hi > tl:
                parts.append(next_ref[t, :, cols])
        return parts[0] if len(parts) == 1 else jnp.concatenate(parts, axis=0)

    for a in range(tl // pq):
        la = a * pq
        slot = a % 2
        kidx = tile * tl + la - halo + col % pk
        valid = band & (kidx >= 0) & (kidx < seq_len)
        bias = jnp.where(valid, 0.0, MASK_VALUE)

        maxima = []
        for c in range(HEAD_PAIRS):
            cols = slice(c * LANES, (c + 1) * LANES)
            qp = gather(q_ref, la, pq, cols)
            kp = window(kp_ref, kc_ref, kn_ref, la, cols)
            q2 = jnp.concatenate([jnp.where(first_head, qp, zero),
                                  jnp.where(first_head, zero, qp)], axis=0)
            s = lax.dot_general(q2, kp, (((1,), (1,)), ((), ())), preferred_element_type=F32)
            s = s + jnp.concatenate([bias, bias], axis=0)
            s_scr[slot, c] = s
            maxima.append(jnp.max(s, axis=-1, keepdims=True))

        for c in range(HEAD_PAIRS):
            cols = slice(c * LANES, (c + 1) * LANES)
            vp = window(vp_ref, vc_ref, vn_ref, la, cols)
            p = jnp.exp2(s_scr[slot, c] - maxima[c]).astype(BF16)
            res = jnp.dot(p, jnp.concatenate([vp, ones_block], axis=1),
                          preferred_element_type=F32)
            pv = jnp.where(first_head, res[:Q_BLOCK, :LANES], res[Q_BLOCK:, :LANES])
            den = jnp.where(first_head, res[:Q_BLOCK, LANES:], res[Q_BLOCK:, LANES:])
            m = jnp.where(first_head, maxima[c][:Q_BLOCK], maxima[c][Q_BLOCK:])
            if first:
                o_new = pv / den
                lse_new = m + jnp.log2(den)
            else:
                if last:
                    o_prev = o_nat[c, la:la + pq, :]
                    lse_prev = lse_nat[c, la:la + pq, :]
                else:
                    o_prev = gather(o_in, la, pq, cols).astype(F32)
                    lse_prev = gather(lse_in, la, pq, cols)
                gap = lse_prev - m
                decay = jnp.exp2(-jnp.abs(gap))
                prev_is_top = gap > 0.0
                w_prev = jnp.where(prev_is_top, 1.0, decay)
                w_cur = jnp.where(prev_is_top, decay, 1.0)
                total = w_prev + w_cur * den
                o_new = (w_prev * o_prev + w_cur * pv) / total
                if not last:
                    lse_new = jnp.maximum(lse_prev, m) + jnp.log2(total)
            if last:
                y_out[la:la + pq, cols] = o_new.astype(BF16)
            else:
                for t in range(T):
                    o_out[t, la:la + pq, cols] = o_new[t * pq:(t + 1) * pq].astype(BF16)
                    lse_out[t, la:la + pq, cols] = lse_new[t * pq:(t + 1) * pq]


def _attn_pass(q, k, v, state, pieces, first, last):
    B, T, R, L, W = q.shape
    assert T == pieces
    tl = min(ATTN_TILE // T, L)
    halo = BAND_HALF // T
    assert L % tl == 0 and tl % (Q_BLOCK // T) == 0
    n_tiles = L // tl
    halo_per_tile = tl // halo
    n_halo = L // halo

    cur = lambda: pl.BlockSpec((None, T, None, tl, W), lambda b, r, i: (b, 0, r, i, 0))
    prev = lambda: pl.BlockSpec(
        (None, T, None, halo, W),
        lambda b, r, i: (b, 0, r, jnp.maximum(i * halo_per_tile - 1, 0), 0))
    nxt = lambda: pl.BlockSpec(
        (None, T, None, halo, W),
        lambda b, r, i: (b, 0, r, jnp.minimum((i + 1) * halo_per_tile, n_halo - 1), 0))
    in_specs = [cur(), prev(), cur(), nxt(), prev(), cur(), nxt()]
    args = [q, k, k, k, v, v, v]
    scratch = [pltpu.VMEM((2, HEAD_PAIRS, 2 * Q_BLOCK, K_BLOCK), F32)]
    if not first:
        if last:
            assert T == 1 and R == 1 and tl == ATTN_TILE
            st = lambda: pl.BlockSpec((None, FOLD, tl // FOLD, W), lambda b, r, i: (b, 0, i, 0))
            scratch += [pltpu.VMEM((HEAD_PAIRS, tl, LANES), F32)] * 2
            scratch += [pltpu.VMEM((HEAD_PAIRS, 4, tl // 4, LANES), F32)] * 2
        else:
            st = cur
        in_specs += [st(), st()]
        args += list(state)
    if last:
        out_specs = pl.BlockSpec((None, tl, W), lambda b, r, i: (b, i, 0))
        out_shape = jax.ShapeDtypeStruct((B, L, W), BF16)
    else:
        out_specs = [cur(), cur()]
        out_shape = [jax.ShapeDtypeStruct((B, T, R, L, W), BF16),
                     jax.ShapeDtypeStruct((B, T, R, L, W), F32)]
    return pl.pallas_call(
        functools.partial(_attn_kernel, pieces=T, first=first, last=last, seq_len=L),
        grid=(B, R, n_tiles),
        in_specs=in_specs,
        out_specs=out_specs,
        out_shape=out_shape,
        scratch_shapes=scratch,
        compiler_params=pltpu.CompilerParams(
            dimension_semantics=("parallel", "parallel", "parallel"),
            vmem_limit_bytes=VMEM_LIMIT),
        name=f"attn_t{T}_{'first' if first else ('last' if last else 'mid')}",
    )(*args)


def _dilated_attention(nat, folded):
    B, S, W = nat[0].shape
    l16 = S // FOLD
    view16 = [a.reshape(B, 1, FOLD, l16, W) for a in folded]
    state = _attn_pass(*view16, None, pieces=1, first=True, last=False)
    t4 = FOLD // MID_DILATION
    view4 = [a.reshape(B, t4, MID_DILATION, l16, W) for a in folded]
    state = [a.reshape(B, t4, MID_DILATION, l16, W) for a in state]
    state = _attn_pass(*view4, state, pieces=t4, first=False, last=False)
    view1 = [a.reshape(B, 1, 1, S, W) for a in nat]
    state = [a.reshape(B, FOLD, l16, W) for a in state]
    return _attn_pass(*view1, state, pieces=1, first=False, last=True)


def _pool_mixer(za_ref, zprev_ref, znext_ref, tile, n_tiles, seq_len):
    rows = za_ref.shape[0]
    za = za_ref[...]
    before = jnp.where(tile > 0, zprev_ref[...], 0.0)
    after = jnp.where(tile < n_tiles - 1, znext_ref[...], 0.0)
    ext = jnp.concatenate([before, za, after], axis=0)
    n_ext = rows + 2 * POOL_HALO
    back = lambda a, s: pltpu.roll(a, s, axis=0)
    fwd = lambda a, s: pltpu.roll(a, n_ext - s, axis=0)
    sums = [ext + back(ext, 1)]
    for step in (1, 2, 4):
        sums.append(back(sums[-1], step) + fwd(sums[-1], step))
    pos = tile * rows + lax.broadcasted_iota(jnp.int32, (rows, 1), 0)
    lane_group = lax.broadcasted_iota(jnp.int32, (1, POOL_WIDTH), 1) // POOL_GROUP_DIM
    mean = None
    for g, win in enumerate(POOL_WINDOWS):
        cnt = (jnp.minimum(pos + win // 2, seq_len) - jnp.maximum(pos - win // 2, 0)).astype(F32)
        m = sums[g][POOL_HALO:POOL_HALO + rows] / cnt
        mean = m if mean is None else jnp.where(lane_group == g, m, mean)
    return mean - za


def _outmlp_kernel(x_ref, mod_ref, za_ref, zprev_ref, znext_ref, yb_ref, yc_ref,
                   poolw_ref, pools_ref, wout_ref, gmlp_ref, wup_ref, wdown_ref, gfin_ref,
                   o_ref, *, seq_len, final):
    tile = pl.program_id(1)
    n_tiles = pl.num_programs(1)
    o1 = POOL_WIDTH
    o2 = o1 + ATTN_WIDTH
    gain = gmlp_ref[...] * (1.0 + mod_ref[4:5, :])
    p = _pool_mixer(za_ref, zprev_ref, znext_ref, tile, n_tiles, seq_len).astype(BF16)
    y = jnp.dot(yb_ref[...], wout_ref[o1:o2, :], preferred_element_type=F32)
    y = y + jnp.dot(yc_ref[...], wout_ref[o2:, :], preferred_element_type=F32)
    ya = jnp.dot(p, poolw_ref[...], preferred_element_type=F32) * pools_ref[...]
    y = y + jnp.dot(ya.astype(BF16), wout_ref[0:o1, :], preferred_element_type=F32)
    o_ref[...] = x_ref[...] + mod_ref[2:3, :] * y

    for st in range(x_ref.shape[0] // SUB_ROWS):
        rs = slice(st * SUB_ROWS, (st + 1) * SUB_ROWS)
        x1 = o_ref[rs, :]
        xn = x1 * lax.rsqrt(jnp.mean(x1 * x1, axis=-1, keepdims=True) + EPS)
        h = (xn * gain + mod_ref[3:4, :]).astype(BF16)
        acc = None
        for c in range(D_FF // FF_CHUNK):
            cs = slice(c * FF_CHUNK, (c + 1) * FF_CHUNK)
            up = jnp.dot(h, wup_ref[:, cs], preferred_element_type=F32)
            act = jnp.square(jnp.maximum(up, 0.0)).astype(BF16)
            part = jnp.dot(act, wdown_ref[cs, :], preferred_element_type=F32)
            acc = part if acc is None else acc + part
        x2 = x1 + mod_ref[5:6, :] * acc
        if final:
            x2 = x2 * lax.rsqrt(jnp.mean(x2 * x2, axis=-1, keepdims=True) + EPS) * gfin_ref[...]
        o_ref[rs, :] = x2


def _outmlp(x, mod, za, yb, yc, pool_w, pool_scale, w_out, g_mlp, w_up, w_down, g_final, final):
    B, S, _ = x.shape
    tm = ROW_TILE
    assert S % tm == 0
    n_tiles = S // tm
    halo_per_tile = tm // POOL_HALO
    n_halo = S // POOL_HALO
    tok = lambda width: pl.BlockSpec((None, tm, width), lambda b, i: (b, i, 0))
    return pl.pallas_call(
        functools.partial(_outmlp_kernel, seq_len=S, final=final),
        grid=(B, n_tiles),
        in_specs=[
            tok(D_MODEL),
            pl.BlockSpec((None, N_MOD, D_MODEL), lambda b, i: (b, 0, 0)),
            tok(POOL_WIDTH),
            pl.BlockSpec((None, POOL_HALO, POOL_WIDTH),
                         lambda b, i: (b, jnp.maximum(i * halo_per_tile - 1, 0), 0)),
            pl.BlockSpec((None, POOL_HALO, POOL_WIDTH),
                         lambda b, i: (b, jnp.minimum((i + 1) * halo_per_tile, n_halo - 1), 0)),
            tok(ATTN_WIDTH),
            tok(SGU_WIDTH),
            _const_spec((POOL_WIDTH, POOL_WIDTH)),
            _const_spec((1, POOL_WIDTH)),
            _const_spec((D_MODEL, D_MODEL)),
            _const_spec((1, D_MODEL)),
            _const_spec((D_MODEL, D_FF)),
            _const_spec((D_FF, D_MODEL)),
            _const_spec((1, D_MODEL)),
        ],
        out_specs=tok(D_MODEL),
        out_shape=jax.ShapeDtypeStruct((B, S, D_MODEL), F32),
        compiler_params=pltpu.CompilerParams(
            dimension_semantics=("parallel", "parallel"), vmem_limit_bytes=VMEM_LIMIT),
        name="outmlp",
    )(x, mod, za, za, za, yb, yc, pool_w, pool_scale, w_out, g_mlp, w_up, w_down, g_final)


def _rope_tables(seq_len):
    inv_freq = ROPE_THETA ** (-jnp.arange(0, ROT_DIM, 2, dtype=F32) / ROT_DIM)
    ang = jnp.arange(seq_len).astype(F32)[:, None] * inv_freq[None, :]
    cos, sin = jnp.cos(ang), jnp.sin(ang)
    pad = HEAD_DIM - ROT_DIM
    cos_head = jnp.concatenate([cos, cos, jnp.ones((seq_len, pad), F32)], axis=1)
    sin_head = jnp.concatenate([-sin, sin, jnp.zeros((seq_len, pad), F32)], axis=1)
    reps = LANES // HEAD_DIM
    return jnp.tile(cos_head, (1, reps)), jnp.tile(sin_head, (1, reps))


def _block_diag(blocks):
    g, n, m = blocks.shape
    eye = jnp.eye(g, dtype=blocks.dtype)
    return (eye[:, None, :, None] * blocks[:, :, None, :]).reshape(g * n, g * m)


def _prepare_layer(l, g_mix, g_mlp, w_in, pool_w, pool_scale, sgu_w, sgu_b, w_out, w_up, w_down):
    return dict(
        g_mix=g_mix[l].reshape(1, D_MODEL),
        g_mlp=g_mlp[l].reshape(1, D_MODEL),
        w_in=w_in[l].astype(BF16),
        pool_w=_block_diag(pool_w[l]).astype(BF16),
        pool_scale=pool_scale[l].reshape(1, POOL_WIDTH),
        sgu_w=sgu_w[l].reshape(SGU_GROUPS * SGU_CHUNK, SGU_CHUNK).astype(BF16),
        sgu_b=jnp.repeat(jnp.transpose(sgu_b[l]), SGU_GROUP_DIM, axis=1),
        w_out=w_out[l].astype(BF16),
        w_up=w_up[l].astype(BF16),
        w_down=w_down[l].astype(BF16),
    )


def _trunk(x, mods, layers, g_final):
    B, S, _ = x.shape
    cos_t, sin_t = _rope_tables(S)
    for l, lw in enumerate(layers):
        mod = mods[l].reshape(B, N_MOD, D_MODEL)
        za, q, k, v, qf, kf, vf, yc = _inproj(x, mod, lw["g_mix"], cos_t, sin_t, lw["w_in"],
                                              lw["sgu_w"], lw["sgu_b"])
        yb = _dilated_attention((q, k, v), (qf, kf, vf))
        x = _outmlp(x, mod, za, yb, yc, lw["pool_w"], lw["pool_scale"], lw["w_out"],
                    lw["g_mlp"], lw["w_up"], lw["w_down"], g_final, final=(l == DEPTH - 1))
    return x


def kernel(x_prompt, x_sample, c_prompt, c_sample, w_ada, b_ada, g_mix, g_mlp, w_in, pool_w,
           pool_scale, sgu_w, sgu_b, w_out, w_up, w_down, g_final):
    nb_p = c_prompt.shape[0]
    nb_s = c_sample.shape[0]
    rows = -(-(nb_p + nb_s) // 8) * 8
    c_all = jnp.concatenate(
        [c_prompt, c_sample, jnp.zeros((rows - nb_p - nb_s, D_MODEL), F32)], axis=0)
    mods = _ada_mod(c_all, w_ada, b_ada)
    layers = [_prepare_layer(l, g_mix, g_mlp, w_in, pool_w, pool_scale, sgu_w, sgu_b,
                             w_out, w_up, w_down) for l in range(DEPTH)]
    gf = g_final.reshape(1, D_MODEL)
    y_prompt = _trunk(x_prompt, mods[:, :nb_p], layers, gf)
    y_sample = _trunk(x_sample, mods[:, nb_p:nb_p + nb_s], layers, gf)
    return (y_prompt, y_sample)
```

```python
import functools

import jax
import jax.numpy as jnp
from jax import lax
from jax.experimental import pallas as pl
from jax.experimental.pallas import tpu as pltpu

F32 = jnp.float32
BF16 = jnp.bfloat16

D_MODEL = 1024
DEPTH = 2
HEAD_DIM = 64
POOL_WINDOWS = (2, 4, 8, 16)
POOL_WIDTH = D_MODEL // 4
POOL_GROUP_DIM = POOL_WIDTH // len(POOL_WINDOWS)
ATTN_WIDTH = D_MODEL // 2
BAND_HALF = 64
FOLD = 16
MID_DILATION = 4
ROT_DIM = HEAD_DIM // 4
ROPE_THETA = 500000.0
SGU_WIDTH = D_MODEL // 4
SGU_GROUPS = 4
SGU_GROUP_DIM = SGU_WIDTH // SGU_GROUPS
SGU_CHUNK = 128
D_FF = 4 * D_MODEL
N_MOD = 6
EPS = 1e-6
MASK_VALUE = -1e30
Q_SCALE = HEAD_DIM ** -0.5 * 1.4426950408889634

OFF_Q = POOL_WIDTH
OFF_K = OFF_Q + ATTN_WIDTH
OFF_V = OFF_K + ATTN_WIDTH
OFF_C = OFF_V + ATTN_WIDTH
PROJ_WIDTH = OFF_C + 2 * SGU_WIDTH

LANES = 128
HEAD_PAIRS = ATTN_WIDTH // LANES
POOL_HALO = 8
ROW_TILE = 512
ATTN_TILE = 512
Q_BLOCK = 128
K_BLOCK = Q_BLOCK + 2 * BAND_HALF
FF_CHUNK = 1024
SUB_ROWS = 256
N_SHUF = 6
VMEM_LIMIT = 56 * 1024 * 1024


def _const_spec(shape):
    nd = len(shape)
    return pl.BlockSpec(shape, lambda *_: (0,) * nd, pipeline_mode=pl.Buffered(1))


def _ada_kernel(c_ref, w_ref, b_ref, o_ref):
    c = c_ref[...]
    act = c * (1.0 / (1.0 + jnp.exp(-c)))
    o_ref[...] = jnp.dot(act.astype(BF16), w_ref[...].astype(BF16),
                         preferred_element_type=F32) + b_ref[...]


def _ada_mod(c_all, w_ada, b_ada):
    rows = c_all.shape[0]
    return pl.pallas_call(
        _ada_kernel,
        grid=(DEPTH, N_MOD),
        in_specs=[
            pl.BlockSpec((rows, D_MODEL), lambda l, j: (0, 0)),
            pl.BlockSpec((None, D_MODEL, D_MODEL), lambda l, j: (l, 0, j)),
            pl.BlockSpec((None, 1, D_MODEL), lambda l, j: (l, 0, j)),
        ],
        out_specs=pl.BlockSpec((None, rows, D_MODEL), lambda l, j: (l, 0, j)),
        out_shape=jax.ShapeDtypeStruct((DEPTH, rows, N_MOD * D_MODEL), F32),
        compiler_params=pltpu.CompilerParams(vmem_limit_bytes=VMEM_LIMIT),
        name="ada_mod",
    )(c_all, w_ada, b_ada.reshape(DEPTH, 1, N_MOD * D_MODEL))


def _rope_cols(z, cos, sin, low_half):
    up = pltpu.roll(z, LANES - ROT_DIM // 2, axis=1)
    down = pltpu.roll(z, ROT_DIM // 2, axis=1)
    return z * cos + jnp.where(low_half, up, down) * sin


def _store_both(val, nat_ref, fold_ref, scr_a, scr_b, row0, c):
    sub = val.shape[0]
    cols = slice(c * LANES, (c + 1) * LANES)
    nat_ref[row0:row0 + sub, cols] = val.astype(BF16)
    scr_a[...] = val
    quarter = sub // 4
    for r1 in range(4):
        scr_b[r1 * quarter:(r1 + 1) * quarter, :] = scr_a[pl.ds(r1, quarter, stride=4), :]
    per_res = sub // FOLD
    f0 = row0 // FOLD
    for r1 in range(4):
        for r2 in range(4):
            piece = scr_b[pl.ds(r1 * quarter + r2, per_res, stride=4), :]
            fold_ref[4 * r2 + r1, f0:f0 + per_res, cols] = piece.astype(BF16)


def _group_sums(v, ones_bd):
    hi = v.astype(BF16)
    lo = (v - hi.astype(F32)).astype(BF16)
    return (jnp.dot(hi, ones_bd, preferred_element_type=F32)
            + jnp.dot(lo, ones_bd, preferred_element_type=F32))


def _inproj_kernel(x_ref, mod_ref, g_ref, cos_ref, sin_ref, w_ref, sguw_ref, sgub_ref,
                   za_ref, q_ref, k_ref, v_ref, qf_ref, kf_ref, vf_ref, yc_ref, scr_a, scr_b):
    rows = x_ref.shape[0]
    lane = lax.broadcasted_iota(jnp.int32, (1, LANES), 1)
    low_half = (lane % HEAD_DIM) < (ROT_DIM // 2)
    gi = lax.broadcasted_iota(jnp.int32, (SGU_WIDTH, SGU_WIDTH), 0) // SGU_GROUP_DIM
    gj = lax.broadcasted_iota(jnp.int32, (SGU_WIDTH, SGU_WIDTH), 1) // SGU_GROUP_DIM
    ones_bd = jnp.where(gi == gj, 1.0, 0.0).astype(BF16)
    lane_group = lax.broadcasted_iota(jnp.int32, (1, SGU_WIDTH), 1) // SGU_GROUP_DIM
    bias = sgub_ref[...]
    gain = g_ref[...] * (1.0 + mod_ref[1:2, :])
    shift = mod_ref[0:1, :]

    for st in range(rows // SUB_ROWS):
        row0 = st * SUB_ROWS
        rs = slice(row0, row0 + SUB_ROWS)
        x = x_ref[rs, :]
        xn = x * lax.rsqrt(jnp.mean(x * x, axis=-1, keepdims=True) + EPS)
        h = (xn * gain + shift).astype(BF16)

        cos = cos_ref[rs, :]
        sin = sin_ref[rs, :]
        gate = jax.nn.gelu(jnp.dot(h, w_ref[:, OFF_C:PROJ_WIDTH], preferred_element_type=F32))
        zq = jnp.dot(h, w_ref[:, OFF_Q:OFF_K], preferred_element_type=F32)
        u = gate[:, :SGU_WIDTH]
        v = gate[:, SGU_WIDTH:]
        dv = v - _group_sums(v, ones_bd) * (1.0 / SGU_GROUP_DIM)
        zk = jnp.dot(h, w_ref[:, OFF_K:OFF_V], preferred_element_type=F32)
        var = _group_sums(dv * dv, ones_bd) * (1.0 / SGU_GROUP_DIM)
        zv = jnp.dot(h, w_ref[:, OFF_V:OFF_C], preferred_element_type=F32)
        vn = (dv * lax.rsqrt(var + EPS)).astype(BF16)
        za_ref[rs, :] = jnp.dot(h, w_ref[:, 0:OFF_Q], preferred_element_type=F32)

        for ch in range(SUB_ROWS // SGU_CHUNK):
            sl = slice(ch * SGU_CHUNK, (ch + 1) * SGU_CHUNK)
            mixed = jnp.dot(sguw_ref[...], vn[sl], preferred_element_type=F32)
            vm = mixed[0:SGU_CHUNK]
            for g in range(1, SGU_GROUPS):
                vm = jnp.where(lane_group == g, mixed[g * SGU_CHUNK:(g + 1) * SGU_CHUNK], vm)
            yc_ref[row0 + ch * SGU_CHUNK:row0 + (ch + 1) * SGU_CHUNK, :] = (
                u[sl] * (vm + bias)).astype(BF16)

        for c in range(HEAD_PAIRS):
            slot = (st * HEAD_PAIRS + c) * 3
            cols = slice(c * LANES, (c + 1) * LANES)
            _store_both(_rope_cols(zq[:, cols], cos, sin, low_half) * Q_SCALE,
                        q_ref, qf_ref, scr_a.at[slot % N_SHUF], scr_b.at[slot % N_SHUF], row0, c)
            _store_both(_rope_cols(zk[:, cols], cos, sin, low_half), k_ref, kf_ref,
                        scr_a.at[(slot + 1) % N_SHUF], scr_b.at[(slot + 1) % N_SHUF], row0, c)
            _store_both(zv[:, cols], v_ref, vf_ref,
                        scr_a.at[(slot + 2) % N_SHUF], scr_b.at[(slot + 2) % N_SHUF], row0, c)


def _inproj(x, mod, g_mix, cos_t, sin_t, w_in, sgu_w, sgu_b):
    B, S, _ = x.shape
    tm = ROW_TILE
    assert S % tm == 0 and tm % SUB_ROWS == 0 and SUB_ROWS % (FOLD * 16) == 0
    grid = (B, S // tm)
    tok = lambda width: pl.BlockSpec((None, tm, width), lambda b, i: (b, i, 0))
    folded = pl.BlockSpec((None, FOLD, tm // FOLD, ATTN_WIDTH), lambda b, i: (b, 0, i, 0))
    nat_shape = jax.ShapeDtypeStruct((B, S, ATTN_WIDTH), BF16)
    fold_shape = jax.ShapeDtypeStruct((B, FOLD, S // FOLD, ATTN_WIDTH), BF16)
    return pl.pallas_call(
        _inproj_kernel,
        grid=grid,
        in_specs=[
            tok(D_MODEL),
            pl.BlockSpec((None, N_MOD, D_MODEL), lambda b, i: (b, 0, 0)),
            _const_spec((1, D_MODEL)),
            pl.BlockSpec((tm, LANES), lambda b, i: (i, 0)),
            pl.BlockSpec((tm, LANES), lambda b, i: (i, 0)),
            _const_spec((D_MODEL, PROJ_WIDTH)),
            _const_spec((SGU_GROUPS * SGU_CHUNK, SGU_CHUNK)),
            _const_spec((SGU_CHUNK, SGU_WIDTH)),
        ],
        out_specs=[tok(POOL_WIDTH), tok(ATTN_WIDTH), tok(ATTN_WIDTH), tok(ATTN_WIDTH),
                   folded, folded, folded, tok(SGU_WIDTH)],
        out_shape=[jax.ShapeDtypeStruct((B, S, POOL_WIDTH), F32),
                   nat_shape, nat_shape, nat_shape, fold_shape, fold_shape, fold_shape,
                   jax.ShapeDtypeStruct((B, S, SGU_WIDTH), BF16)],
        scratch_shapes=[pltpu.VMEM((N_SHUF, SUB_ROWS, LANES), F32)] * 2,
        compiler_params=pltpu.CompilerParams(
            dimension_semantics=("parallel", "parallel"), vmem_limit_bytes=VMEM_LIMIT),
        name="inproj",
    )(x, mod, g_mix, cos_t, sin_t, w_in, sgu_w, sgu_b)


def _attn_kernel(*refs, pieces, first, last, seq_len):
    T = pieces
    n_in = 7 + (0 if first else 2)
    q_ref, kp_ref, kc_ref, kn_ref, vp_ref, vc_ref, vn_ref = refs[:7]
    if not first:
        o_in, lse_in = refs[7:9]
    outs = refs[n_in:]
    if last:
        y_out, s_scr, o_nat, lse_nat, o_mid, lse_mid = outs
    else:
        o_out, lse_out, s_scr = outs
    tl = q_ref.shape[1]
    halo = BAND_HALF // T
    pq = Q_BLOCK // T
    pk = K_BLOCK // T
    tile = pl.program_id(2)

    if last:
        per_res = tl // FOLD
        for c in range(HEAD_PAIRS):
            cols = slice(c * LANES, (c + 1) * LANES)
            for src, mid, dst in ((o_in, o_mid, o_nat), (lse_in, lse_mid, lse_nat)):
                for r1 in range(4):
                    for r2 in range(4):
                        mid[c, r1, pl.ds(r2, per_res, stride=4), :] = (
                            src[4 * r2 + r1, :, cols].astype(F32))
                for r1 in range(4):
                    dst[c, pl.ds(r1, 4 * per_res, stride=4), :] = mid[c, r1]

    row = lax.broadcasted_iota(jnp.int32, (Q_BLOCK, K_BLOCK), 0)
    col = lax.broadcasted_iota(jnp.int32, (Q_BLOCK, K_BLOCK), 1)
    rel = T * (col % pk - halo - row % pq) + (col // pk - row // pq)
    band = jnp.abs(rel) <= BAND_HALF
    lane = lax.broadcasted_iota(jnp.int32, (1, LANES), 1)
    first_head = lane < HEAD_DIM
    zero = jnp.zeros((), BF16)
    ones_block = jnp.ones((K_BLOCK, LANES), BF16)

    def gather(ref, start, size, cols):
        parts = [ref[t, start:start + size, cols] for t in range(T)]
        return parts[0] if T == 1 else jnp.concatenate(parts, axis=0)

    def window(prev_ref, cur_ref, next_ref, la, cols):
        lo, hi = la - halo, la + pq + halo
        parts = []
        for t in range(T):
            if lo < 0:
                parts.append(prev_ref[t, :, cols])
            parts.append(cur_ref[t, max(lo, 0):min(hi, tl), cols])
            if hi > tl:
                parts.append(next_ref[t, :, cols])
        return parts[0] if len(parts) == 1 else jnp.concatenate(parts, axis=0)

    def block_bias(a):
        kidx = tile * tl + a * pq - halo + col % pk
        valid = band & (kidx >= 0) & (kidx < seq_len)
        return jnp.where(valid, 0.0, MASK_VALUE)

    def scores(a, c, bias):
        la = a * pq
        cols = slice(c * LANES, (c + 1) * LANES)
        qp = gather(q_ref, la, pq, cols)
        kp = window(kp_ref, kc_ref, kn_ref, la, cols)
        q2 = jnp.concatenate([jnp.where(first_head, qp, zero),
                              jnp.where(first_head, zero, qp)], axis=0)
        s = lax.dot_general(q2, kp, (((1,), (1,)), ((), ())), preferred_element_type=F32)
        s = s + jnp.concatenate([bias, bias], axis=0)
        s_scr[a % 2, c] = s
        return jnp.max(s, axis=-1, keepdims=True)

    def values(a, c, row_max):
        la = a * pq
        cols = slice(c * LANES, (c + 1) * LANES)
        vp = window(vp_ref, vc_ref, vn_ref, la, cols)
        p = jnp.exp2(s_scr[a % 2, c] - row_max).astype(BF16)
        res = jnp.dot(p, jnp.concatenate([vp, ones_block], axis=1),
                      preferred_element_type=F32)
        pv = jnp.where(first_head, res[:Q_BLOCK, :LANES], res[Q_BLOCK:, :LANES])
        den = jnp.where(first_head, res[:Q_BLOCK, LANES:], res[Q_BLOCK:, LANES:])
        m = jnp.where(first_head, row_max[:Q_BLOCK], row_max[Q_BLOCK:])
        if first:
            o_new = pv / den
            lse_new = m + jnp.log2(den)
        else:
            if last:
                o_prev = o_nat[c, la:la + pq, :]
                lse_prev = lse_nat[c, la:la + pq, :]
            else:
                o_prev = gather(o_in, la, pq, cols).astype(F32)
                lse_prev = gather(lse_in, la, pq, cols)
            top = jnp.maximum(lse_prev, m)
            w_prev = jnp.exp2(lse_prev - top)
            w_cur = jnp.exp2(m - top)
            total = w_prev + w_cur * den
            o_new = (w_prev * o_prev + w_cur * pv) / total
            if not last:
                lse_new = top + jnp.log2(total)
        if last:
            y_out[la:la + pq, cols] = o_new.astype(BF16)
        else:
            for t in range(T):
                o_out[t, la:la + pq, cols] = o_new[t * pq:(t + 1) * pq].astype(BF16)
                lse_out[t, la:la + pq, cols] = lse_new[t * pq:(t + 1) * pq]

    n_blocks = tl // pq
    skew = 1 if first else 0
    row_maxima = {}
    for a in range(n_blocks + skew):
        bias = block_bias(a) if a < n_blocks else None
        for c in range(HEAD_PAIRS):
            if a < n_blocks:
                row_maxima[a, c] = scores(a, c, bias)
            if skew and a >= 1:
                values(a - 1, c, row_maxima.pop((a - 1, c)))
        if not skew:
            for c in range(HEAD_PAIRS):
                values(a, c, row_maxima.pop((a, c)))


def _attn_pass(q, k, v, state, pieces, first, last):
    B, T, R, L, W = q.shape
    assert T == pieces
    tl = min(ATTN_TILE // T, L)
    halo = BAND_HALF // T
    assert L % tl == 0 and tl % (Q_BLOCK // T) == 0
    n_tiles = L // tl
    halo_per_tile = tl // halo
    n_halo = L // halo

    cur = lambda: pl.BlockSpec((None, T, None, tl, W), lambda b, r, i: (b, 0, r, i, 0))
    prev = lambda: pl.BlockSpec(
        (None, T, None, halo, W),
        lambda b, r, i: (b, 0, r, jnp.maximum(i * halo_per_tile - 1, 0), 0))
    nxt = lambda: pl.BlockSpec(
        (None, T, None, halo, W),
        lambda b, r, i: (b, 0, r, jnp.minimum((i + 1) * halo_per_tile, n_halo - 1), 0))
    in_specs = [cur(), prev(), cur(), nxt(), prev(), cur(), nxt()]
    args = [q, k, k, k, v, v, v]
    scratch = [pltpu.VMEM((2, HEAD_PAIRS, 2 * Q_BLOCK, K_BLOCK), F32)]
    if not first:
        if last:
            assert T == 1 and R == 1 and tl == ATTN_TILE
            st = lambda: pl.BlockSpec((None, FOLD, tl // FOLD, W), lambda b, r, i: (b, 0, i, 0))
            scratch += [pltpu.VMEM((HEAD_PAIRS, tl, LANES), F32)] * 2
            scratch += [pltpu.VMEM((HEAD_PAIRS, 4, tl // 4, LANES), F32)] * 2
        else:
            st = cur
        in_specs += [st(), st()]
        args += list(state)
    if last:
        out_specs = pl.BlockSpec((None, tl, W), lambda b, r, i: (b, i, 0))
        out_shape = jax.ShapeDtypeStruct((B, L, W), BF16)
    else:
        out_specs = [cur(), cur()]
        out_shape = [jax.ShapeDtypeStruct((B, T, R, L, W), BF16),
                     jax.ShapeDtypeStruct((B, T, R, L, W), F32)]
    return pl.pallas_call(
        functools.partial(_attn_kernel, pieces=T, first=first, last=last, seq_len=L),
        grid=(B, R, n_tiles),
        in_specs=in_specs,
        out_specs=out_specs,
        out_shape=out_shape,
        scratch_shapes=scratch,
        compiler_params=pltpu.CompilerParams(
            dimension_semantics=("parallel", "parallel", "parallel"),
            vmem_limit_bytes=VMEM_LIMIT),
        name=f"attn_t{T}_{'first' if first else ('last' if last else 'mid')}",
    )(*args)


def _dilated_attention(nat, folded):
    B, S, W = nat[0].shape
    l16 = S // FOLD
    view16 = [a.reshape(B, 1, FOLD, l16, W) for a in folded]
    state = _attn_pass(*view16, None, pieces=1, first=True, last=False)
    t4 = FOLD // MID_DILATION
    view4 = [a.reshape(B, t4, MID_DILATION, l16, W) for a in folded]
    state = [a.reshape(B, t4, MID_DILATION, l16, W) for a in state]
    state = _attn_pass(*view4, state, pieces=t4, first=False, last=False)
    view1 = [a.reshape(B, 1, 1, S, W) for a in nat]
    state = [a.reshape(B, FOLD, l16, W) for a in state]
    return _attn_pass(*view1, state, pieces=1, first=False, last=True)


def _pool_mixer(za_ref, zprev_ref, znext_ref, tile, n_tiles, seq_len):
    rows = za_ref.shape[0]
    za = za_ref[...]
    before = jnp.where(tile > 0, zprev_ref[...], 0.0)
    after = jnp.where(tile < n_tiles - 1, znext_ref[...], 0.0)
    ext = jnp.concatenate([before, za, after], axis=0)
    n_ext = rows + 2 * POOL_HALO
    back = lambda a, s: pltpu.roll(a, s, axis=0)
    fwd = lambda a, s: pltpu.roll(a, n_ext - s, axis=0)
    sums = [ext + back(ext, 1)]
    for step in (1, 2, 4):
        sums.append(back(sums[-1], step) + fwd(sums[-1], step))
    lane_group = lax.broadcasted_iota(jnp.int32, (1, POOL_WIDTH), 1) // POOL_GROUP_DIM
    win_sum = sums[0][POOL_HALO:POOL_HALO + rows]
    half = jnp.full((1, POOL_WIDTH), POOL_WINDOWS[0] // 2, jnp.int32)
    for g in range(1, len(POOL_WINDOWS)):
        win_sum = jnp.where(lane_group == g, sums[g][POOL_HALO:POOL_HALO + rows], win_sum)
        half = jnp.where(lane_group == g, POOL_WINDOWS[g] // 2, half)

    def inv_count(row0):
        pos = tile * rows + row0 + lax.broadcasted_iota(jnp.int32, (POOL_HALO, 1), 0)
        cnt = jnp.minimum(pos + half, seq_len) - jnp.maximum(pos - half, 0)
        return 1.0 / cnt.astype(F32)

    edge = POOL_HALO
    mean = jnp.concatenate([
        win_sum[:edge] * inv_count(0),
        win_sum[edge:rows - edge] * (0.5 / half.astype(F32)),
        win_sum[rows - edge:] * inv_count(rows - edge)], axis=0)
    return mean - za


def _outmlp_kernel(x_ref, mod_ref, za_ref, zprev_ref, znext_ref, yb_ref, yc_ref,
                   poolw_ref, pools_ref, wout_ref, gmlp_ref, wup_ref, wdown_ref, gfin_ref,
                   o_ref, *, seq_len, final):
    tile = pl.program_id(1)
    n_tiles = pl.num_programs(1)
    o1 = POOL_WIDTH
    o2 = o1 + ATTN_WIDTH
    gain = gmlp_ref[...] * (1.0 + mod_ref[4:5, :])
    p = _pool_mixer(za_ref, zprev_ref, znext_ref, tile, n_tiles, seq_len).astype(BF16)
    y = jnp.dot(yb_ref[...], wout_ref[o1:o2, :], preferred_element_type=F32)
    y = y + jnp.dot(yc_ref[...], wout_ref[o2:, :], preferred_element_type=F32)
    ya = jnp.dot(p, poolw_ref[...], preferred_element_type=F32) * pools_ref[...]
    y = y + jnp.dot(ya.astype(BF16), wout_ref[0:o1, :], preferred_element_type=F32)
    o_ref[...] = x_ref[...] + mod_ref[2:3, :] * y

    for st in range(x_ref.shape[0] // SUB_ROWS):
        rs = slice(st * SUB_ROWS, (st + 1) * SUB_ROWS)
        x1 = o_ref[rs, :]
        xn = x1 * lax.rsqrt(jnp.mean(x1 * x1, axis=-1, keepdims=True) + EPS)
        h = (xn * gain + mod_ref[3:4, :]).astype(BF16)
        acc = None
        for c in range(D_FF // FF_CHUNK):
            cs = slice(c * FF_CHUNK, (c + 1) * FF_CHUNK)
            up = jnp.dot(h, wup_ref[:, cs], preferred_element_type=F32)
            act = jnp.square(jnp.maximum(up, 0.0)).astype(BF16)
            part = jnp.dot(act, wdown_ref[cs, :], preferred_element_type=F32)
            acc = part if acc is None else acc + part
        x2 = x1 + mod_ref[5:6, :] * acc
        if final:
            x2 = x2 * lax.rsqrt(jnp.mean(x2 * x2, axis=-1, keepdims=True) + EPS) * gfin_ref[...]
        o_ref[rs, :] = x2


def _outmlp(x, mod, za, yb, yc, pool_w, pool_scale, w_out, g_mlp, w_up, w_down, g_final, final):
    B, S, _ = x.shape
    tm = ROW_TILE
    assert S % tm == 0
    n_tiles = S // tm
    halo_per_tile = tm // POOL_HALO
    n_halo = S // POOL_HALO
    tok = lambda width: pl.BlockSpec((None, tm, width), lambda b, i: (b, i, 0))
    return pl.pallas_call(
        functools.partial(_outmlp_kernel, seq_len=S, final=final),
        grid=(B, n_tiles),
        in_specs=[
            tok(D_MODEL),
            pl.BlockSpec((None, N_MOD, D_MODEL), lambda b, i: (b, 0, 0)),
            tok(POOL_WIDTH),
            pl.BlockSpec((None, POOL_HALO, POOL_WIDTH),
                         lambda b, i: (b, jnp.maximum(i * halo_per_tile - 1, 0), 0)),
            pl.BlockSpec((None, POOL_HALO, POOL_WIDTH),
                         lambda b, i: (b, jnp.minimum((i + 1) * halo_per_tile, n_halo - 1), 0)),
            tok(ATTN_WIDTH),
            tok(SGU_WIDTH),
            _const_spec((POOL_WIDTH, POOL_WIDTH)),
            _const_spec((1, POOL_WIDTH)),
            _const_spec((D_MODEL, D_MODEL)),
            _const_spec((1, D_MODEL)),
            _const_spec((D_MODEL, D_FF)),
            _const_spec((D_FF, D_MODEL)),
            _const_spec((1, D_MODEL)),
        ],
        out_specs=tok(D_MODEL),
        out_shape=jax.ShapeDtypeStruct((B, S, D_MODEL), F32),
        compiler_params=pltpu.CompilerParams(
            dimension_semantics=("parallel", "parallel"), vmem_limit_bytes=VMEM_LIMIT),
        name="outmlp",
    )(x, mod, za, za, za, yb, yc, pool_w, pool_scale, w_out, g_mlp, w_up, w_down, g_final)


def _rope_tables(seq_len):
    inv_freq = ROPE_THETA ** (-jnp.arange(0, ROT_DIM, 2, dtype=F32) / ROT_DIM)
    ang = jnp.arange(seq_len).astype(F32)[:, None] * inv_freq[None, :]
    cos, sin = jnp.cos(ang), jnp.sin(ang)
    pad = HEAD_DIM - ROT_DIM
    cos_head = jnp.concatenate([cos, cos, jnp.ones((seq_len, pad), F32)], axis=1)
    sin_head = jnp.concatenate([-sin, sin, jnp.zeros((seq_len, pad), F32)], axis=1)
    reps = LANES // HEAD_DIM
    return jnp.tile(cos_head, (1, reps)), jnp.tile(sin_head, (1, reps))


def _block_diag(blocks):
    g, n, m = blocks.shape
    eye = jnp.eye(g, dtype=blocks.dtype)
    return (eye[:, None, :, None] * blocks[:, :, None, :]).reshape(g * n, g * m)


def _prepare_layer(l, g_mix, g_mlp, w_in, pool_w, pool_scale, sgu_w, sgu_b, w_out, w_up, w_down):
    return dict(
        g_mix=g_mix[l].reshape(1, D_MODEL),
        g_mlp=g_mlp[l].reshape(1, D_MODEL),
        w_in=w_in[l].astype(BF16),
        pool_w=_block_diag(pool_w[l]).astype(BF16),
        pool_scale=pool_scale[l].reshape(1, POOL_WIDTH),
        sgu_w=sgu_w[l].reshape(SGU_GROUPS * SGU_CHUNK, SGU_CHUNK).astype(BF16),
        sgu_b=jnp.repeat(jnp.transpose(sgu_b[l]), SGU_GROUP_DIM, axis=1),
        w_out=w_out[l].astype(BF16),
        w_up=w_up[l].astype(BF16),
        w_down=w_down[l].astype(BF16),
    )


def _trunk(x, mods, layers, g_final):
    B, S, _ = x.shape
    cos_t, sin_t = _rope_tables(S)
    for l, lw in enumerate(layers):
        mod = mods[l].reshape(B, N_MOD, D_MODEL)
        za, q, k, v, qf, kf, vf, yc = _inproj(x, mod, lw["g_mix"], cos_t, sin_t, lw["w_in"],
                                              lw["sgu_w"], lw["sgu_b"])
        yb = _dilated_attention((q, k, v), (qf, kf, vf))
        x = _outmlp(x, mod, za, yb, yc, lw["pool_w"], lw["pool_scale"], lw["w_out"],
                    lw["g_mlp"], lw["w_up"], lw["w_down"], g_final, final=(l == DEPTH - 1))
    return x


def kernel(x_prompt, x_sample, c_prompt, c_sample, w_ada, b_ada, g_mix, g_mlp, w_in, pool_w,
           pool_scale, sgu_w, sgu_b, w_out, w_up, w_down, g_final):
    nb_p = c_prompt.shape[0]
    nb_s = c_sample.shape[0]
    rows = -(-(nb_p + nb_s) // 8) * 8
    c_all = jnp.concatenate(
        [c_prompt, c_sample, jnp.zeros((rows - nb_p - nb_s, D_MODEL), F32)], axis=0)
    mods = _ada_mod(c_all, w_ada, b_ada)
    layers = [_prepare_layer(l, g_mix, g_mlp, w_in, pool_w, pool_scale, sgu_w, sgu_b,
                             w_out, w_up, w_down) for l in range(DEPTH)]
    gf = g_final.reshape(1, D_MODEL)
    y_prompt = _trunk(x_prompt, mods[:, :nb_p], layers, gf)
    y_sample = _trunk(x_sample, mods[:, nb_p:nb_p + nb_s], layers, gf)
    return (y_prompt, y_sample)
```

```python
import functools

import jax
import jax.numpy as jnp
from jax import lax
from jax.experimental import pallas as pl
from jax.experimental.pallas import tpu as pltpu

F32 = jnp.float32
BF16 = jnp.bfloat16

D_MODEL = 1024
DEPTH = 2
HEAD_DIM = 64
POOL_WINDOWS = (2, 4, 8, 16)
POOL_WIDTH = D_MODEL // 4
POOL_GROUP_DIM = POOL_WIDTH // len(POOL_WINDOWS)
ATTN_WIDTH = D_MODEL // 2
BAND_HALF = 64
FOLD = 16
MID_DILATION = 4
ROT_DIM = HEAD_DIM // 4
ROPE_THETA = 500000.0
SGU_WIDTH = D_MODEL // 4
SGU_GROUPS = 4
SGU_GROUP_DIM = SGU_WIDTH // SGU_GROUPS
SGU_CHUNK = 128
D_FF = 4 * D_MODEL
N_MOD = 6
EPS = 1e-6
MASK_VALUE = -1e30
Q_SCALE = HEAD_DIM ** -0.5 * 1.4426950408889634

OFF_Q = POOL_WIDTH
OFF_K = OFF_Q + ATTN_WIDTH
OFF_V = OFF_K + ATTN_WIDTH
OFF_C = OFF_V + ATTN_WIDTH
PROJ_WIDTH = OFF_C + 2 * SGU_WIDTH

LANES = 128
HEAD_PAIRS = ATTN_WIDTH // LANES
POOL_HALO = 8
ROW_TILE = 512
INPROJ_TILE = 1024
ATTN_TILE = 1024
Q_BLOCK = 128
K_BLOCK = Q_BLOCK + 2 * BAND_HALF
FF_CHUNK = 1024
SUB_ROWS = 256
N_SHUF = 6
VMEM_LIMIT = 56 * 1024 * 1024


def _const_spec(shape):
    nd = len(shape)
    return pl.BlockSpec(shape, lambda *_: (0,) * nd, pipeline_mode=pl.Buffered(1))


def _ada_kernel(c_ref, w_ref, b_ref, o_ref):
    c = c_ref[...]
    act = c * (1.0 / (1.0 + jnp.exp(-c)))
    o_ref[...] = jnp.dot(act.astype(BF16), w_ref[...].astype(BF16),
                         preferred_element_type=F32) + b_ref[...]


def _ada_mod(c_all, w_ada, b_ada):
    rows = c_all.shape[0]
    return pl.pallas_call(
        _ada_kernel,
        grid=(DEPTH, N_MOD),
        in_specs=[
            pl.BlockSpec((rows, D_MODEL), lambda l, j: (0, 0)),
            pl.BlockSpec((None, D_MODEL, D_MODEL), lambda l, j: (l, 0, j)),
            pl.BlockSpec((None, 1, D_MODEL), lambda l, j: (l, 0, j)),
        ],
        out_specs=pl.BlockSpec((None, rows, D_MODEL), lambda l, j: (l, 0, j)),
        out_shape=jax.ShapeDtypeStruct((DEPTH, rows, N_MOD * D_MODEL), F32),
        compiler_params=pltpu.CompilerParams(vmem_limit_bytes=VMEM_LIMIT),
        name="ada_mod",
    )(c_all, w_ada, b_ada.reshape(DEPTH, 1, N_MOD * D_MODEL))


def _rope_cols(z, cos, sin, low_half):
    up = pltpu.roll(z, LANES - ROT_DIM // 2, axis=1)
    down = pltpu.roll(z, ROT_DIM // 2, axis=1)
    return z * cos + jnp.where(low_half, up, down) * sin


def _store_both(val, nat_ref, fold_ref, scr_a, scr_b, row0, col0):
    sub = val.shape[0]
    cols = slice(col0, col0 + LANES)
    nat_ref[row0:row0 + sub, cols] = val.astype(BF16)
    scr_a[...] = val
    quarter = sub // 4
    for r1 in range(4):
        scr_b[r1 * quarter:(r1 + 1) * quarter, :] = scr_a[pl.ds(r1, quarter, stride=4), :]
    per_res = sub // FOLD
    f0 = row0 // FOLD
    for r1 in range(4):
        for r2 in range(4):
            piece = scr_b[pl.ds(r1 * quarter + r2, per_res, stride=4), :]
            fold_ref[4 * r2 + r1, f0:f0 + per_res, cols] = piece.astype(BF16)


def _group_sums(v, ones_bd):
    hi = v.astype(BF16)
    lo = (v - hi.astype(F32)).astype(BF16)
    return (jnp.dot(hi, ones_bd, preferred_element_type=F32)
            + jnp.dot(lo, ones_bd, preferred_element_type=F32))


def _inproj_kernel(x_ref, mod_ref, g_ref, cos_ref, sin_ref, w_ref, sguw_ref, sgub_ref,
                   za_ref, qkv_ref, qkvf_ref, yc_ref, scr_a, scr_b):
    rows = x_ref.shape[0]
    lane = lax.broadcasted_iota(jnp.int32, (1, LANES), 1)
    low_half = (lane % HEAD_DIM) < (ROT_DIM // 2)
    gi = lax.broadcasted_iota(jnp.int32, (SGU_WIDTH, SGU_WIDTH), 0) // SGU_GROUP_DIM
    gj = lax.broadcasted_iota(jnp.int32, (SGU_WIDTH, SGU_WIDTH), 1) // SGU_GROUP_DIM
    ones_bd = jnp.where(gi == gj, 1.0, 0.0).astype(BF16)
    lane_group = lax.broadcasted_iota(jnp.int32, (1, SGU_WIDTH), 1) // SGU_GROUP_DIM
    bias = sgub_ref[...]
    gain = g_ref[...] * (1.0 + mod_ref[1:2, :])
    shift = mod_ref[0:1, :]

    for st in range(rows // SUB_ROWS):
        row0 = st * SUB_ROWS
        rs = slice(row0, row0 + SUB_ROWS)
        x = x_ref[rs, :]
        xn = x * lax.rsqrt(jnp.mean(x * x, axis=-1, keepdims=True) + EPS)
        h = (xn * gain + shift).astype(BF16)

        cos = cos_ref[rs, :]
        sin = sin_ref[rs, :]
        gate = jax.nn.gelu(jnp.dot(h, w_ref[:, OFF_C:PROJ_WIDTH], preferred_element_type=F32))
        zq = jnp.dot(h, w_ref[:, OFF_Q:OFF_K], preferred_element_type=F32)
        u = gate[:, :SGU_WIDTH]
        v = gate[:, SGU_WIDTH:]
        dv = v - _group_sums(v, ones_bd) * (1.0 / SGU_GROUP_DIM)
        zk = jnp.dot(h, w_ref[:, OFF_K:OFF_V], preferred_element_type=F32)
        var = _group_sums(dv * dv, ones_bd) * (1.0 / SGU_GROUP_DIM)
        zv = jnp.dot(h, w_ref[:, OFF_V:OFF_C], preferred_element_type=F32)
        vn = (dv * lax.rsqrt(var + EPS)).astype(BF16)
        za_ref[rs, :] = jnp.dot(h, w_ref[:, 0:OFF_Q], preferred_element_type=F32)

        for ch in range(SUB_ROWS // SGU_CHUNK):
            sl = slice(ch * SGU_CHUNK, (ch + 1) * SGU_CHUNK)
            mixed = jnp.dot(sguw_ref[...], vn[sl], preferred_element_type=F32)
            vm = mixed[0:SGU_CHUNK]
            for g in range(1, SGU_GROUPS):
                vm = jnp.where(lane_group == g, mixed[g * SGU_CHUNK:(g + 1) * SGU_CHUNK], vm)
            yc_ref[row0 + ch * SGU_CHUNK:row0 + (ch + 1) * SGU_CHUNK, :] = (
                u[sl] * (vm + bias)).astype(BF16)

        for c in range(HEAD_PAIRS):
            slot = (st * HEAD_PAIRS + c) * 3
            cols = slice(c * LANES, (c + 1) * LANES)
            _store_both(_rope_cols(zq[:, cols], cos, sin, low_half) * Q_SCALE, qkv_ref, qkvf_ref,
                        scr_a.at[slot % N_SHUF], scr_b.at[slot % N_SHUF], row0, c * LANES)
            _store_both(_rope_cols(zk[:, cols], cos, sin, low_half), qkv_ref, qkvf_ref,
                        scr_a.at[(slot + 1) % N_SHUF], scr_b.at[(slot + 1) % N_SHUF], row0,
                        ATTN_WIDTH + c * LANES)
            _store_both(zv[:, cols], qkv_ref, qkvf_ref,
                        scr_a.at[(slot + 2) % N_SHUF], scr_b.at[(slot + 2) % N_SHUF], row0,
                        2 * ATTN_WIDTH + c * LANES)


def _inproj(x, mod, g_mix, cos_t, sin_t, w_in, sgu_w, sgu_b):
    B, S, _ = x.shape
    tm = INPROJ_TILE
    assert S % tm == 0 and tm % SUB_ROWS == 0 and SUB_ROWS % (FOLD * 16) == 0
    grid = (B, S // tm)
    tok = lambda width: pl.BlockSpec((None, tm, width), lambda b, i: (b, i, 0))
    qkv_width = 3 * ATTN_WIDTH
    folded = pl.BlockSpec((None, FOLD, tm // FOLD, qkv_width), lambda b, i: (b, 0, i, 0))
    nat_shape = jax.ShapeDtypeStruct((B, S, qkv_width), BF16)
    fold_shape = jax.ShapeDtypeStruct((B, FOLD, S // FOLD, qkv_width), BF16)
    return pl.pallas_call(
        _inproj_kernel,
        grid=grid,
        in_specs=[
            tok(D_MODEL),
            pl.BlockSpec((None, N_MOD, D_MODEL), lambda b, i: (b, 0, 0)),
            _const_spec((1, D_MODEL)),
            pl.BlockSpec((tm, LANES), lambda b, i: (i, 0)),
            pl.BlockSpec((tm, LANES), lambda b, i: (i, 0)),
            _const_spec((D_MODEL, PROJ_WIDTH)),
            _const_spec((SGU_GROUPS * SGU_CHUNK, SGU_CHUNK)),
            _const_spec((SGU_CHUNK, SGU_WIDTH)),
        ],
        out_specs=[tok(POOL_WIDTH), tok(qkv_width), folded, tok(SGU_WIDTH)],
        out_shape=[jax.ShapeDtypeStruct((B, S, POOL_WIDTH), F32),
                   nat_shape, fold_shape,
                   jax.ShapeDtypeStruct((B, S, SGU_WIDTH), BF16)],
        scratch_shapes=[pltpu.VMEM((N_SHUF, SUB_ROWS, LANES), F32)] * 2,
        compiler_params=pltpu.CompilerParams(
            dimension_semantics=("parallel", "parallel"), vmem_limit_bytes=VMEM_LIMIT),
        name="inproj",
    )(x, mod, g_mix, cos_t, sin_t, w_in, sgu_w, sgu_b)


def _attn_kernel(*refs, pieces, first, last, seq_len):
    T = pieces
    n_in = 7 + (0 if first else 2)
    q_ref, kp_ref, kc_ref, kn_ref, vp_ref, vc_ref, vn_ref = refs[:7]
    if not first:
        o_in, lse_in = refs[7:9]
    outs = refs[n_in:]
    if last:
        y_out, s_scr, o_nat, lse_nat, o_mid, lse_mid = outs
    else:
        o_out, lse_out, s_scr = outs
    tl = q_ref.shape[1]
    halo = BAND_HALF // T
    pq = Q_BLOCK // T
    pk = K_BLOCK // T
    tile = pl.program_id(2)

    if last:
        per_res = tl // FOLD
        for c in range(HEAD_PAIRS):
            cols = slice(c * LANES, (c + 1) * LANES)
            for src, mid, dst in ((o_in, o_mid, o_nat), (lse_in, lse_mid, lse_nat)):
                for r1 in range(4):
                    for r2 in range(4):
                        mid[c, r1, pl.ds(r2, per_res, stride=4), :] = (
                            src[4 * r2 + r1, :, cols].astype(F32))
                for r1 in range(4):
                    dst[c, pl.ds(r1, 4 * per_res, stride=4), :] = mid[c, r1]

    row = lax.broadcasted_iota(jnp.int32, (Q_BLOCK, K_BLOCK), 0)
    col = lax.broadcasted_iota(jnp.int32, (Q_BLOCK, K_BLOCK), 1)
    rel = T * (col % pk - halo - row % pq) + (col // pk - row // pq)
    band = jnp.abs(rel) <= BAND_HALF
    lane = lax.broadcasted_iota(jnp.int32, (1, LANES), 1)
    first_head = lane < HEAD_DIM
    zero = jnp.zeros((), BF16)
    ones_block = jnp.ones((K_BLOCK, LANES), BF16)

    def gather(ref, start, size, cols):
        parts = [ref[t, start:start + size, cols] for t in range(T)]
        return parts[0] if T == 1 else jnp.concatenate(parts, axis=0)

    def window(prev_ref, cur_ref, next_ref, la, cols):
        lo, hi = la - halo, la + pq + halo
        parts = []
        for t in range(T):
            if lo < 0:
                parts.append(prev_ref[t, :, cols])
            parts.append(cur_ref[t, max(lo, 0):min(hi, tl), cols])
            if hi > tl:
                parts.append(next_ref[t, :, cols])
        return parts[0] if len(parts) == 1 else jnp.concatenate(parts, axis=0)

    def block_bias(a):
        kidx = tile * tl + a * pq - halo + col % pk
        valid = band & (kidx >= 0) & (kidx < seq_len)
        return jnp.where(valid, 0.0, MASK_VALUE)

    def scores(a, c, bias):
        la = a * pq
        cols = slice(c * LANES, (c + 1) * LANES)
        qp = gather(q_ref, la, pq, cols)
        kp = window(kp_ref, kc_ref, kn_ref, la, cols)
        q2 = jnp.concatenate([jnp.where(first_head, qp, zero),
                              jnp.where(first_head, zero, qp)], axis=0)
        s = lax.dot_general(q2, kp, (((1,), (1,)), ((), ())), preferred_element_type=F32)
        s = s + jnp.concatenate([bias, bias], axis=0)
        s_scr[a % 2, c] = s
        return jnp.max(s, axis=-1, keepdims=True)

    def values(a, c, row_max):
        la = a * pq
        cols = slice(c * LANES, (c + 1) * LANES)
        vp = window(vp_ref, vc_ref, vn_ref, la, cols)
        p = jnp.exp2(s_scr[a % 2, c] - row_max).astype(BF16)
        res = jnp.dot(p, jnp.concatenate([vp, ones_block], axis=1),
                      preferred_element_type=F32)
        pv = jnp.where(first_head, res[:Q_BLOCK, :LANES], res[Q_BLOCK:, :LANES])
        den = jnp.where(first_head, res[:Q_BLOCK, LANES:], res[Q_BLOCK:, LANES:])
        m = jnp.where(first_head, row_max[:Q_BLOCK], row_max[Q_BLOCK:])
        if first:
            o_new = pv / den
            lse_new = m + jnp.log2(den)
        else:
            if last:
                o_prev = o_nat[c, la:la + pq, :]
                lse_prev = lse_nat[c, la:la + pq, :]
            else:
                o_prev = gather(o_in, la, pq, cols).astype(F32)
                lse_prev = gather(lse_in, la, pq, cols)
            top = jnp.maximum(lse_prev, m)
            w_prev = jnp.exp2(lse_prev - top)
            w_cur = jnp.exp2(m - top)
            total = w_prev + w_cur * den
            o_new = (w_prev * o_prev + w_cur * pv) / total
            if not last:
                lse_new = top + jnp.log2(total)
        if last:
            y_out[la:la + pq, cols] = o_new.astype(BF16)
        else:
            for t in range(T):
                o_out[t, la:la + pq, cols] = o_new[t * pq:(t + 1) * pq].astype(BF16)
                lse_out[t, la:la + pq, cols] = lse_new[t * pq:(t + 1) * pq]

    n_blocks = tl // pq
    skew = 1 if first else 0
    row_maxima = {}
    for a in range(n_blocks + skew):
        bias = block_bias(a) if a < n_blocks else None
        for c in range(HEAD_PAIRS):
            if a < n_blocks:
                row_maxima[a, c] = scores(a, c, bias)
            if skew and a >= 1:
                values(a - 1, c, row_maxima.pop((a - 1, c)))
        if not skew:
            for c in range(HEAD_PAIRS):
                values(a, c, row_maxima.pop((a, c)))


def _attn_pass(qkv, state, pieces, first, last):
    B, T, R, L, W3 = qkv.shape
    W = W3 // 3
    assert T == pieces
    tl = min(ATTN_TILE // T, L)
    halo = BAND_HALF // T
    assert L % tl == 0 and tl % (Q_BLOCK // T) == 0
    n_tiles = L // tl
    halo_per_tile = tl // halo
    n_halo = L // halo

    cur = lambda j=0: pl.BlockSpec((None, T, None, tl, W), lambda b, r, i: (b, 0, r, i, j))
    prev = lambda j: pl.BlockSpec(
        (None, T, None, halo, W),
        lambda b, r, i: (b, 0, r, jnp.maximum(i * halo_per_tile - 1, 0), j))
    nxt = lambda j: pl.BlockSpec(
        (None, T, None, halo, W),
        lambda b, r, i: (b, 0, r, jnp.minimum((i + 1) * halo_per_tile, n_halo - 1), j))
    in_specs = [cur(0), prev(1), cur(1), nxt(1), prev(2), cur(2), nxt(2)]
    args = [qkv] * 7
    scratch = [pltpu.VMEM((2, HEAD_PAIRS, 2 * Q_BLOCK, K_BLOCK), F32)]
    if not first:
        if last:
            assert T == 1 and R == 1 and tl == ATTN_TILE
            st = lambda: pl.BlockSpec((None, FOLD, tl // FOLD, W), lambda b, r, i: (b, 0, i, 0))
            scratch += [pltpu.VMEM((HEAD_PAIRS, tl, LANES), F32)] * 2
            scratch += [pltpu.VMEM((HEAD_PAIRS, 4, tl // 4, LANES), F32)] * 2
        else:
            st = cur
        in_specs += [st(), st()]
        args += list(state)
    if last:
        out_specs = pl.BlockSpec((None, tl, W), lambda b, r, i: (b, i, 0))
        out_shape = jax.ShapeDtypeStruct((B, L, W), BF16)
    else:
        out_specs = [cur(), cur()]
        out_shape = [jax.ShapeDtypeStruct((B, T, R, L, W), BF16),
                     jax.ShapeDtypeStruct((B, T, R, L, W), F32)]
    return pl.pallas_call(
        functools.partial(_attn_kernel, pieces=T, first=first, last=last, seq_len=L),
        grid=(B, R, n_tiles),
        in_specs=in_specs,
        out_specs=out_specs,
        out_shape=out_shape,
        scratch_shapes=scratch,
        compiler_params=pltpu.CompilerParams(
            dimension_semantics=("parallel", "parallel", "parallel"),
            vmem_limit_bytes=VMEM_LIMIT),
        name=f"attn_t{T}_{'first' if first else ('last' if last else 'mid')}",
    )(*args)


def _dilated_attention(nat, folded):
    B, S, W3 = nat.shape
    W = W3 // 3
    l16 = S // FOLD
    state = _attn_pass(folded.reshape(B, 1, FOLD, l16, W3), None, pieces=1, first=True, last=False)
    t4 = FOLD // MID_DILATION
    state = [a.reshape(B, t4, MID_DILATION, l16, W) for a in state]
    state = _attn_pass(folded.reshape(B, t4, MID_DILATION, l16, W3), state, pieces=t4,
                       first=False, last=False)
    state = [a.reshape(B, FOLD, l16, W) for a in state]
    return _attn_pass(nat.reshape(B, 1, 1, S, W3), state, pieces=1, first=False, last=True)


def _pool_mixer(za_ref, zprev_ref, znext_ref, tile, n_tiles, seq_len):
    rows = za_ref.shape[0]
    za = za_ref[...]
    before = jnp.where(tile > 0, zprev_ref[...], 0.0)
    after = jnp.where(tile < n_tiles - 1, znext_ref[...], 0.0)
    ext = jnp.concatenate([before, za, after], axis=0)
    n_ext = rows + 2 * POOL_HALO
    back = lambda a, s: pltpu.roll(a, s, axis=0)
    fwd = lambda a, s: pltpu.roll(a, n_ext - s, axis=0)
    sums = [ext + back(ext, 1)]
    for step in (1, 2, 4):
        sums.append(back(sums[-1], step) + fwd(sums[-1], step))
    lane_group = lax.broadcasted_iota(jnp.int32, (1, POOL_WIDTH), 1) // POOL_GROUP_DIM
    win_sum = sums[0][POOL_HALO:POOL_HALO + rows]
    half = jnp.full((1, POOL_WIDTH), POOL_WINDOWS[0] // 2, jnp.int32)
    for g in range(1, len(POOL_WINDOWS)):
        win_sum = jnp.where(lane_group == g, sums[g][POOL_HALO:POOL_HALO + rows], win_sum)
        half = jnp.where(lane_group == g, POOL_WINDOWS[g] // 2, half)

    def inv_count(row0):
        pos = tile * rows + row0 + lax.broadcasted_iota(jnp.int32, (POOL_HALO, 1), 0)
        cnt = jnp.minimum(pos + half, seq_len) - jnp.maximum(pos - half, 0)
        return 1.0 / cnt.astype(F32)

    edge = POOL_HALO
    mean = jnp.concatenate([
        win_sum[:edge] * inv_count(0),
        win_sum[edge:rows - edge] * (0.5 / half.astype(F32)),
        win_sum[rows - edge:] * inv_count(rows - edge)], axis=0)
    return mean - za


def _outmlp_kernel(x_ref, mod_ref, za_ref, zprev_ref, znext_ref, yb_ref, yc_ref,
                   poolw_ref, pools_ref, wout_ref, gmlp_ref, wup_ref, wdown_ref, gfin_ref,
                   o_ref, *, seq_len, final):
    tile = pl.program_id(1)
    n_tiles = pl.num_programs(1)
    o1 = POOL_WIDTH
    o2 = o1 + ATTN_WIDTH
    gain = gmlp_ref[...] * (1.0 + mod_ref[4:5, :])
    p = _pool_mixer(za_ref, zprev_ref, znext_ref, tile, n_tiles, seq_len).astype(BF16)
    y = jnp.dot(yb_ref[...], wout_ref[o1:o2, :], preferred_element_type=F32)
    y = y + jnp.dot(yc_ref[...], wout_ref[o2:, :], preferred_element_type=F32)
    ya = jnp.dot(p, poolw_ref[...], preferred_element_type=F32) * pools_ref[...]
    y = y + jnp.dot(ya.astype(BF16), wout_ref[0:o1, :], preferred_element_type=F32)
    o_ref[...] = x_ref[...] + mod_ref[2:3, :] * y

    for st in range(x_ref.shape[0] // SUB_ROWS):
        rs = slice(st * SUB_ROWS, (st + 1) * SUB_ROWS)
        x1 = o_ref[rs, :]
        xn = x1 * lax.rsqrt(jnp.mean(x1 * x1, axis=-1, keepdims=True) + EPS)
        h = (xn * gain + mod_ref[3:4, :]).astype(BF16)
        acc = None
        for c in range(D_FF // FF_CHUNK):
            cs = slice(c * FF_CHUNK, (c + 1) * FF_CHUNK)
            up = jnp.dot(h, wup_ref[:, cs], preferred_element_type=F32)
            act = jnp.square(jnp.maximum(up, 0.0)).astype(BF16)
            part = jnp.dot(act, wdown_ref[cs, :], preferred_element_type=F32)
            acc = part if acc is None else acc + part
        x2 = x1 + mod_ref[5:6, :] * acc
        if final:
            x2 = x2 * lax.rsqrt(jnp.mean(x2 * x2, axis=-1, keepdims=True) + EPS) * gfin_ref[...]
        o_ref[rs, :] = x2


def _outmlp(x, mod, za, yb, yc, pool_w, pool_scale, w_out, g_mlp, w_up, w_down, g_final, final):
    B, S, _ = x.shape
    tm = ROW_TILE
    assert S % tm == 0
    n_tiles = S // tm
    halo_per_tile = tm // POOL_HALO
    n_halo = S // POOL_HALO
    tok = lambda width: pl.BlockSpec((None, tm, width), lambda b, i: (b, i, 0))
    return pl.pallas_call(
        functools.partial(_outmlp_kernel, seq_len=S, final=final),
        grid=(B, n_tiles),
        in_specs=[
            tok(D_MODEL),
            pl.BlockSpec((None, N_MOD, D_MODEL), lambda b, i: (b, 0, 0)),
            tok(POOL_WIDTH),
            pl.BlockSpec((None, POOL_HALO, POOL_WIDTH),
                         lambda b, i: (b, jnp.maximum(i * halo_per_tile - 1, 0), 0)),
            pl.BlockSpec((None, POOL_HALO, POOL_WIDTH),
                         lambda b, i: (b, jnp.minimum((i + 1) * halo_per_tile, n_halo - 1), 0)),
            tok(ATTN_WIDTH),
            tok(SGU_WIDTH),
            _const_spec((POOL_WIDTH, POOL_WIDTH)),
            _const_spec((1, POOL_WIDTH)),
            _const_spec((D_MODEL, D_MODEL)),
            _const_spec((1, D_MODEL)),
            _const_spec((D_MODEL, D_FF)),
            _const_spec((D_FF, D_MODEL)),
            _const_spec((1, D_MODEL)),
        ],
        out_specs=tok(D_MODEL),
        out_shape=jax.ShapeDtypeStruct((B, S, D_MODEL), F32),
        compiler_params=pltpu.CompilerParams(
            dimension_semantics=("parallel", "parallel"), vmem_limit_bytes=VMEM_LIMIT),
        name="outmlp",
    )(x, mod, za, za, za, yb, yc, pool_w, pool_scale, w_out, g_mlp, w_up, w_down, g_final)


def _rope_tables(seq_len):
    inv_freq = ROPE_THETA ** (-jnp.arange(0, ROT_DIM, 2, dtype=F32) / ROT_DIM)
    ang = jnp.arange(seq_len).astype(F32)[:, None] * inv_freq[None, :]
    cos, sin = jnp.cos(ang), jnp.sin(ang)
    pad = HEAD_DIM - ROT_DIM
    cos_head = jnp.concatenate([cos, cos, jnp.ones((seq_len, pad), F32)], axis=1)
    sin_head = jnp.concatenate([-sin, sin, jnp.zeros((seq_len, pad), F32)], axis=1)
    reps = LANES // HEAD_DIM
    return jnp.tile(cos_head, (1, reps)), jnp.tile(sin_head, (1, reps))


def _block_diag(blocks):
    g, n, m = blocks.shape
    eye = jnp.eye(g, dtype=blocks.dtype)
    return (eye[:, None, :, None] * blocks[:, :, None, :]).reshape(g * n, g * m)


def _prepare_layer(l, g_mix, g_mlp, w_in, pool_w, pool_scale, sgu_w, sgu_b, w_out, w_up, w_down):
    return dict(
        g_mix=g_mix[l].reshape(1, D_MODEL),
        g_mlp=g_mlp[l].reshape(1, D_MODEL),
        w_in=w_in[l].astype(BF16),
        pool_w=_block_diag(pool_w[l]).astype(BF16),
        pool_scale=pool_scale[l].reshape(1, POOL_WIDTH),
        sgu_w=sgu_w[l].reshape(SGU_GROUPS * SGU_CHUNK, SGU_CHUNK).astype(BF16),
        sgu_b=jnp.repeat(jnp.transpose(sgu_b[l]), SGU_GROUP_DIM, axis=1),
        w_out=w_out[l].astype(BF16),
        w_up=w_up[l].astype(BF16),
        w_down=w_down[l].astype(BF16),
    )


def _trunk(x, mods, layers, g_final):
    B, S, _ = x.shape
    cos_t, sin_t = _rope_tables(S)
    for l, lw in enumerate(layers):
        mod = mods[l].reshape(B, N_MOD, D_MODEL)
        za, qkv, qkv_folded, yc = _inproj(x, mod, lw["g_mix"], cos_t, sin_t, lw["w_in"],
                                          lw["sgu_w"], lw["sgu_b"])
        yb = _dilated_attention(qkv, qkv_folded)
        x = _outmlp(x, mod, za, yb, yc, lw["pool_w"], lw["pool_scale"], lw["w_out"],
                    lw["g_mlp"], lw["w_up"], lw["w_down"], g_final, final=(l == DEPTH - 1))
    return x


def kernel(x_prompt, x_sample, c_prompt, c_sample, w_ada, b_ada, g_mix, g_mlp, w_in, pool_w,
           pool_scale, sgu_w, sgu_b, w_out, w_up, w_down, g_final):
    nb_p = c_prompt.shape[0]
    nb_s = c_sample.shape[0]
    rows = -(-(nb_p + nb_s) // 8) * 8
    c_all = jnp.concatenate(
        [c_prompt, c_sample, jnp.zeros((rows - nb_p - nb_s, D_MODEL), F32)], axis=0)
    mods = _ada_mod(c_all, w_ada, b_ada)
    layers = [_prepare_layer(l, g_mix, g_mlp, w_in, pool_w, pool_scale, sgu_w, sgu_b,
                             w_out, w_up, w_down) for l in range(DEPTH)]
    gf = g_final.reshape(1, D_MODEL)
    y_prompt = _trunk(x_prompt, mods[:, :nb_p], layers, gf)
    y_sample = _trunk(x_sample, mods[:, nb_p:nb_p + nb_s], layers, gf)
    return (y_prompt, y_sample)
```

```python
import functools

import jax
import jax.numpy as jnp
from jax import lax
from jax.experimental import pallas as pl
from jax.experimental.pallas import tpu as pltpu

F32 = jnp.float32
BF16 = jnp.bfloat16

D_MODEL = 1024
DEPTH = 2
HEAD_DIM = 64
POOL_WINDOWS = (2, 4, 8, 16)
POOL_WIDTH = D_MODEL // 4
POOL_GROUP_DIM = POOL_WIDTH // len(POOL_WINDOWS)
ATTN_WIDTH = D_MODEL // 2
BAND_HALF = 64
FOLD = 16
MID_DILATION = 4
ROT_DIM = HEAD_DIM // 4
ROPE_THETA = 500000.0
SGU_WIDTH = D_MODEL // 4
SGU_GROUPS = 4
SGU_GROUP_DIM = SGU_WIDTH // SGU_GROUPS
SGU_CHUNK = 128
D_FF = 4 * D_MODEL
N_MOD = 6
EPS = 1e-6
MASK_VALUE = -1e30
Q_SCALE = HEAD_DIM ** -0.5 * 1.4426950408889634

OFF_Q = POOL_WIDTH
OFF_K = OFF_Q + ATTN_WIDTH
OFF_V = OFF_K + ATTN_WIDTH
OFF_C = OFF_V + ATTN_WIDTH
PROJ_WIDTH = OFF_C + 2 * SGU_WIDTH

LANES = 128
HEAD_PAIRS = ATTN_WIDTH // LANES
POOL_HALO = 8
ROW_TILE = 512
INPROJ_TILE = 1024
ATTN_TILE = 1024
Q_BLOCK = 128
K_BLOCK = Q_BLOCK + 2 * BAND_HALF
FF_CHUNK = 1024
SUB_ROWS = 256
N_SHUF = 6
VMEM_LIMIT = 56 * 1024 * 1024


def _const_spec(shape):
    nd = len(shape)
    return pl.BlockSpec(shape, lambda *_: (0,) * nd, pipeline_mode=pl.Buffered(1))


def _ada_kernel(c_ref, w_ref, b_ref, o_ref):
    c = c_ref[...]
    act = c * (1.0 / (1.0 + jnp.exp(-c)))
    o_ref[...] = jnp.dot(act.astype(BF16), w_ref[...].astype(BF16),
                         preferred_element_type=F32) + b_ref[...]


def _ada_mod(c_all, w_ada, b_ada):
    rows = c_all.shape[0]
    return pl.pallas_call(
        _ada_kernel,
        grid=(DEPTH, N_MOD),
        in_specs=[
            pl.BlockSpec((rows, D_MODEL), lambda l, j: (0, 0)),
            pl.BlockSpec((None, D_MODEL, D_MODEL), lambda l, j: (l, 0, j)),
            pl.BlockSpec((None, 1, D_MODEL), lambda l, j: (l, 0, j)),
        ],
        out_specs=pl.BlockSpec((None, rows, D_MODEL), lambda l, j: (l, 0, j)),
        out_shape=jax.ShapeDtypeStruct((DEPTH, rows, N_MOD * D_MODEL), F32),
        compiler_params=pltpu.CompilerParams(vmem_limit_bytes=VMEM_LIMIT),
        name="ada_mod",
    )(c_all, w_ada, b_ada.reshape(DEPTH, 1, N_MOD * D_MODEL))


def _rope_cols(z, cos, sin, low_half):
    up = pltpu.roll(z, LANES - ROT_DIM // 2, axis=1)
    down = pltpu.roll(z, ROT_DIM // 2, axis=1)
    return z * cos + jnp.where(low_half, up, down) * sin


def _store_both(val, nat_ref, fold_ref, scr_a, scr_b, row0, col0):
    sub = val.shape[0]
    cols = slice(col0, col0 + LANES)
    nat_ref[row0:row0 + sub, cols] = val.astype(BF16)
    scr_a[...] = val
    quarter = sub // 4
    for r1 in range(4):
        scr_b[r1 * quarter:(r1 + 1) * quarter, :] = scr_a[pl.ds(r1, quarter, stride=4), :]
    per_res = sub // FOLD
    f0 = row0 // FOLD
    for r1 in range(4):
        for r2 in range(4):
            piece = scr_b[pl.ds(r1 * quarter + r2, per_res, stride=4), :]
            fold_ref[4 * r2 + r1, f0:f0 + per_res, cols] = piece.astype(BF16)


def _group_sums(v, ones_bd):
    hi = v.astype(BF16)
    lo = (v - hi.astype(F32)).astype(BF16)
    return (jnp.dot(hi, ones_bd, preferred_element_type=F32)
            + jnp.dot(lo, ones_bd, preferred_element_type=F32))


def _inproj_kernel(x_ref, mod_ref, g_ref, cos_ref, sin_ref, w_ref, sguw_ref, sgub_ref,
                   za_ref, qkv_ref, qkvf_ref, yc_ref, scr_a, scr_b):
    rows = x_ref.shape[0]
    lane = lax.broadcasted_iota(jnp.int32, (1, LANES), 1)
    low_half = (lane % HEAD_DIM) < (ROT_DIM // 2)
    gi = lax.broadcasted_iota(jnp.int32, (SGU_WIDTH, SGU_WIDTH), 0) // SGU_GROUP_DIM
    gj = lax.broadcasted_iota(jnp.int32, (SGU_WIDTH, SGU_WIDTH), 1) // SGU_GROUP_DIM
    ones_bd = jnp.where(gi == gj, 1.0, 0.0).astype(BF16)
    lane_group = lax.broadcasted_iota(jnp.int32, (1, SGU_WIDTH), 1) // SGU_GROUP_DIM
    bias = sgub_ref[...]
    gain = g_ref[...] * (1.0 + mod_ref[1:2, :])
    shift = mod_ref[0:1, :]

    for st in range(rows // SUB_ROWS):
        row0 = st * SUB_ROWS
        rs = slice(row0, row0 + SUB_ROWS)
        x = x_ref[rs, :]
        xn = x * lax.rsqrt(jnp.mean(x * x, axis=-1, keepdims=True) + EPS)
        h = (xn * gain + shift).astype(BF16)

        cos = cos_ref[rs, :]
        sin = sin_ref[rs, :]
        gate = jax.nn.gelu(jnp.dot(h, w_ref[:, OFF_C:PROJ_WIDTH], preferred_element_type=F32))
        zq = jnp.dot(h, w_ref[:, OFF_Q:OFF_K], preferred_element_type=F32)
        u = gate[:, :SGU_WIDTH]
        v = gate[:, SGU_WIDTH:]
        dv = v - _group_sums(v, ones_bd) * (1.0 / SGU_GROUP_DIM)
        zk = jnp.dot(h, w_ref[:, OFF_K:OFF_V], preferred_element_type=F32)
        var = _group_sums(dv * dv, ones_bd) * (1.0 / SGU_GROUP_DIM)
        zv = jnp.dot(h, w_ref[:, OFF_V:OFF_C], preferred_element_type=F32)
        vn = (dv * lax.rsqrt(var + EPS)).astype(BF16)
        za_ref[rs, :] = jnp.dot(h, w_ref[:, 0:OFF_Q], preferred_element_type=F32)

        for ch in range(SUB_ROWS // SGU_CHUNK):
            sl = slice(ch * SGU_CHUNK, (ch + 1) * SGU_CHUNK)
            mixed = jnp.dot(sguw_ref[...], vn[sl], preferred_element_type=F32)
            vm = mixed[0:SGU_CHUNK]
            for g in range(1, SGU_GROUPS):
                vm = jnp.where(lane_group == g, mixed[g * SGU_CHUNK:(g + 1) * SGU_CHUNK], vm)
            yc_ref[row0 + ch * SGU_CHUNK:row0 + (ch + 1) * SGU_CHUNK, :] = (
                u[sl] * (vm + bias)).astype(BF16)

        for c in range(HEAD_PAIRS):
            slot = (st * HEAD_PAIRS + c) * 3
            cols = slice(c * LANES, (c + 1) * LANES)
            _store_both(_rope_cols(zq[:, cols], cos, sin, low_half) * Q_SCALE, qkv_ref, qkvf_ref,
                        scr_a.at[slot % N_SHUF], scr_b.at[slot % N_SHUF], row0, c * LANES)
            _store_both(_rope_cols(zk[:, cols], cos, sin, low_half), qkv_ref, qkvf_ref,
                        scr_a.at[(slot + 1) % N_SHUF], scr_b.at[(slot + 1) % N_SHUF], row0,
                        ATTN_WIDTH + c * LANES)
            _store_both(zv[:, cols], qkv_ref, qkvf_ref,
                        scr_a.at[(slot + 2) % N_SHUF], scr_b.at[(slot + 2) % N_SHUF], row0,
                        2 * ATTN_WIDTH + c * LANES)


def _inproj(x, mod, g_mix, cos_t, sin_t, w_in, sgu_w, sgu_b):
    B, S, _ = x.shape
    tm = INPROJ_TILE
    assert S % tm == 0 and tm % SUB_ROWS == 0 and SUB_ROWS % (FOLD * 16) == 0
    grid = (B, S // tm)
    tok = lambda width: pl.BlockSpec((None, tm, width), lambda b, i: (b, i, 0))
    qkv_width = 3 * ATTN_WIDTH
    folded = pl.BlockSpec((None, FOLD, tm // FOLD, qkv_width), lambda b, i: (b, 0, i, 0))
    nat_shape = jax.ShapeDtypeStruct((B, S, qkv_width), BF16)
    fold_shape = jax.ShapeDtypeStruct((B, FOLD, S // FOLD, qkv_width), BF16)
    return pl.pallas_call(
        _inproj_kernel,
        grid=grid,
        in_specs=[
            tok(D_MODEL),
            pl.BlockSpec((None, N_MOD, D_MODEL), lambda b, i: (b, 0, 0)),
            _const_spec((1, D_MODEL)),
            pl.BlockSpec((tm, LANES), lambda b, i: (i, 0)),
            pl.BlockSpec((tm, LANES), lambda b, i: (i, 0)),
            _const_spec((D_MODEL, PROJ_WIDTH)),
            _const_spec((SGU_GROUPS * SGU_CHUNK, SGU_CHUNK)),
            _const_spec((SGU_CHUNK, SGU_WIDTH)),
        ],
        out_specs=[tok(POOL_WIDTH), tok(qkv_width), folded, tok(SGU_WIDTH)],
        out_shape=[jax.ShapeDtypeStruct((B, S, POOL_WIDTH), F32),
                   nat_shape, fold_shape,
                   jax.ShapeDtypeStruct((B, S, SGU_WIDTH), BF16)],
        scratch_shapes=[pltpu.VMEM((N_SHUF, SUB_ROWS, LANES), F32)] * 2,
        compiler_params=pltpu.CompilerParams(
            dimension_semantics=("parallel", "parallel"), vmem_limit_bytes=VMEM_LIMIT),
        name="inproj",
    )(x, mod, g_mix, cos_t, sin_t, w_in, sgu_w, sgu_b)


def _attention_tile(q_ref, k_refs, v_refs, state_in, outs, s_scr, order_scr, *,
                    pieces, first, last, seq_len, tile):
    T = pieces
    kp_ref, kc_ref, kn_ref = k_refs
    vp_ref, vc_ref, vn_ref = v_refs
    if not first:
        o_in, lse_in = state_in
    if last:
        (y_out,) = outs
        o_nat, lse_nat, o_mid, lse_mid = order_scr
    else:
        o_out, lse_out = outs
    tl = q_ref.shape[1]
    halo = BAND_HALF // T
    pq = Q_BLOCK // T
    pk = K_BLOCK // T

    def prepare():
        if not last:
            return
        per_res = tl // FOLD
        for c in range(HEAD_PAIRS):
            cols = slice(c * LANES, (c + 1) * LANES)
            for src, mid, dst in ((o_in, o_mid, o_nat), (lse_in, lse_mid, lse_nat)):
                for r1 in range(4):
                    for r2 in range(4):
                        mid[c, r1, pl.ds(r2, per_res, stride=4), :] = (
                            src[4 * r2 + r1, :, cols].astype(F32))
                for r1 in range(4):
                    dst[c, pl.ds(r1, 4 * per_res, stride=4), :] = mid[c, r1]

    row = lax.broadcasted_iota(jnp.int32, (Q_BLOCK, K_BLOCK), 0)
    col = lax.broadcasted_iota(jnp.int32, (Q_BLOCK, K_BLOCK), 1)
    rel = T * (col % pk - halo - row % pq) + (col // pk - row // pq)
    band = jnp.abs(rel) <= BAND_HALF
    lane = lax.broadcasted_iota(jnp.int32, (1, LANES), 1)
    first_head = lane < HEAD_DIM
    zero = jnp.zeros((), BF16)
    ones_block = jnp.ones((K_BLOCK, LANES), BF16)

    def gather(ref, start, size, cols):
        parts = [ref[t, start:start + size, cols] for t in range(T)]
        return parts[0] if T == 1 else jnp.concatenate(parts, axis=0)

    def window(prev_ref, cur_ref, next_ref, la, cols):
        lo, hi = la - halo, la + pq + halo
        parts = []
        for t in range(T):
            if lo < 0:
                parts.append(prev_ref[t, :, cols])
            parts.append(cur_ref[t, max(lo, 0):min(hi, tl), cols])
            if hi > tl:
                parts.append(next_ref[t, :, cols])
        return parts[0] if len(parts) == 1 else jnp.concatenate(parts, axis=0)

    def block_bias(a):
        kidx = tile * tl + a * pq - halo + col % pk
        valid = band & (kidx >= 0) & (kidx < seq_len)
        return jnp.where(valid, 0.0, MASK_VALUE)

    def scores(a, c, bias):
        la = a * pq
        cols = slice(c * LANES, (c + 1) * LANES)
        qp = gather(q_ref, la, pq, cols)
        kp = window(kp_ref, kc_ref, kn_ref, la, cols)
        q2 = jnp.concatenate([jnp.where(first_head, qp, zero),
                              jnp.where(first_head, zero, qp)], axis=0)
        s = lax.dot_general(q2, kp, (((1,), (1,)), ((), ())), preferred_element_type=F32)
        s = s + jnp.concatenate([bias, bias], axis=0)
        s_scr[a % 2, c] = s
        return jnp.max(s, axis=-1, keepdims=True)

    def values(a, c, row_max):
        la = a * pq
        cols = slice(c * LANES, (c + 1) * LANES)
        vp = window(vp_ref, vc_ref, vn_ref, la, cols)
        p = jnp.exp2(s_scr[a % 2, c] - row_max).astype(BF16)
        res = jnp.dot(p, jnp.concatenate([vp, ones_block], axis=1),
                      preferred_element_type=F32)
        pv = jnp.where(first_head, res[:Q_BLOCK, :LANES], res[Q_BLOCK:, :LANES])
        den = jnp.where(first_head, res[:Q_BLOCK, LANES:], res[Q_BLOCK:, LANES:])
        m = jnp.where(first_head, row_max[:Q_BLOCK], row_max[Q_BLOCK:])
        if first:
            o_new = pv / den
            lse_new = m + jnp.log2(den)
        else:
            if last:
                o_prev = o_nat[c, la:la + pq, :]
                lse_prev = lse_nat[c, la:la + pq, :]
            else:
                o_prev = gather(o_in, la, pq, cols).astype(F32)
                lse_prev = gather(lse_in, la, pq, cols)
            top = jnp.maximum(lse_prev, m)
            w_prev = jnp.exp2(lse_prev - top)
            w_cur = jnp.exp2(m - top)
            total = w_prev + w_cur * den
            o_new = (w_prev * o_prev + w_cur * pv) / total
            if not last:
                lse_new = top + jnp.log2(total)
        if last:
            y_out[la:la + pq, cols] = o_new.astype(BF16)
        else:
            for t in range(T):
                o_out[t, la:la + pq, cols] = o_new[t * pq:(t + 1) * pq].astype(BF16)
                lse_out[t, la:la + pq, cols] = lse_new[t * pq:(t + 1) * pq]

    return prepare, block_bias, scores, values, tl // pq


def _attn_kernel(*refs, pieces, first, seq_len):
    n_in = 7 + (0 if first else 2)
    prepare, block_bias, scores, values, n_blocks = _attention_tile(
        refs[0], refs[1:4], refs[4:7], refs[7:n_in], refs[n_in:n_in + 2], refs[n_in + 2], (),
        pieces=pieces, first=first, last=False, seq_len=seq_len, tile=pl.program_id(2))
    skew = 1 if first else 0
    row_maxima = {}
    for a in range(n_blocks + skew):
        bias = block_bias(a) if a < n_blocks else None
        for c in range(HEAD_PAIRS):
            if a < n_blocks:
                row_maxima[a, c] = scores(a, c, bias)
            if skew and a >= 1:
                values(a - 1, c, row_maxima.pop((a - 1, c)))
        if not skew:
            for c in range(HEAD_PAIRS):
                values(a, c, row_maxima.pop((a, c)))


def _attn_pass(qkv, state, pieces, first):
    B, T, R, L, W3 = qkv.shape
    W = W3 // 3
    assert T == pieces
    tl = min(ATTN_TILE // T, L)
    halo = BAND_HALF // T
    assert L % tl == 0 and tl % (Q_BLOCK // T) == 0
    n_tiles = L // tl
    halo_per_tile = tl // halo
    n_halo = L // halo

    cur = lambda j=0: pl.BlockSpec((None, T, None, tl, W), lambda b, r, i: (b, 0, r, i, j))
    prev = lambda j: pl.BlockSpec(
        (None, T, None, halo, W),
        lambda b, r, i: (b, 0, r, jnp.maximum(i * halo_per_tile - 1, 0), j))
    nxt = lambda j: pl.BlockSpec(
        (None, T, None, halo, W),
        lambda b, r, i: (b, 0, r, jnp.minimum((i + 1) * halo_per_tile, n_halo - 1), j))
    in_specs = [cur(0), prev(1), cur(1), nxt(1), prev(2), cur(2), nxt(2)]
    args = [qkv] * 7
    scratch = [pltpu.VMEM((2, HEAD_PAIRS, 2 * Q_BLOCK, K_BLOCK), F32)]
    if not first:
        in_specs += [cur(), cur()]
        args += list(state)
    return pl.pallas_call(
        functools.partial(_attn_kernel, pieces=T, first=first, seq_len=L),
        grid=(B, R, n_tiles),
        in_specs=in_specs,
        out_specs=[cur(), cur()],
        out_shape=[jax.ShapeDtypeStruct((B, T, R, L, W), BF16),
                   jax.ShapeDtypeStruct((B, T, R, L, W), F32)],
        scratch_shapes=scratch,
        compiler_params=pltpu.CompilerParams(
            dimension_semantics=("parallel", "parallel", "parallel"),
            vmem_limit_bytes=VMEM_LIMIT),
        name=f"attn_t{T}_{'first' if first else 'mid'}",
    )(*args)


def _folded_attention(folded):
    B, _, l16, W3 = folded.shape
    W = W3 // 3
    state = _attn_pass(folded.reshape(B, 1, FOLD, l16, W3), None, pieces=1, first=True)
    t4 = FOLD // MID_DILATION
    state = [a.reshape(B, t4, MID_DILATION, l16, W) for a in state]
    state = _attn_pass(folded.reshape(B, t4, MID_DILATION, l16, W3), state, pieces=t4, first=False)
    return [a.reshape(B, FOLD, l16, W) for a in state]


def _pool_mixer(za_ref, zprev_ref, znext_ref, tile, n_tiles, seq_len):
    rows = za_ref.shape[0]
    za = za_ref[...]
    before = jnp.where(tile > 0, zprev_ref[...], 0.0)
    after = jnp.where(tile < n_tiles - 1, znext_ref[...], 0.0)
    ext = jnp.concatenate([before, za, after], axis=0)
    n_ext = rows + 2 * POOL_HALO
    back = lambda a, s: pltpu.roll(a, s, axis=0)
    fwd = lambda a, s: pltpu.roll(a, n_ext - s, axis=0)
    sums = [ext + back(ext, 1)]
    for step in (1, 2, 4):
        sums.append(back(sums[-1], step) + fwd(sums[-1], step))
    lane_group = lax.broadcasted_iota(jnp.int32, (1, POOL_WIDTH), 1) // POOL_GROUP_DIM
    win_sum = sums[0][POOL_HALO:POOL_HALO + rows]
    half = jnp.full((1, POOL_WIDTH), POOL_WINDOWS[0] // 2, jnp.int32)
    for g in range(1, len(POOL_WINDOWS)):
        win_sum = jnp.where(lane_group == g, sums[g][POOL_HALO:POOL_HALO + rows], win_sum)
        half = jnp.where(lane_group == g, POOL_WINDOWS[g] // 2, half)

    def inv_count(row0):
        pos = tile * rows + row0 + lax.broadcasted_iota(jnp.int32, (POOL_HALO, 1), 0)
        cnt = jnp.minimum(pos + half, seq_len) - jnp.maximum(pos - half, 0)
        return 1.0 / cnt.astype(F32)

    edge = POOL_HALO
    mean = jnp.concatenate([
        win_sum[:edge] * inv_count(0),
        win_sum[edge:rows - edge] * (0.5 / half.astype(F32)),
        win_sum[rows - edge:] * inv_count(rows - edge)], axis=0)
    return mean - za


def _tail_kernel(q_ref, kp_ref, kc_ref, kn_ref, vp_ref, vc_ref, vn_ref, o_in, lse_in,
                 x_ref, mod_ref, za_ref, zprev_ref, znext_ref, yc_ref,
                 poolw_ref, pools_ref, wout_ref, gmlp_ref, wup_ref, wdown_ref, gfin_ref,
                 o_ref, yb_scr, s_scr, o_nat, lse_nat, o_mid, lse_mid, *, seq_len, final):
    tile = pl.program_id(1)
    n_tiles = pl.num_programs(1)
    prepare, block_bias, scores, values, n_blocks = _attention_tile(
        q_ref, (kp_ref, kc_ref, kn_ref), (vp_ref, vc_ref, vn_ref), (o_in, lse_in), (yb_scr,),
        s_scr, (o_nat, lse_nat, o_mid, lse_mid),
        pieces=1, first=False, last=True, seq_len=seq_len, tile=tile)
    n_sub = x_ref.shape[0] // SUB_ROWS
    blocks_per_sub = n_blocks // n_sub

    def attention_steps(st):
        steps = []
        for a in range(st * blocks_per_sub, (st + 1) * blocks_per_sub):
            maxima = {}

            def score_step(a=a, maxima=maxima):
                bias = block_bias(a)
                for c in range(HEAD_PAIRS):
                    maxima[c] = scores(a, c, bias)

            def value_step(a=a, maxima=maxima):
                for c in range(HEAD_PAIRS):
                    values(a, c, maxima[c])

            steps += [score_step, value_step]
        return steps

    o1 = POOL_WIDTH
    o2 = o1 + ATTN_WIDTH
    gain = gmlp_ref[...] * (1.0 + mod_ref[4:5, :])
    n_chunks = D_FF // FF_CHUNK
    chunk = lambda c: slice(c * FF_CHUNK, (c + 1) * FF_CHUNK)

    prepare()
    for step in attention_steps(0):
        step()
    p = _pool_mixer(za_ref, zprev_ref, znext_ref, tile, n_tiles, seq_len).astype(BF16)

    for st in range(n_sub):
        rs = slice(st * SUB_ROWS, (st + 1) * SUB_ROWS)
        pending = iter(attention_steps(st + 1) if st + 1 < n_sub else [])
        run_next = lambda: next(pending, lambda: None)()

        y = jnp.dot(yb_scr[rs, :], wout_ref[o1:o2, :], preferred_element_type=F32)
        y = y + jnp.dot(yc_ref[rs, :], wout_ref[o2:, :], preferred_element_type=F32)
        ya = jnp.dot(p[rs], poolw_ref[...], preferred_element_type=F32) * pools_ref[...]
        y = y + jnp.dot(ya.astype(BF16), wout_ref[0:o1, :], preferred_element_type=F32)
        x1 = x_ref[rs, :] + mod_ref[2:3, :] * y
        run_next()

        xn = x1 * lax.rsqrt(jnp.mean(x1 * x1, axis=-1, keepdims=True) + EPS)
        h = (xn * gain + mod_ref[3:4, :]).astype(BF16)
        acc = None
        for c in range(n_chunks):
            up = jnp.dot(h, wup_ref[:, chunk(c)], preferred_element_type=F32)
            run_next()
            act = jnp.square(jnp.maximum(up, 0.0)).astype(BF16)
            part = jnp.dot(act, wdown_ref[chunk(c), :], preferred_element_type=F32)
            acc = part if acc is None else acc + part
        for step in pending:
            step()
        x2 = x1 + mod_ref[5:6, :] * acc
        if final:
            x2 = x2 * lax.rsqrt(jnp.mean(x2 * x2, axis=-1, keepdims=True) + EPS) * gfin_ref[...]
        o_ref[rs, :] = x2


def _tail(qkv, state, x, mod, za, yc, pool_w, pool_scale, w_out, g_mlp, w_up, w_down, g_final,
          final):
    B, S, _ = x.shape
    W = ATTN_WIDTH
    tm = ROW_TILE
    assert S % tm == 0 and tm % SUB_ROWS == 0 and SUB_ROWS % Q_BLOCK == 0
    n_tiles = S // tm
    pool_per_tile = tm // POOL_HALO
    n_pool = S // POOL_HALO
    band_per_tile = tm // BAND_HALF
    n_band = S // BAND_HALF
    tok = lambda width: pl.BlockSpec((None, tm, width), lambda b, i: (b, i, 0))
    cur = lambda j: pl.BlockSpec((None, 1, tm, W), lambda b, i: (b, 0, i, j))
    prev = lambda j: pl.BlockSpec(
        (None, 1, BAND_HALF, W), lambda b, i: (b, 0, jnp.maximum(i * band_per_tile - 1, 0), j))
    nxt = lambda j: pl.BlockSpec(
        (None, 1, BAND_HALF, W),
        lambda b, i: (b, 0, jnp.minimum((i + 1) * band_per_tile, n_band - 1), j))
    st = lambda: pl.BlockSpec((None, FOLD, tm // FOLD, W), lambda b, i: (b, 0, i, 0))
    qkv4 = qkv.reshape(B, 1, S, 3 * W)
    return pl.pallas_call(
        functools.partial(_tail_kernel, seq_len=S, final=final),
        grid=(B, n_tiles),
        in_specs=[
            cur(0), prev(1), cur(1), nxt(1), prev(2), cur(2), nxt(2), st(), st(),
            tok(D_MODEL),
            pl.BlockSpec((None, N_MOD, D_MODEL), lambda b, i: (b, 0, 0)),
            tok(POOL_WIDTH),
            pl.BlockSpec((None, POOL_HALO, POOL_WIDTH),
                         lambda b, i: (b, jnp.maximum(i * pool_per_tile - 1, 0), 0)),
            pl.BlockSpec((None, POOL_HALO, POOL_WIDTH),
                         lambda b, i: (b, jnp.minimum((i + 1) * pool_per_tile, n_pool - 1), 0)),
            tok(SGU_WIDTH),
            _const_spec((POOL_WIDTH, POOL_WIDTH)),
            _const_spec((1, POOL_WIDTH)),
            _const_spec((D_MODEL, D_MODEL)),
            _const_spec((1, D_MODEL)),
            _const_spec((D_MODEL, D_FF)),
            _const_spec((D_FF, D_MODEL)),
            _const_spec((1, D_MODEL)),
        ],
        out_specs=tok(D_MODEL),
        out_shape=jax.ShapeDtypeStruct((B, S, D_MODEL), F32),
        scratch_shapes=[
            pltpu.VMEM((tm, W), BF16),
            pltpu.VMEM((2, HEAD_PAIRS, 2 * Q_BLOCK, K_BLOCK), F32),
            pltpu.VMEM((HEAD_PAIRS, tm, LANES), F32), pltpu.VMEM((HEAD_PAIRS, tm, LANES), F32),
            pltpu.VMEM((HEAD_PAIRS, 4, tm // 4, LANES), F32),
            pltpu.VMEM((HEAD_PAIRS, 4, tm // 4, LANES), F32)],
        compiler_params=pltpu.CompilerParams(
            dimension_semantics=("parallel", "parallel"), vmem_limit_bytes=VMEM_LIMIT),
        name="tail",
    )(qkv4, qkv4, qkv4, qkv4, qkv4, qkv4, qkv4, state[0], state[1],
      x, mod, za, za, za, yc, pool_w, pool_scale, w_out, g_mlp, w_up, w_down, g_final)


def _rope_tables(seq_len):
    inv_freq = ROPE_THETA ** (-jnp.arange(0, ROT_DIM, 2, dtype=F32) / ROT_DIM)
    ang = jnp.arange(seq_len).astype(F32)[:, None] * inv_freq[None, :]
    cos, sin = jnp.cos(ang), jnp.sin(ang)
    pad = HEAD_DIM - ROT_DIM
    cos_head = jnp.concatenate([cos, cos, jnp.ones((seq_len, pad), F32)], axis=1)
    sin_head = jnp.concatenate([-sin, sin, jnp.zeros((seq_len, pad), F32)], axis=1)
    reps = LANES // HEAD_DIM
    return jnp.tile(cos_head, (1, reps)), jnp.tile(sin_head, (1, reps))


def _block_diag(blocks):
    g, n, m = blocks.shape
    eye = jnp.eye(g, dtype=blocks.dtype)
    return (eye[:, None, :, None] * blocks[:, :, None, :]).reshape(g * n, g * m)


def _prepare_layer(l, g_mix, g_mlp, w_in, pool_w, pool_scale, sgu_w, sgu_b, w_out, w_up, w_down):
    return dict(
        g_mix=g_mix[l].reshape(1, D_MODEL),
        g_mlp=g_mlp[l].reshape(1, D_MODEL),
        w_in=w_in[l].astype(BF16),
        pool_w=_block_diag(pool_w[l]).astype(BF16),
        pool_scale=pool_scale[l].reshape(1, POOL_WIDTH),
        sgu_w=sgu_w[l].reshape(SGU_GROUPS * SGU_CHUNK, SGU_CHUNK).astype(BF16),
        sgu_b=jnp.repeat(jnp.transpose(sgu_b[l]), SGU_GROUP_DIM, axis=1),
        w_out=w_out[l].astype(BF16),
        w_up=w_up[l].astype(BF16),
        w_down=w_down[l].astype(BF16),
    )


def _trunk(x, mods, layers, g_final):
    B, S, _ = x.shape
    cos_t, sin_t = _rope_tables(S)
    for l, lw in enumerate(layers):
        mod = mods[l].reshape(B, N_MOD, D_MODEL)
        za, qkv, qkv_folded, yc = _inproj(x, mod, lw["g_mix"], cos_t, sin_t, lw["w_in"],
                                          lw["sgu_w"], lw["sgu_b"])
        state = _folded_attention(qkv_folded)
        x = _tail(qkv, state, x, mod, za, yc, lw["pool_w"], lw["pool_scale"], lw["w_out"],
                  lw["g_mlp"], lw["w_up"], lw["w_down"], g_final, final=(l == DEPTH - 1))
    return x


def kernel(x_prompt, x_sample, c_prompt, c_sample, w_ada, b_ada, g_mix, g_mlp, w_in, pool_w,
           pool_scale, sgu_w, sgu_b, w_out, w_up, w_down, g_final):
    nb_p = c_prompt.shape[0]
    nb_s = c_sample.shape[0]
    rows = -(-(nb_p + nb_s) // 8) * 8
    c_all = jnp.concatenate(
        [c_prompt, c_sample, jnp.zeros((rows - nb_p - nb_s, D_MODEL), F32)], axis=0)
    mods = _ada_mod(c_all, w_ada, b_ada)
    layers = [_prepare_layer(l, g_mix, g_mlp, w_in, pool_w, pool_scale, sgu_w, sgu_b,
                             w_out, w_up, w_down) for l in range(DEPTH)]
    gf = g_final.reshape(1, D_MODEL)
    y_prompt = _trunk(x_prompt, mods[:, :nb_p], layers, gf)
    y_sample = _trunk(x_sample, mods[:, nb_p:nb_p + nb_s], layers, gf)
    return (y_prompt, y_sample)
```

```python
import functools

import jax
import jax.numpy as jnp
from jax import lax
from jax.experimental import pallas as pl
from jax.experimental.pallas import tpu as pltpu

F32 = jnp.float32
BF16 = jnp.bfloat16

D_MODEL = 1024
DEPTH = 2
HEAD_DIM = 64
POOL_WINDOWS = (2, 4, 8, 16)
POOL_WIDTH = D_MODEL // 4
POOL_GROUP_DIM = POOL_WIDTH // len(POOL_WINDOWS)
ATTN_WIDTH = D_MODEL // 2
BAND_HALF = 64
FOLD = 16
MID_DILATION = 4
ROT_DIM = HEAD_DIM // 4
ROPE_THETA = 500000.0
SGU_WIDTH = D_MODEL // 4
SGU_GROUPS = 4
SGU_GROUP_DIM = SGU_WIDTH // SGU_GROUPS
SGU_CHUNK = 128
D_FF = 4 * D_MODEL
N_MOD = 6
EPS = 1e-6
MASK_VALUE = -1e30
Q_SCALE = HEAD_DIM ** -0.5 * 1.4426950408889634

OFF_Q = POOL_WIDTH
OFF_K = OFF_Q + ATTN_WIDTH
OFF_V = OFF_K + ATTN_WIDTH
OFF_C = OFF_V + ATTN_WIDTH
PROJ_WIDTH = OFF_C + 2 * SGU_WIDTH

LANES = 128
HEAD_PAIRS = ATTN_WIDTH // LANES
POOL_HALO = 8
ROW_TILE = 512
INPROJ_TILE = 1024
ATTN_TILE = 1024
Q_BLOCK = 128
K_BLOCK = Q_BLOCK + 2 * BAND_HALF
FF_CHUNK = 1024
SUB_ROWS = 256
N_SHUF = 6
VMEM_LIMIT = 56 * 1024 * 1024


def _const_spec(shape):
    nd = len(shape)
    return pl.BlockSpec(shape, lambda *_: (0,) * nd, pipeline_mode=pl.Buffered(1))


def _ada_kernel(c_ref, w_ref, b_ref, o_ref):
    c = c_ref[...]
    act = c * (1.0 / (1.0 + jnp.exp(-c)))
    o_ref[...] = jnp.dot(act.astype(BF16), w_ref[...].astype(BF16),
                         preferred_element_type=F32) + b_ref[...]


def _ada_mod(c_all, w_ada, b_ada):
    rows = c_all.shape[0]
    return pl.pallas_call(
        _ada_kernel,
        grid=(DEPTH, N_MOD),
        in_specs=[
            pl.BlockSpec((rows, D_MODEL), lambda l, j: (0, 0)),
            pl.BlockSpec((None, D_MODEL, D_MODEL), lambda l, j: (l, 0, j)),
            pl.BlockSpec((None, 1, D_MODEL), lambda l, j: (l, 0, j)),
        ],
        out_specs=pl.BlockSpec((None, rows, D_MODEL), lambda l, j: (l, 0, j)),
        out_shape=jax.ShapeDtypeStruct((DEPTH, rows, N_MOD * D_MODEL), F32),
        compiler_params=pltpu.CompilerParams(vmem_limit_bytes=VMEM_LIMIT),
        name="ada_mod",
    )(c_all, w_ada, b_ada.reshape(DEPTH, 1, N_MOD * D_MODEL))


def _rope_cols(z, cos, sin, low_half):
    up = pltpu.roll(z, LANES - ROT_DIM // 2, axis=1)
    down = pltpu.roll(z, ROT_DIM // 2, axis=1)
    return z * cos + jnp.where(low_half, up, down) * sin


def _store_both(val, nat_ref, fold_ref, scr_a, scr_b, row0, col0):
    sub = val.shape[0]
    cols = slice(col0, col0 + LANES)
    nat_ref[row0:row0 + sub, cols] = val.astype(BF16)
    scr_a[...] = val
    quarter = sub // 4
    for r1 in range(4):
        scr_b[r1 * quarter:(r1 + 1) * quarter, :] = scr_a[pl.ds(r1, quarter, stride=4), :]
    per_res = sub // FOLD
    f0 = row0 // FOLD
    for r1 in range(4):
        for r2 in range(4):
            piece = scr_b[pl.ds(r1 * quarter + r2, per_res, stride=4), :]
            fold_ref[4 * r2 + r1, f0:f0 + per_res, cols] = piece.astype(BF16)


def _group_sums(v, ones_bd):
    hi = v.astype(BF16)
    lo = (v - hi.astype(F32)).astype(BF16)
    return (jnp.dot(hi, ones_bd, preferred_element_type=F32)
            + jnp.dot(lo, ones_bd, preferred_element_type=F32))


def _inproj_kernel(x_ref, mod_ref, g_ref, cos_ref, sin_ref, w_ref, sguw_ref, sgub_ref,
                   za_ref, qkv_ref, qkvf_ref, yc_ref, scr_a, scr_b):
    rows = x_ref.shape[0]
    lane = lax.broadcasted_iota(jnp.int32, (1, LANES), 1)
    low_half = (lane % HEAD_DIM) < (ROT_DIM // 2)
    gi = lax.broadcasted_iota(jnp.int32, (SGU_WIDTH, SGU_WIDTH), 0) // SGU_GROUP_DIM
    gj = lax.broadcasted_iota(jnp.int32, (SGU_WIDTH, SGU_WIDTH), 1) // SGU_GROUP_DIM
    ones_bd = jnp.where(gi == gj, 1.0, 0.0).astype(BF16)
    lane_group = lax.broadcasted_iota(jnp.int32, (1, SGU_WIDTH), 1) // SGU_GROUP_DIM
    bias = sgub_ref[...]
    gain = g_ref[...] * (1.0 + mod_ref[1:2, :])
    shift = mod_ref[0:1, :]

    for st in range(rows // SUB_ROWS):
        row0 = st * SUB_ROWS
        rs = slice(row0, row0 + SUB_ROWS)
        x = x_ref[rs, :]
        xn = x * lax.rsqrt(jnp.mean(x * x, axis=-1, keepdims=True) + EPS)
        h = (xn * gain + shift).astype(BF16)

        cos = cos_ref[rs, :]
        sin = sin_ref[rs, :]
        gate = jax.nn.gelu(jnp.dot(h, w_ref[:, OFF_C:PROJ_WIDTH], preferred_element_type=F32))
        zq = jnp.dot(h, w_ref[:, OFF_Q:OFF_K], preferred_element_type=F32)
        u = gate[:, :SGU_WIDTH]
        v = gate[:, SGU_WIDTH:]
        dv = v - _group_sums(v, ones_bd) * (1.0 / SGU_GROUP_DIM)
        zk = jnp.dot(h, w_ref[:, OFF_K:OFF_V], preferred_element_type=F32)
        var = _group_sums(dv * dv, ones_bd) * (1.0 / SGU_GROUP_DIM)
        zv = jnp.dot(h, w_ref[:, OFF_V:OFF_C], preferred_element_type=F32)
        vn = (dv * lax.rsqrt(var + EPS)).astype(BF16)
        za_ref[rs, :] = jnp.dot(h, w_ref[:, 0:OFF_Q], preferred_element_type=F32)

        for ch in range(SUB_ROWS // SGU_CHUNK):
            sl = slice(ch * SGU_CHUNK, (ch + 1) * SGU_CHUNK)
            mixed = jnp.dot(sguw_ref[...], vn[sl], preferred_element_type=F32)
            vm = mixed[0:SGU_CHUNK]
            for g in range(1, SGU_GROUPS):
                vm = jnp.where(lane_group == g, mixed[g * SGU_CHUNK:(g + 1) * SGU_CHUNK], vm)
            yc_ref[row0 + ch * SGU_CHUNK:row0 + (ch + 1) * SGU_CHUNK, :] = (
                u[sl] * (vm + bias)).astype(BF16)

        for c in range(HEAD_PAIRS):
            slot = (st * HEAD_PAIRS + c) * 3
            cols = slice(c * LANES, (c + 1) * LANES)
            _store_both(_rope_cols(zq[:, cols], cos, sin, low_half) * Q_SCALE, qkv_ref, qkvf_ref,
                        scr_a.at[slot % N_SHUF], scr_b.at[slot % N_SHUF], row0, c * LANES)
            _store_both(_rope_cols(zk[:, cols], cos, sin, low_half), qkv_ref, qkvf_ref,
                        scr_a.at[(slot + 1) % N_SHUF], scr_b.at[(slot + 1) % N_SHUF], row0,
                        ATTN_WIDTH + c * LANES)
            _store_both(zv[:, cols], qkv_ref, qkvf_ref,
                        scr_a.at[(slot + 2) % N_SHUF], scr_b.at[(slot + 2) % N_SHUF], row0,
                        2 * ATTN_WIDTH + c * LANES)


def _inproj(x, mod, g_mix, cos_t, sin_t, w_in, sgu_w, sgu_b):
    B, S, _ = x.shape
    tm = INPROJ_TILE
    assert S % tm == 0 and tm % SUB_ROWS == 0 and SUB_ROWS % (FOLD * 16) == 0
    grid = (B, S // tm)
    tok = lambda width: pl.BlockSpec((None, tm, width), lambda b, i: (b, i, 0))
    qkv_width = 3 * ATTN_WIDTH
    folded = pl.BlockSpec((None, FOLD, tm // FOLD, qkv_width), lambda b, i: (b, 0, i, 0))
    nat_shape = jax.ShapeDtypeStruct((B, S, qkv_width), BF16)
    fold_shape = jax.ShapeDtypeStruct((B, FOLD, S // FOLD, qkv_width), BF16)
    return pl.pallas_call(
        _inproj_kernel,
        grid=grid,
        in_specs=[
            tok(D_MODEL),
            pl.BlockSpec((None, N_MOD, D_MODEL), lambda b, i: (b, 0, 0)),
            _const_spec((1, D_MODEL)),
            pl.BlockSpec((tm, LANES), lambda b, i: (i, 0)),
            pl.BlockSpec((tm, LANES), lambda b, i: (i, 0)),
            _const_spec((D_MODEL, PROJ_WIDTH)),
            _const_spec((SGU_GROUPS * SGU_CHUNK, SGU_CHUNK)),
            _const_spec((SGU_CHUNK, SGU_WIDTH)),
        ],
        out_specs=[tok(POOL_WIDTH), tok(qkv_width), folded, tok(SGU_WIDTH)],
        out_shape=[jax.ShapeDtypeStruct((B, S, POOL_WIDTH), F32),
                   nat_shape, fold_shape,
                   jax.ShapeDtypeStruct((B, S, SGU_WIDTH), BF16)],
        scratch_shapes=[pltpu.VMEM((N_SHUF, SUB_ROWS, LANES), F32)] * 2,
        compiler_params=pltpu.CompilerParams(
            dimension_semantics=("parallel", "parallel"), vmem_limit_bytes=VMEM_LIMIT),
        name="inproj",
    )(x, mod, g_mix, cos_t, sin_t, w_in, sgu_w, sgu_b)


def _attention_tile(q_ref, k_refs, v_refs, state_in, outs, s_scr, order_scr, *,
                    pieces, first, last, seq_len, tile):
    T = pieces
    kp_ref, kc_ref, kn_ref = k_refs
    vp_ref, vc_ref, vn_ref = v_refs
    if not first:
        o_in, lse_in = state_in
    if last:
        (y_out,) = outs
        o_nat, lse_nat, o_mid, lse_mid = order_scr
    else:
        o_out, lse_out = outs
    tl = q_ref.shape[1]
    halo = BAND_HALF // T
    pq = Q_BLOCK // T
    pk = K_BLOCK // T

    def prepare():
        if not last:
            return
        per_res = tl // FOLD
        for c in range(HEAD_PAIRS):
            cols = slice(c * LANES, (c + 1) * LANES)
            for src, mid, dst in ((o_in, o_mid, o_nat), (lse_in, lse_mid, lse_nat)):
                for r1 in range(4):
                    for r2 in range(4):
                        mid[c, r1, pl.ds(r2, per_res, stride=4), :] = (
                            src[4 * r2 + r1, :, cols].astype(F32))
                for r1 in range(4):
                    dst[c, pl.ds(r1, 4 * per_res, stride=4), :] = mid[c, r1]

    row = lax.broadcasted_iota(jnp.int32, (Q_BLOCK, K_BLOCK), 0)
    col = lax.broadcasted_iota(jnp.int32, (Q_BLOCK, K_BLOCK), 1)
    rel = T * (col % pk - halo - row % pq) + (col // pk - row // pq)
    band = jnp.abs(rel) <= BAND_HALF
    lane = lax.broadcasted_iota(jnp.int32, (1, LANES), 1)
    first_head = lane < HEAD_DIM
    zero = jnp.zeros((), BF16)
    ones_block = jnp.ones((K_BLOCK, LANES), BF16)

    def gather(ref, start, size, cols):
        parts = [ref[t, start:start + size, cols] for t in range(T)]
        return parts[0] if T == 1 else jnp.concatenate(parts, axis=0)

    def window(prev_ref, cur_ref, next_ref, la, cols):
        lo, hi = la - halo, la + pq + halo
        parts = []
        for t in range(T):
            if lo < 0:
                parts.append(prev_ref[t, :, cols])
            parts.append(cur_ref[t, max(lo, 0):min(hi, tl), cols])
            if hi > tl:
                parts.append(next_ref[t, :, cols])
        return parts[0] if len(parts) == 1 else jnp.concatenate(parts, axis=0)

    def block_bias(a):
        kidx = tile * tl + a * pq - halo + col % pk
        valid = band & (kidx >= 0) & (kidx < seq_len)
        return jnp.where(valid, 0.0, MASK_VALUE)

    def scores(a, c, bias):
        la = a * pq
        cols = slice(c * LANES, (c + 1) * LANES)
        qp = gather(q_ref, la, pq, cols)
        kp = window(kp_ref, kc_ref, kn_ref, la, cols)
        q2 = jnp.concatenate([jnp.where(first_head, qp, zero),
                              jnp.where(first_head, zero, qp)], axis=0)
        s = lax.dot_general(q2, kp, (((1,), (1,)), ((), ())), preferred_element_type=F32)
        s = s + jnp.concatenate([bias, bias], axis=0)
        s_scr[a % 2, c] = s
        return jnp.max(s, axis=-1, keepdims=True)

    def values(a, c, row_max):
        la = a * pq
        cols = slice(c * LANES, (c + 1) * LANES)
        vp = window(vp_ref, vc_ref, vn_ref, la, cols)
        p = jnp.exp2(s_scr[a % 2, c] - row_max).astype(BF16)
        res = jnp.dot(p, jnp.concatenate([vp, ones_block], axis=1),
                      preferred_element_type=F32)
        pv = jnp.where(first_head, res[:Q_BLOCK, :LANES], res[Q_BLOCK:, :LANES])
        den = jnp.where(first_head, res[:Q_BLOCK, LANES:], res[Q_BLOCK:, LANES:])
        m = jnp.where(first_head, row_max[:Q_BLOCK], row_max[Q_BLOCK:])
        if first:
            o_new = pv / den
            lse_new = m + jnp.log2(den)
        else:
            if last:
                o_prev = o_nat[c, la:la + pq, :]
                lse_prev = lse_nat[c, la:la + pq, :]
            else:
                o_prev = gather(o_in, la, pq, cols).astype(F32)
                lse_prev = gather(lse_in, la, pq, cols)
            top = jnp.maximum(lse_prev, m)
            w_prev = jnp.exp2(lse_prev - top)
            w_cur = jnp.exp2(m - top)
            total = w_prev + w_cur * den
            o_new = (w_prev * o_prev + w_cur * pv) / total
            if not last:
                lse_new = top + jnp.log2(total)
        if last:
            y_out[la:la + pq, cols] = o_new.astype(BF16)
        else:
            for t in range(T):
                o_out[t, la:la + pq, cols] = o_new[t * pq:(t + 1) * pq].astype(BF16)
                lse_out[t, la:la + pq, cols] = lse_new[t * pq:(t + 1) * pq]

    return prepare, block_bias, scores, values, tl // pq


class _PieceView:
    def __init__(self, ref, t):
        self.ref, self.t = ref, t
        self.shape = (1,) + tuple(ref.shape[1:])

    def __getitem__(self, idx):
        _, rows, cols = idx
        return self.ref[self.t, rows, cols]

    def __setitem__(self, idx, value):
        _, rows, cols = idx
        self.ref[self.t, rows, cols] = value


class _RowWindow:
    def __init__(self, ref, start, size):
        self.ref, self.start, self.size = ref, start, size
        self.shape = (ref.shape[0], size, ref.shape[2])

    def __getitem__(self, idx):
        t, rows, cols = idx
        lo = self.start + (rows.start or 0)
        hi = self.start + (self.size if rows.stop is None else rows.stop)
        return self.ref[t, lo:hi, cols]


def _folded_kernel(q_ref, kp_ref, kc_ref, kn_ref, vp_ref, vc_ref, vn_ref, o_out, lse_out,
                   o_mid, lse_mid, s_scr, *, seq_len):
    T = FOLD // MID_DILATION
    tile = pl.program_id(2)
    tl = q_ref.shape[1]
    halo4 = BAND_HALF // T
    wide = []
    for t in range(T):
        piece = lambda ref: _PieceView(ref, t)
        wide.append(_attention_tile(
            piece(q_ref), tuple(map(piece, (kp_ref, kc_ref, kn_ref))),
            tuple(map(piece, (vp_ref, vc_ref, vn_ref))), (), (piece(o_mid), piece(lse_mid)),
            s_scr.at[pl.ds(2 * (t % 2), 2)], (), pieces=1, first=True, last=False, seq_len=seq_len,
            tile=tile))
    inner = lambda prev_ref, nxt_ref: (
        _RowWindow(prev_ref, BAND_HALF - halo4, halo4), None, _RowWindow(nxt_ref, 0, halo4))
    k4 = inner(kp_ref, kn_ref)
    v4 = inner(vp_ref, vn_ref)
    _, bias4, scores4, values4, n4 = _attention_tile(
        q_ref, (k4[0], kc_ref, k4[2]), (v4[0], vc_ref, v4[2]), (o_mid, lse_mid),
        (o_out, lse_out), s_scr.at[pl.ds(4, 2)], (), pieces=T, first=False, last=False,
        seq_len=seq_len, tile=tile)

    def block_steps(bias_fn, scores_fn, values_fn, a):
        maxima = {}

        def score_step():
            bias = bias_fn(a)
            for c in range(HEAD_PAIRS):
                maxima[c] = scores_fn(a, c, bias)

        def value_step():
            for c in range(HEAD_PAIRS):
                values_fn(a, c, maxima[c])

        return [score_step, value_step]

    n16 = tl // Q_BLOCK
    per_group = n4 // n16

    def wide_group(j):
        pairs = []
        for t in range(T):
            _, bias16, scores16, values16, _ = wide[t]
            pairs.append(block_steps(bias16, scores16, values16, j))
        steps = [pairs[0][0]]
        for t in range(T):
            if t + 1 < T:
                steps.append(pairs[t + 1][0])
            steps.append(pairs[t][1])
        return steps

    def narrow_group(j):
        steps = []
        for a in range(j * per_group, (j + 1) * per_group):
            steps += block_steps(bias4, scores4, values4, a)
        return steps

    for j in range(n16):
        for step in wide_group(j) + narrow_group(j):
            step()


def _folded_attention(folded):
    B, _, L, W3 = folded.shape
    W = W3 // 3
    T = FOLD // MID_DILATION
    tl = min(ATTN_TILE // T, L)
    assert L % tl == 0 and tl % Q_BLOCK == 0
    halo_per_tile = tl // BAND_HALF
    n_halo = L // BAND_HALF
    qkv = folded.reshape(B, T, MID_DILATION, L, W3)
    cur = lambda j=0: pl.BlockSpec((None, T, None, tl, W), lambda b, r, i: (b, 0, r, i, j))
    prev = lambda j: pl.BlockSpec(
        (None, T, None, BAND_HALF, W),
        lambda b, r, i: (b, 0, r, jnp.maximum(i * halo_per_tile - 1, 0), j))
    nxt = lambda j: pl.BlockSpec(
        (None, T, None, BAND_HALF, W),
        lambda b, r, i: (b, 0, r, jnp.minimum((i + 1) * halo_per_tile, n_halo - 1), j))
    o, lse = pl.pallas_call(
        functools.partial(_folded_kernel, seq_len=L),
        grid=(B, MID_DILATION, L // tl),
        in_specs=[cur(0), prev(1), cur(1), nxt(1), prev(2), cur(2), nxt(2)],
        out_specs=[cur(), cur()],
        out_shape=[jax.ShapeDtypeStruct((B, T, MID_DILATION, L, W), BF16),
                   jax.ShapeDtypeStruct((B, T, MID_DILATION, L, W), F32)],
        scratch_shapes=[pltpu.VMEM((T, tl, W), BF16), pltpu.VMEM((T, tl, W), F32),
                        pltpu.VMEM((6, HEAD_PAIRS, 2 * Q_BLOCK, K_BLOCK), F32)],
        compiler_params=pltpu.CompilerParams(
            dimension_semantics=("parallel", "parallel", "parallel"),
            vmem_limit_bytes=VMEM_LIMIT),
        name="folded_attn",
    )(*([qkv] * 7))
    return o.reshape(B, FOLD, L, W), lse.reshape(B, FOLD, L, W)


def _pool_mixer(za_ref, zprev_ref, znext_ref, tile, n_tiles, seq_len):
    rows = za_ref.shape[0]
    za = za_ref[...]
    before = jnp.where(tile > 0, zprev_ref[...], 0.0)
    after = jnp.where(tile < n_tiles - 1, znext_ref[...], 0.0)
    ext = jnp.concatenate([before, za, after], axis=0)
    n_ext = rows + 2 * POOL_HALO
    back = lambda a, s: pltpu.roll(a, s, axis=0)
    fwd = lambda a, s: pltpu.roll(a, n_ext - s, axis=0)
    sums = [ext + back(ext, 1)]
    for step in (1, 2, 4):
        sums.append(back(sums[-1], step) + fwd(sums[-1], step))
    lane_group = lax.broadcasted_iota(jnp.int32, (1, POOL_WIDTH), 1) // POOL_GROUP_DIM
    win_sum = sums[0][POOL_HALO:POOL_HALO + rows]
    half = jnp.full((1, POOL_WIDTH), POOL_WINDOWS[0] // 2, jnp.int32)
    for g in range(1, len(POOL_WINDOWS)):
        win_sum = jnp.where(lane_group == g, sums[g][POOL_HALO:POOL_HALO + rows], win_sum)
        half = jnp.where(lane_group == g, POOL_WINDOWS[g] // 2, half)

    def inv_count(row0):
        pos = tile * rows + row0 + lax.broadcasted_iota(jnp.int32, (POOL_HALO, 1), 0)
        cnt = jnp.minimum(pos + half, seq_len) - jnp.maximum(pos - half, 0)
        return 1.0 / cnt.astype(F32)

    edge = POOL_HALO
    mean = jnp.concatenate([
        win_sum[:edge] * inv_count(0),
        win_sum[edge:rows - edge] * (0.5 / half.astype(F32)),
        win_sum[rows - edge:] * inv_count(rows - edge)], axis=0)
    return mean - za


def _tail_kernel(q_ref, kp_ref, kc_ref, kn_ref, vp_ref, vc_ref, vn_ref, o_in, lse_in,
                 x_ref, mod_ref, za_ref, zprev_ref, znext_ref, yc_ref,
                 poolw_ref, pools_ref, wout_ref, gmlp_ref, wup_ref, wdown_ref, gfin_ref,
                 o_ref, yb_scr, s_scr, o_nat, lse_nat, o_mid, lse_mid, *, seq_len, final):
    tile = pl.program_id(1)
    n_tiles = pl.num_programs(1)
    prepare, block_bias, scores, values, n_blocks = _attention_tile(
        q_ref, (kp_ref, kc_ref, kn_ref), (vp_ref, vc_ref, vn_ref), (o_in, lse_in), (yb_scr,),
        s_scr, (o_nat, lse_nat, o_mid, lse_mid),
        pieces=1, first=False, last=True, seq_len=seq_len, tile=tile)
    n_sub = x_ref.shape[0] // SUB_ROWS
    blocks_per_sub = n_blocks // n_sub

    def attention_steps(st):
        steps = []
        for a in range(st * blocks_per_sub, (st + 1) * blocks_per_sub):
            maxima = {}

            def score_step(a=a, maxima=maxima):
                bias = block_bias(a)
                for c in range(HEAD_PAIRS):
                    maxima[c] = scores(a, c, bias)

            def value_step(a=a, maxima=maxima):
                for c in range(HEAD_PAIRS):
                    values(a, c, maxima[c])

            steps += [score_step, value_step]
        return steps

    o1 = POOL_WIDTH
    o2 = o1 + ATTN_WIDTH
    gain = gmlp_ref[...] * (1.0 + mod_ref[4:5, :])
    n_chunks = D_FF // FF_CHUNK
    chunk = lambda c: slice(c * FF_CHUNK, (c + 1) * FF_CHUNK)

    prepare()
    for step in attention_steps(0):
        step()
    p = _pool_mixer(za_ref, zprev_ref, znext_ref, tile, n_tiles, seq_len).astype(BF16)

    for st in range(n_sub):
        rs = slice(st * SUB_ROWS, (st + 1) * SUB_ROWS)
        pending = iter(attention_steps(st + 1) if st + 1 < n_sub else [])
        run_next = lambda: next(pending, lambda: None)()

        y = jnp.dot(yb_scr[rs, :], wout_ref[o1:o2, :], preferred_element_type=F32)
        y = y + jnp.dot(yc_ref[rs, :], wout_ref[o2:, :], preferred_element_type=F32)
        ya = jnp.dot(p[rs], poolw_ref[...], preferred_element_type=F32) * pools_ref[...]
        y = y + jnp.dot(ya.astype(BF16), wout_ref[0:o1, :], preferred_element_type=F32)
        x1 = x_ref[rs, :] + mod_ref[2:3, :] * y
        run_next()

        xn = x1 * lax.rsqrt(jnp.mean(x1 * x1, axis=-1, keepdims=True) + EPS)
        h = (xn * gain + mod_ref[3:4, :]).astype(BF16)
        acc = None
        for c in range(n_chunks):
            up = jnp.dot(h, wup_ref[:, chunk(c)], preferred_element_type=F32)
            run_next()
            act = jnp.square(jnp.maximum(up, 0.0)).astype(BF16)
            part = jnp.dot(act, wdown_ref[chunk(c), :], preferred_element_type=F32)
            acc = part if acc is None else acc + part
        for step in pending:
            step()
        x2 = x1 + mod_ref[5:6, :] * acc
        if final:
            x2 = x2 * lax.rsqrt(jnp.mean(x2 * x2, axis=-1, keepdims=True) + EPS) * gfin_ref[...]
        o_ref[rs, :] = x2


def _tail(qkv, state, x, mod, za, yc, pool_w, pool_scale, w_out, g_mlp, w_up, w_down, g_final,
          final):
    B, S, _ = x.shape
    W = ATTN_WIDTH
    tm = ROW_TILE
    assert S % tm == 0 and tm % SUB_ROWS == 0 and SUB_ROWS % Q_BLOCK == 0
    n_tiles = S // tm
    pool_per_tile = tm // POOL_HALO
    n_pool = S // POOL_HALO
    band_per_tile = tm // BAND_HALF
    n_band = S // BAND_HALF
    tok = lambda width: pl.BlockSpec((None, tm, width), lambda b, i: (b, i, 0))
    cur = lambda j: pl.BlockSpec((None, 1, tm, W), lambda b, i: (b, 0, i, j))
    prev = lambda j: pl.BlockSpec(
        (None, 1, BAND_HALF, W), lambda b, i: (b, 0, jnp.maximum(i * band_per_tile - 1, 0), j))
    nxt = lambda j: pl.BlockSpec(
        (None, 1, BAND_HALF, W),
        lambda b, i: (b, 0, jnp.minimum((i + 1) * band_per_tile, n_band - 1), j))
    st = lambda: pl.BlockSpec((None, FOLD, tm // FOLD, W), lambda b, i: (b, 0, i, 0))
    qkv4 = qkv.reshape(B, 1, S, 3 * W)
    return pl.pallas_call(
        functools.partial(_tail_kernel, seq_len=S, final=final),
        grid=(B, n_tiles),
        in_specs=[
            cur(0), prev(1), cur(1), nxt(1), prev(2), cur(2), nxt(2), st(), st(),
            tok(D_MODEL),
            pl.BlockSpec((None, N_MOD, D_MODEL), lambda b, i: (b, 0, 0)),
            tok(POOL_WIDTH),
            pl.BlockSpec((None, POOL_HALO, POOL_WIDTH),
                         lambda b, i: (b, jnp.maximum(i * pool_per_tile - 1, 0), 0)),
            pl.BlockSpec((None, POOL_HALO, POOL_WIDTH),
                         lambda b, i: (b, jnp.minimum((i + 1) * pool_per_tile, n_pool - 1), 0)),
            tok(SGU_WIDTH),
            _const_spec((POOL_WIDTH, POOL_WIDTH)),
            _const_spec((1, POOL_WIDTH)),
            _const_spec((D_MODEL, D_MODEL)),
            _const_spec((1, D_MODEL)),
            _const_spec((D_MODEL, D_FF)),
            _const_spec((D_FF, D_MODEL)),
            _const_spec((1, D_MODEL)),
        ],
        out_specs=tok(D_MODEL),
        out_shape=jax.ShapeDtypeStruct((B, S, D_MODEL), F32),
        scratch_shapes=[
            pltpu.VMEM((tm, W), BF16),
            pltpu.VMEM((2, HEAD_PAIRS, 2 * Q_BLOCK, K_BLOCK), F32),
            pltpu.VMEM((HEAD_PAIRS, tm, LANES), F32), pltpu.VMEM((HEAD_PAIRS, tm, LANES), F32),
            pltpu.VMEM((HEAD_PAIRS, 4, tm // 4, LANES), F32),
            pltpu.VMEM((HEAD_PAIRS, 4, tm // 4, LANES), F32)],
        compiler_params=pltpu.CompilerParams(
            dimension_semantics=("parallel", "parallel"), vmem_limit_bytes=VMEM_LIMIT),
        name="tail",
    )(qkv4, qkv4, qkv4, qkv4, qkv4, qkv4, qkv4, state[0], state[1],
      x, mod, za, za, za, yc, pool_w, pool_scale, w_out, g_mlp, w_up, w_down, g_final)


def _rope_tables(seq_len):
    inv_freq = ROPE_THETA ** (-jnp.arange(0, ROT_DIM, 2, dtype=F32) / ROT_DIM)
    ang = jnp.arange(seq_len).astype(F32)[:, None] * inv_freq[None, :]
    cos, sin = jnp.cos(ang), jnp.sin(ang)
    pad = HEAD_DIM - ROT_DIM
    cos_head = jnp.concatenate([cos, cos, jnp.ones((seq_len, pad), F32)], axis=1)
    sin_head = jnp.concatenate([-sin, sin, jnp.zeros((seq_len, pad), F32)], axis=1)
    reps = LANES // HEAD_DIM
    return jnp.tile(cos_head, (1, reps)), jnp.tile(sin_head, (1, reps))


def _block_diag(blocks):
    g, n, m = blocks.shape
    eye = jnp.eye(g, dtype=blocks.dtype)
    return (eye[:, None, :, None] * blocks[:, :, None, :]).reshape(g * n, g * m)


def _prepare_layer(l, g_mix, g_mlp, w_in, pool_w, pool_scale, sgu_w, sgu_b, w_out, w_up, w_down):
    return dict(
        g_mix=g_mix[l].reshape(1, D_MODEL),
        g_mlp=g_mlp[l].reshape(1, D_MODEL),
        w_in=w_in[l].astype(BF16),
        pool_w=_block_diag(pool_w[l]).astype(BF16),
        pool_scale=pool_scale[l].reshape(1, POOL_WIDTH),
        sgu_w=sgu_w[l].reshape(SGU_GROUPS * SGU_CHUNK, SGU_CHUNK).astype(BF16),
        sgu_b=jnp.repeat(jnp.transpose(sgu_b[l]), SGU_GROUP_DIM, axis=1),
        w_out=w_out[l].astype(BF16),
        w_up=w_up[l].astype(BF16),
        w_down=w_down[l].astype(BF16),
    )


def _trunk(x, mods, layers, g_final):
    B, S, _ = x.shape
    cos_t, sin_t = _rope_tables(S)
    for l, lw in enumerate(layers):
        mod = mods[l].reshape(B, N_MOD, D_MODEL)
        za, qkv, qkv_folded, yc = _inproj(x, mod, lw["g_mix"], cos_t, sin_t, lw["w_in"],
                                          lw["sgu_w"], lw["sgu_b"])
        state = _folded_attention(qkv_folded)
        x = _tail(qkv, state, x, mod, za, yc, lw["pool_w"], lw["pool_scale"], lw["w_out"],
                  lw["g_mlp"], lw["w_up"], lw["w_down"], g_final, final=(l == DEPTH - 1))
    return x


def kernel(x_prompt, x_sample, c_prompt, c_sample, w_ada, b_ada, g_mix, g_mlp, w_in, pool_w,
           pool_scale, sgu_w, sgu_b, w_out, w_up, w_down, g_final):
    nb_p = c_prompt.shape[0]
    nb_s = c_sample.shape[0]
    rows = -(-(nb_p + nb_s) // 8) * 8
    c_all = jnp.concatenate(
        [c_prompt, c_sample, jnp.zeros((rows - nb_p - nb_s, D_MODEL), F32)], axis=0)
    mods = _ada_mod(c_all, w_ada, b_ada)
    layers = [_prepare_layer(l, g_mix, g_mlp, w_in, pool_w, pool_scale, sgu_w, sgu_b,
                             w_out, w_up, w_down) for l in range(DEPTH)]
    gf = g_final.reshape(1, D_MODEL)
    y_prompt = _trunk(x_prompt, mods[:, :nb_p], layers, gf)
    y_sample = _trunk(x_sample, mods[:, nb_p:nb_p + nb_s], layers, gf)
    return (y_prompt, y_sample)
```

```python
import functools

import jax
import jax.numpy as jnp
from jax import lax
from jax.experimental import pallas as pl
from jax.experimental.pallas import tpu as pltpu

F32 = jnp.float32
BF16 = jnp.bfloat16

D_MODEL = 1024
DEPTH = 2
HEAD_DIM = 64
POOL_WINDOWS = (2, 4, 8, 16)
POOL_WIDTH = D_MODEL // 4
POOL_GROUP_DIM = POOL_WIDTH // len(POOL_WINDOWS)
ATTN_WIDTH = D_MODEL // 2
BAND_HALF = 64
FOLD = 16
MID_DILATION = 4
ROT_DIM = HEAD_DIM // 4
ROPE_THETA = 500000.0
SGU_WIDTH = D_MODEL // 4
SGU_GROUPS = 4
SGU_GROUP_DIM = SGU_WIDTH // SGU_GROUPS
SGU_CHUNK = 128
D_FF = 4 * D_MODEL
N_MOD = 6
EPS = 1e-6
MASK_VALUE = -1e30
Q_SCALE = HEAD_DIM ** -0.5 * 1.4426950408889634

OFF_Q = POOL_WIDTH
OFF_K = OFF_Q + ATTN_WIDTH
OFF_V = OFF_K + ATTN_WIDTH
OFF_C = OFF_V + ATTN_WIDTH
PROJ_WIDTH = OFF_C + 2 * SGU_WIDTH

LANES = 128
HEAD_PAIRS = ATTN_WIDTH // LANES
POOL_HALO = 8
ROW_TILE = 512
INPROJ_TILE = 1024
ATTN_TILE = 1024
Q_BLOCK = 128
K_BLOCK = Q_BLOCK + 2 * BAND_HALF
FF_CHUNK = 1024
SUB_ROWS = 256
VMEM_LIMIT = 56 * 1024 * 1024


def _const_spec(shape):
    nd = len(shape)
    return pl.BlockSpec(shape, lambda *_: (0,) * nd, pipeline_mode=pl.Buffered(1))


def _ada_kernel(c_ref, w_ref, b_ref, o_ref):
    c = c_ref[...]
    act = c * (1.0 / (1.0 + jnp.exp(-c)))
    o_ref[...] = jnp.dot(act.astype(BF16), w_ref[...].astype(BF16),
                         preferred_element_type=F32) + b_ref[...]


def _ada_mod(c_all, w_ada, b_ada):
    rows = c_all.shape[0]
    return pl.pallas_call(
        _ada_kernel,
        grid=(DEPTH, N_MOD),
        in_specs=[
            pl.BlockSpec((rows, D_MODEL), lambda l, j: (0, 0)),
            pl.BlockSpec((None, D_MODEL, D_MODEL), lambda l, j: (l, 0, j)),
            pl.BlockSpec((None, 1, D_MODEL), lambda l, j: (l, 0, j)),
        ],
        out_specs=pl.BlockSpec((None, rows, D_MODEL), lambda l, j: (l, 0, j)),
        out_shape=jax.ShapeDtypeStruct((DEPTH, rows, N_MOD * D_MODEL), F32),
        compiler_params=pltpu.CompilerParams(vmem_limit_bytes=VMEM_LIMIT),
        name="ada_mod",
    )(c_all, w_ada, b_ada.reshape(DEPTH, 1, N_MOD * D_MODEL))


def _rope_cols(z, cos, sin, low_half):
    up = pltpu.roll(z, LANES - ROT_DIM // 2, axis=1)
    down = pltpu.roll(z, ROT_DIM // 2, axis=1)
    return z * cos + jnp.where(low_half, up, down) * sin


def _group_sums(v, ones_bd):
    hi = v.astype(BF16)
    lo = (v - hi.astype(F32)).astype(BF16)
    return (jnp.dot(hi, ones_bd, preferred_element_type=F32)
            + jnp.dot(lo, ones_bd, preferred_element_type=F32))


def _inproj_kernel(x_ref, mod_ref, g_ref, cos_ref, sin_ref, w_ref, sguw_ref, sgub_ref,
                   za_ref, qkv_ref, qkvf_ref, yc_ref):
    rows = x_ref.shape[0]
    lane = lax.broadcasted_iota(jnp.int32, (1, LANES), 1)
    low_half = (lane % HEAD_DIM) < (ROT_DIM // 2)
    gi = lax.broadcasted_iota(jnp.int32, (SGU_WIDTH, SGU_WIDTH), 0) // SGU_GROUP_DIM
    gj = lax.broadcasted_iota(jnp.int32, (SGU_WIDTH, SGU_WIDTH), 1) // SGU_GROUP_DIM
    ones_bd = jnp.where(gi == gj, 1.0, 0.0).astype(BF16)
    lane_group = lax.broadcasted_iota(jnp.int32, (1, SGU_WIDTH), 1) // SGU_GROUP_DIM
    bias = sgub_ref[...]
    gain = g_ref[...] * (1.0 + mod_ref[1:2, :])
    shift = mod_ref[0:1, :]
    out_row = lax.broadcasted_iota(jnp.int32, (SUB_ROWS, SUB_ROWS), 0)
    in_row = lax.broadcasted_iota(jnp.int32, (SUB_ROWS, SUB_ROWS), 1)
    per = SUB_ROWS // FOLD
    fold_perm = jnp.where(in_row == (out_row % per) * FOLD + out_row // per, 1.0, 0.0).astype(BF16)

    for st in range(rows // SUB_ROWS):
        row0 = st * SUB_ROWS
        rs = slice(row0, row0 + SUB_ROWS)
        x = x_ref[rs, :]
        xn = x * lax.rsqrt(jnp.mean(x * x, axis=-1, keepdims=True) + EPS)
        h = (xn * gain + shift).astype(BF16)

        cos = cos_ref[rs, :]
        sin = sin_ref[rs, :]
        gate = jax.nn.gelu(jnp.dot(h, w_ref[:, OFF_C:PROJ_WIDTH], preferred_element_type=F32))
        zq = jnp.dot(h, w_ref[:, OFF_Q:OFF_K], preferred_element_type=F32)
        u = gate[:, :SGU_WIDTH]
        v = gate[:, SGU_WIDTH:]
        dv = v - _group_sums(v, ones_bd) * (1.0 / SGU_GROUP_DIM)
        zk = jnp.dot(h, w_ref[:, OFF_K:OFF_V], preferred_element_type=F32)
        var = _group_sums(dv * dv, ones_bd) * (1.0 / SGU_GROUP_DIM)
        zv = jnp.dot(h, w_ref[:, OFF_V:OFF_C], preferred_element_type=F32)
        vn = (dv * lax.rsqrt(var + EPS)).astype(BF16)
        za_ref[rs, :] = jnp.dot(h, w_ref[:, 0:OFF_Q], preferred_element_type=F32)

        for ch in range(SUB_ROWS // SGU_CHUNK):
            sl = slice(ch * SGU_CHUNK, (ch + 1) * SGU_CHUNK)
            mixed = jnp.dot(sguw_ref[...], vn[sl], preferred_element_type=F32)
            vm = mixed[0:SGU_CHUNK]
            for g in range(1, SGU_GROUPS):
                vm = jnp.where(lane_group == g, mixed[g * SGU_CHUNK:(g + 1) * SGU_CHUNK], vm)
            yc_ref[row0 + ch * SGU_CHUNK:row0 + (ch + 1) * SGU_CHUNK, :] = (
                u[sl] * (vm + bias)).astype(BF16)

        blocks = lambda z, f: jnp.concatenate(
            [f(z[:, c * LANES:(c + 1) * LANES]) for c in range(HEAD_PAIRS)], axis=1).astype(BF16)
        rope = lambda zc: _rope_cols(zc, cos, sin, low_half)
        for j, val in enumerate((blocks(zq, lambda zc: rope(zc) * Q_SCALE), blocks(zk, rope),
                                 zv.astype(BF16))):
            cols = slice(j * ATTN_WIDTH, (j + 1) * ATTN_WIDTH)
            qkv_ref[rs, cols] = val
            folded = jnp.dot(fold_perm, val, preferred_element_type=F32).astype(BF16)
            per_res = SUB_ROWS // FOLD
            for r in range(FOLD):
                qkvf_ref[r, row0 // FOLD:row0 // FOLD + per_res, cols] = (
                    folded[r * per_res:(r + 1) * per_res])


def _inproj(x, mod, g_mix, cos_t, sin_t, w_in, sgu_w, sgu_b):
    B, S, _ = x.shape
    tm = INPROJ_TILE
    assert S % tm == 0 and tm % SUB_ROWS == 0 and SUB_ROWS % (FOLD * 16) == 0
    grid = (B, S // tm)
    tok = lambda width: pl.BlockSpec((None, tm, width), lambda b, i: (b, i, 0))
    qkv_width = 3 * ATTN_WIDTH
    folded = pl.BlockSpec((None, FOLD, tm // FOLD, qkv_width), lambda b, i: (b, 0, i, 0))
    nat_shape = jax.ShapeDtypeStruct((B, S, qkv_width), BF16)
    fold_shape = jax.ShapeDtypeStruct((B, FOLD, S // FOLD, qkv_width), BF16)
    return pl.pallas_call(
        _inproj_kernel,
        grid=grid,
        in_specs=[
            tok(D_MODEL),
            pl.BlockSpec((None, N_MOD, D_MODEL), lambda b, i: (b, 0, 0)),
            _const_spec((1, D_MODEL)),
            pl.BlockSpec((tm, LANES), lambda b, i: (i, 0)),
            pl.BlockSpec((tm, LANES), lambda b, i: (i, 0)),
            _const_spec((D_MODEL, PROJ_WIDTH)),
            _const_spec((SGU_GROUPS * SGU_CHUNK, SGU_CHUNK)),
            _const_spec((SGU_CHUNK, SGU_WIDTH)),
        ],
        out_specs=[tok(POOL_WIDTH), tok(qkv_width), folded, tok(SGU_WIDTH)],
        out_shape=[jax.ShapeDtypeStruct((B, S, POOL_WIDTH), F32),
                   nat_shape, fold_shape,
                   jax.ShapeDtypeStruct((B, S, SGU_WIDTH), BF16)],
        compiler_params=pltpu.CompilerParams(
            dimension_semantics=("parallel", "parallel"), vmem_limit_bytes=VMEM_LIMIT),
        name="inproj",
    )(x, mod, g_mix, cos_t, sin_t, w_in, sgu_w, sgu_b)


def _attention_tile(q_ref, k_refs, v_refs, state_in, outs, s_scr, order_scr, *,
                    pieces, first, last, seq_len, tile):
    T = pieces
    kp_ref, kc_ref, kn_ref = k_refs
    vp_ref, vc_ref, vn_ref = v_refs
    if not first:
        o_in, lse_in = state_in
    if last:
        (y_out,) = outs
        o_nat, lse_nat, o_mid, lse_mid = order_scr
    else:
        o_out, lse_out = outs
    tl = q_ref.shape[1]
    halo = BAND_HALF // T
    pq = Q_BLOCK // T
    pk = K_BLOCK // T

    def prepare():
        if not last:
            return
        per_res = tl // FOLD
        for c in range(HEAD_PAIRS):
            cols = slice(c * LANES, (c + 1) * LANES)
            for src, mid, dst in ((o_in, o_mid, o_nat), (lse_in, lse_mid, lse_nat)):
                for r1 in range(4):
                    for r2 in range(4):
                        mid[c, r1, pl.ds(r2, per_res, stride=4), :] = (
                            src[4 * r2 + r1, :, cols].astype(F32))
                for r1 in range(4):
                    dst[c, pl.ds(r1, 4 * per_res, stride=4), :] = mid[c, r1]

    row = lax.broadcasted_iota(jnp.int32, (Q_BLOCK, K_BLOCK), 0)
    col = lax.broadcasted_iota(jnp.int32, (Q_BLOCK, K_BLOCK), 1)
    rel = T * (col % pk - halo - row % pq) + (col // pk - row // pq)
    band = jnp.abs(rel) <= BAND_HALF
    lane = lax.broadcasted_iota(jnp.int32, (1, LANES), 1)
    first_head = lane < HEAD_DIM
    zero = jnp.zeros((), BF16)
    ones_block = jnp.ones((K_BLOCK, LANES), BF16)

    def gather(ref, start, size, cols):
        parts = [ref[t, start:start + size, cols] for t in range(T)]
        return parts[0] if T == 1 else jnp.concatenate(parts, axis=0)

    def window(prev_ref, cur_ref, next_ref, la, cols):
        lo, hi = la - halo, la + pq + halo
        parts = []
        for t in range(T):
            if lo < 0:
                parts.append(prev_ref[t, :, cols])
            parts.append(cur_ref[t, max(lo, 0):min(hi, tl), cols])
            if hi > tl:
                parts.append(next_ref[t, :, cols])
        return parts[0] if len(parts) == 1 else jnp.concatenate(parts, axis=0)

    def block_bias(a):
        kidx = tile * tl + a * pq - halo + col % pk
        valid = band & (kidx >= 0) & (kidx < seq_len)
        return jnp.where(valid, 0.0, MASK_VALUE)

    def scores(a, c, bias):
        la = a * pq
        cols = slice(c * LANES, (c + 1) * LANES)
        qp = gather(q_ref, la, pq, cols)
        kp = window(kp_ref, kc_ref, kn_ref, la, cols)
        q2 = jnp.concatenate([jnp.where(first_head, qp, zero),
                              jnp.where(first_head, zero, qp)], axis=0)
        s = lax.dot_general(q2, kp, (((1,), (1,)), ((), ())), preferred_element_type=F32)
        s = s + jnp.concatenate([bias, bias], axis=0)
        s_scr[a % 2, c] = s
        return jnp.max(s, axis=-1, keepdims=True)

    def values(a, c, row_max):
        la = a * pq
        cols = slice(c * LANES, (c + 1) * LANES)
        vp = window(vp_ref, vc_ref, vn_ref, la, cols)
        p = jnp.exp2(s_scr[a % 2, c] - row_max).astype(BF16)
        res = jnp.dot(p, jnp.concatenate([vp, ones_block], axis=1),
                      preferred_element_type=F32)
        pv = jnp.where(first_head, res[:Q_BLOCK, :LANES], res[Q_BLOCK:, :LANES])
        den = jnp.where(first_head, res[:Q_BLOCK, LANES:], res[Q_BLOCK:, LANES:])
        m = jnp.where(first_head, row_max[:Q_BLOCK], row_max[Q_BLOCK:])
        if first:
            o_new = pv / den
            lse_new = m + jnp.log2(den)
        else:
            if last:
                o_prev = o_nat[c, la:la + pq, :]
                lse_prev = lse_nat[c, la:la + pq, :]
            else:
                o_prev = gather(o_in, la, pq, cols).astype(F32)
                lse_prev = gather(lse_in, la, pq, cols)
            top = jnp.maximum(lse_prev, m)
            w_prev = jnp.exp2(lse_prev - top)
            w_cur = jnp.exp2(m - top)
            total = w_prev + w_cur * den
            o_new = (w_prev * o_prev + w_cur * pv) / total
            if not last:
                lse_new = top + jnp.log2(total)
        if last:
            y_out[la:la + pq, cols] = o_new.astype(BF16)
        else:
            for t in range(T):
                o_out[t, la:la + pq, cols] = o_new[t * pq:(t + 1) * pq].astype(BF16)
                lse_out[t, la:la + pq, cols] = lse_new[t * pq:(t + 1) * pq]

    return prepare, block_bias, scores, values, tl // pq


class _PieceView:
    def __init__(self, ref, t):
        self.ref, self.t = ref, t
        self.shape = (1,) + tuple(ref.shape[1:])

    def __getitem__(self, idx):
        _, rows, cols = idx
        return self.ref[self.t, rows, cols]

    def __setitem__(self, idx, value):
        _, rows, cols = idx
        self.ref[self.t, rows, cols] = value


class _RowWindow:
    def __init__(self, ref, start, size):
        self.ref, self.start, self.size = ref, start, size
        self.shape = (ref.shape[0], size, ref.shape[2])

    def __getitem__(self, idx):
        t, rows, cols = idx
        lo = self.start + (rows.start or 0)
        hi = self.start + (self.size if rows.stop is None else rows.stop)
        return self.ref[t, lo:hi, cols]


def _folded_kernel(q_ref, kp_ref, kc_ref, kn_ref, vp_ref, vc_ref, vn_ref, o_out, lse_out,
                   o_mid, lse_mid, s_scr, *, seq_len):
    T = FOLD // MID_DILATION
    tile = pl.program_id(2)
    tl = q_ref.shape[1]
    halo4 = BAND_HALF // T
    wide = []
    for t in range(T):
        piece = lambda ref: _PieceView(ref, t)
        wide.append(_attention_tile(
            piece(q_ref), tuple(map(piece, (kp_ref, kc_ref, kn_ref))),
            tuple(map(piece, (vp_ref, vc_ref, vn_ref))), (), (piece(o_mid), piece(lse_mid)),
            s_scr.at[pl.ds(2 * (t % 2), 2)], (), pieces=1, first=True, last=False, seq_len=seq_len,
            tile=tile))
    inner = lambda prev_ref, nxt_ref: (
        _RowWindow(prev_ref, BAND_HALF - halo4, halo4), None, _RowWindow(nxt_ref, 0, halo4))
    k4 = inner(kp_ref, kn_ref)
    v4 = inner(vp_ref, vn_ref)
    _, bias4, scores4, values4, n4 = _attention_tile(
        q_ref, (k4[0], kc_ref, k4[2]), (v4[0], vc_ref, v4[2]), (o_mid, lse_mid),
        (o_out, lse_out), s_scr.at[pl.ds(4, 2)], (), pieces=T, first=False, last=False,
        seq_len=seq_len, tile=tile)

    def block_steps(bias_fn, scores_fn, values_fn, a):
        maxima = {}

        def score_step():
            bias = bias_fn(a)
            for c in range(HEAD_PAIRS):
                maxima[c] = scores_fn(a, c, bias)

        def value_step():
            for c in range(HEAD_PAIRS):
                values_fn(a, c, maxima[c])

        return [score_step, value_step]

    n16 = tl // Q_BLOCK
    per_group = n4 // n16

    def wide_group(j):
        pairs = []
        for t in range(T):
            _, bias16, scores16, values16, _ = wide[t]
            pairs.append(block_steps(bias16, scores16, values16, j))
        steps = [pairs[0][0]]
        for t in range(T):
            if t + 1 < T:
                steps.append(pairs[t + 1][0])
            steps.append(pairs[t][1])
        return steps

    def narrow_group(j):
        steps = []
        for a in range(j * per_group, (j + 1) * per_group):
            steps += block_steps(bias4, scores4, values4, a)
        return steps

    for j in range(n16):
        for step in wide_group(j) + narrow_group(j):
            step()


def _folded_attention(folded):
    B, _, L, W3 = folded.shape
    W = W3 // 3
    T = FOLD // MID_DILATION
    tl = min(ATTN_TILE // T, L)
    assert L % tl == 0 and tl % Q_BLOCK == 0
    halo_per_tile = tl // BAND_HALF
    n_halo = L // BAND_HALF
    qkv = folded.reshape(B, T, MID_DILATION, L, W3)
    cur = lambda j=0: pl.BlockSpec((None, T, None, tl, W), lambda b, r, i: (b, 0, r, i, j))
    prev = lambda j: pl.BlockSpec(
        (None, T, None, BAND_HALF, W),
        lambda b, r, i: (b, 0, r, jnp.maximum(i * halo_per_tile - 1, 0), j))
    nxt = lambda j: pl.BlockSpec(
        (None, T, None, BAND_HALF, W),
        lambda b, r, i: (b, 0, r, jnp.minimum((i + 1) * halo_per_tile, n_halo - 1), j))
    o, lse = pl.pallas_call(
        functools.partial(_folded_kernel, seq_len=L),
        grid=(B, MID_DILATION, L // tl),
        in_specs=[cur(0), prev(1), cur(1), nxt(1), prev(2), cur(2), nxt(2)],
        out_specs=[cur(), cur()],
        out_shape=[jax.ShapeDtypeStruct((B, T, MID_DILATION, L, W), BF16),
                   jax.ShapeDtypeStruct((B, T, MID_DILATION, L, W), F32)],
        scratch_shapes=[pltpu.VMEM((T, tl, W), BF16), pltpu.VMEM((T, tl, W), F32),
                        pltpu.VMEM((6, HEAD_PAIRS, 2 * Q_BLOCK, K_BLOCK), F32)],
        compiler_params=pltpu.CompilerParams(
            dimension_semantics=("parallel", "parallel", "parallel"),
            vmem_limit_bytes=VMEM_LIMIT),
        name="folded_attn",
    )(*([qkv] * 7))
    return o.reshape(B, FOLD, L, W), lse.reshape(B, FOLD, L, W)


def _pool_mixer(za_ref, zprev_ref, znext_ref, tile, n_tiles, seq_len):
    rows = za_ref.shape[0]
    za = za_ref[...]
    before = jnp.where(tile > 0, zprev_ref[...], 0.0)
    after = jnp.where(tile < n_tiles - 1, znext_ref[...], 0.0)
    ext = jnp.concatenate([before, za, after], axis=0)
    n_ext = rows + 2 * POOL_HALO
    back = lambda a, s: pltpu.roll(a, s, axis=0)
    fwd = lambda a, s: pltpu.roll(a, n_ext - s, axis=0)
    sums = [ext + back(ext, 1)]
    for step in (1, 2, 4):
        sums.append(back(sums[-1], step) + fwd(sums[-1], step))
    lane_group = lax.broadcasted_iota(jnp.int32, (1, POOL_WIDTH), 1) // POOL_GROUP_DIM
    win_sum = sums[0][POOL_HALO:POOL_HALO + rows]
    half = jnp.full((1, POOL_WIDTH), POOL_WINDOWS[0] // 2, jnp.int32)
    for g in range(1, len(POOL_WINDOWS)):
        win_sum = jnp.where(lane_group == g, sums[g][POOL_HALO:POOL_HALO + rows], win_sum)
        half = jnp.where(lane_group == g, POOL_WINDOWS[g] // 2, half)

    def inv_count(row0):
        pos = tile * rows + row0 + lax.broadcasted_iota(jnp.int32, (POOL_HALO, 1), 0)
        cnt = jnp.minimum(pos + half, seq_len) - jnp.maximum(pos - half, 0)
        return 1.0 / cnt.astype(F32)

    edge = POOL_HALO
    mean = jnp.concatenate([
        win_sum[:edge] * inv_count(0),
        win_sum[edge:rows - edge] * (0.5 / half.astype(F32)),
        win_sum[rows - edge:] * inv_count(rows - edge)], axis=0)
    return mean - za


def _tail_kernel(q_ref, kp_ref, kc_ref, kn_ref, vp_ref, vc_ref, vn_ref, o_in, lse_in,
                 x_ref, mod_ref, za_ref, zprev_ref, znext_ref, yc_ref,
                 poolw_ref, pools_ref, wout_ref, gmlp_ref, wup_ref, wdown_ref, gfin_ref,
                 o_ref, yb_scr, s_scr, o_nat, lse_nat, o_mid, lse_mid, *, seq_len, final):
    tile = pl.program_id(1)
    n_tiles = pl.num_programs(1)
    prepare, block_bias, scores, values, n_blocks = _attention_tile(
        q_ref, (kp_ref, kc_ref, kn_ref), (vp_ref, vc_ref, vn_ref), (o_in, lse_in), (yb_scr,),
        s_scr, (o_nat, lse_nat, o_mid, lse_mid),
        pieces=1, first=False, last=True, seq_len=seq_len, tile=tile)
    n_sub = x_ref.shape[0] // SUB_ROWS
    blocks_per_sub = n_blocks // n_sub

    def attention_steps(st):
        steps = []
        for a in range(st * blocks_per_sub, (st + 1) * blocks_per_sub):
            maxima = {}

            def score_step(a=a, maxima=maxima):
                bias = block_bias(a)
                for c in range(HEAD_PAIRS):
                    maxima[c] = scores(a, c, bias)

            def value_step(a=a, maxima=maxima):
                for c in range(HEAD_PAIRS):
                    values(a, c, maxima[c])

            steps += [score_step, value_step]
        return steps

    o1 = POOL_WIDTH
    o2 = o1 + ATTN_WIDTH
    gain = gmlp_ref[...] * (1.0 + mod_ref[4:5, :])
    n_chunks = D_FF // FF_CHUNK
    chunk = lambda c: slice(c * FF_CHUNK, (c + 1) * FF_CHUNK)

    prepare()
    for step in attention_steps(0):
        step()
    p = _pool_mixer(za_ref, zprev_ref, znext_ref, tile, n_tiles, seq_len).astype(BF16)

    for st in range(n_sub):
        rs = slice(st * SUB_ROWS, (st + 1) * SUB_ROWS)
        pending = iter(attention_steps(st + 1) if st + 1 < n_sub else [])
        run_next = lambda: next(pending, lambda: None)()

        y = jnp.dot(yb_scr[rs, :], wout_ref[o1:o2, :], preferred_element_type=F32)
        y = y + jnp.dot(yc_ref[rs, :], wout_ref[o2:, :], preferred_element_type=F32)
        ya = jnp.dot(p[rs], poolw_ref[...], preferred_element_type=F32) * pools_ref[...]
        y = y + jnp.dot(ya.astype(BF16), wout_ref[0:o1, :], preferred_element_type=F32)
        x1 = x_ref[rs, :] + mod_ref[2:3, :] * y
        run_next()

        xn = x1 * lax.rsqrt(jnp.mean(x1 * x1, axis=-1, keepdims=True) + EPS)
        h = (xn * gain + mod_ref[3:4, :]).astype(BF16)
        acc = None
        for c in range(n_chunks):
            up = jnp.dot(h, wup_ref[:, chunk(c)], preferred_element_type=F32)
            run_next()
            act = jnp.square(jnp.maximum(up, 0.0)).astype(BF16)
            part = jnp.dot(act, wdown_ref[chunk(c), :], preferred_element_type=F32)
            acc = part if acc is None else acc + part
        for step in pending:
            step()
        x2 = x1 + mod_ref[5:6, :] * acc
        if final:
            x2 = x2 * lax.rsqrt(jnp.mean(x2 * x2, axis=-1, keepdims=True) + EPS) * gfin_ref[...]
        o_ref[rs, :] = x2


def _tail(qkv, state, x, mod, za, yc, pool_w, pool_scale, w_out, g_mlp, w_up, w_down, g_final,
          final):
    B, S, _ = x.shape
    W = ATTN_WIDTH
    tm = ROW_TILE
    assert S % tm == 0 and tm % SUB_ROWS == 0 and SUB_ROWS % Q_BLOCK == 0
    n_tiles = S // tm
    pool_per_tile = tm // POOL_HALO
    n_pool = S // POOL_HALO
    band_per_tile = tm // BAND_HALF
    n_band = S // BAND_HALF
    tok = lambda width: pl.BlockSpec((None, tm, width), lambda b, i: (b, i, 0))
    cur = lambda j: pl.BlockSpec((None, 1, tm, W), lambda b, i: (b, 0, i, j))
    prev = lambda j: pl.BlockSpec(
        (None, 1, BAND_HALF, W), lambda b, i: (b, 0, jnp.maximum(i * band_per_tile - 1, 0), j))
    nxt = lambda j: pl.BlockSpec(
        (None, 1, BAND_HALF, W),
        lambda b, i: (b, 0, jnp.minimum((i + 1) * band_per_tile, n_band - 1), j))
    st = lambda: pl.BlockSpec((None, FOLD, tm // FOLD, W), lambda b, i: (b, 0, i, 0))
    qkv4 = qkv.reshape(B, 1, S, 3 * W)
    return pl.pallas_call(
        functools.partial(_tail_kernel, seq_len=S, final=final),
        grid=(B, n_tiles),
        in_specs=[
            cur(0), prev(1), cur(1), nxt(1), prev(2), cur(2), nxt(2), st(), st(),
            tok(D_MODEL),
            pl.BlockSpec((None, N_MOD, D_MODEL), lambda b, i: (b, 0, 0)),
            tok(POOL_WIDTH),
            pl.BlockSpec((None, POOL_HALO, POOL_WIDTH),
                         lambda b, i: (b, jnp.maximum(i * pool_per_tile - 1, 0), 0)),
            pl.BlockSpec((None, POOL_HALO, POOL_WIDTH),
                         lambda b, i: (b, jnp.minimum((i + 1) * pool_per_tile, n_pool - 1), 0)),
            tok(SGU_WIDTH),
            _const_spec((POOL_WIDTH, POOL_WIDTH)),
            _const_spec((1, POOL_WIDTH)),
            _const_spec((D_MODEL, D_MODEL)),
            _const_spec((1, D_MODEL)),
            _const_spec((D_MODEL, D_FF)),
            _const_spec((D_FF, D_MODEL)),
            _const_spec((1, D_MODEL)),
        ],
        out_specs=tok(D_MODEL),
        out_shape=jax.ShapeDtypeStruct((B, S, D_MODEL), F32),
        scratch_shapes=[
            pltpu.VMEM((tm, W), BF16),
            pltpu.VMEM((2, HEAD_PAIRS, 2 * Q_BLOCK, K_BLOCK), F32),
            pltpu.VMEM((HEAD_PAIRS, tm, LANES), F32), pltpu.VMEM((HEAD_PAIRS, tm, LANES), F32),
            pltpu.VMEM((HEAD_PAIRS, 4, tm // 4, LANES), F32),
            pltpu.VMEM((HEAD_PAIRS, 4, tm // 4, LANES), F32)],
        compiler_params=pltpu.CompilerParams(
            dimension_semantics=("parallel", "parallel"), vmem_limit_bytes=VMEM_LIMIT),
        name="tail",
    )(qkv4, qkv4, qkv4, qkv4, qkv4, qkv4, qkv4, state[0], state[1],
      x, mod, za, za, za, yc, pool_w, pool_scale, w_out, g_mlp, w_up, w_down, g_final)


def _rope_tables(seq_len):
    inv_freq = ROPE_THETA ** (-jnp.arange(0, ROT_DIM, 2, dtype=F32) / ROT_DIM)
    ang = jnp.arange(seq_len).astype(F32)[:, None] * inv_freq[None, :]
    cos, sin = jnp.cos(ang), jnp.sin(ang)
    pad = HEAD_DIM - ROT_DIM
    cos_head = jnp.concatenate([cos, cos, jnp.ones((seq_len, pad), F32)], axis=1)
    sin_head = jnp.concatenate([-sin, sin, jnp.zeros((seq_len, pad), F32)], axis=1)
    reps = LANES // HEAD_DIM
    return jnp.tile(cos_head, (1, reps)), jnp.tile(sin_head, (1, reps))


def _block_diag(blocks):
    g, n, m = blocks.shape
    eye = jnp.eye(g, dtype=blocks.dtype)
    return (eye[:, None, :, None] * blocks[:, :, None, :]).reshape(g * n, g * m)


def _prepare_layer(l, g_mix, g_mlp, w_in, pool_w, pool_scale, sgu_w, sgu_b, w_out, w_up, w_down):
    return dict(
        g_mix=g_mix[l].reshape(1, D_MODEL),
        g_mlp=g_mlp[l].reshape(1, D_MODEL),
        w_in=w_in[l].astype(BF16),
        pool_w=_block_diag(pool_w[l]).astype(BF16),
        pool_scale=pool_scale[l].reshape(1, POOL_WIDTH),
        sgu_w=sgu_w[l].reshape(SGU_GROUPS * SGU_CHUNK, SGU_CHUNK).astype(BF16),
        sgu_b=jnp.repeat(jnp.transpose(sgu_b[l]), SGU_GROUP_DIM, axis=1),
        w_out=w_out[l].astype(BF16),
        w_up=w_up[l].astype(BF16),
        w_down=w_down[l].astype(BF16),
    )


def _trunk(x, mods, layers, g_final):
    B, S, _ = x.shape
    cos_t, sin_t = _rope_tables(S)
    for l, lw in enumerate(layers):
        mod = mods[l].reshape(B, N_MOD, D_MODEL)
        za, qkv, qkv_folded, yc = _inproj(x, mod, lw["g_mix"], cos_t, sin_t, lw["w_in"],
                                          lw["sgu_w"], lw["sgu_b"])
        state = _folded_attention(qkv_folded)
        x = _tail(qkv, state, x, mod, za, yc, lw["pool_w"], lw["pool_scale"], lw["w_out"],
                  lw["g_mlp"], lw["w_up"], lw["w_down"], g_final, final=(l == DEPTH - 1))
    return x


def kernel(x_prompt, x_sample, c_prompt, c_sample, w_ada, b_ada, g_mix, g_mlp, w_in, pool_w,
           pool_scale, sgu_w, sgu_b, w_out, w_up, w_down, g_final):
    nb_p = c_prompt.shape[0]
    nb_s = c_sample.shape[0]
    rows = -(-(nb_p + nb_s) // 8) * 8
    c_all = jnp.concatenate(
        [c_prompt, c_sample, jnp.zeros((rows - nb_p - nb_s, D_MODEL), F32)], axis=0)
    mods = _ada_mod(c_all, w_ada, b_ada)
    layers = [_prepare_layer(l, g_mix, g_mlp, w_in, pool_w, pool_scale, sgu_w, sgu_b,
                             w_out, w_up, w_down) for l in range(DEPTH)]
    gf = g_final.reshape(1, D_MODEL)
    y_prompt = _trunk(x_prompt, mods[:, :nb_p], layers, gf)
    y_sample = _trunk(x_sample, mods[:, nb_p:nb_p + nb_s], layers, gf)
    return (y_prompt, y_sample)
```

```python
import functools

import jax
import jax.numpy as jnp
from jax import lax
from jax.experimental import pallas as pl
from jax.experimental.pallas import tpu as pltpu

F32 = jnp.float32
BF16 = jnp.bfloat16

D_MODEL = 1024
DEPTH = 2
HEAD_DIM = 64
POOL_WINDOWS = (2, 4, 8, 16)
POOL_WIDTH = D_MODEL // 4
POOL_GROUP_DIM = POOL_WIDTH // len(POOL_WINDOWS)
ATTN_WIDTH = D_MODEL // 2
BAND_HALF = 64
FOLD = 16
MID_DILATION = 4
ROT_DIM = HEAD_DIM // 4
ROPE_THETA = 500000.0
SGU_WIDTH = D_MODEL // 4
SGU_GROUPS = 4
SGU_GROUP_DIM = SGU_WIDTH // SGU_GROUPS
SGU_CHUNK = 128
D_FF = 4 * D_MODEL
N_MOD = 6
EPS = 1e-6
MASK_VALUE = -1e30
Q_SCALE = HEAD_DIM ** -0.5 * 1.4426950408889634

OFF_Q = POOL_WIDTH
OFF_K = OFF_Q + ATTN_WIDTH
OFF_V = OFF_K + ATTN_WIDTH
OFF_C = OFF_V + ATTN_WIDTH
PROJ_WIDTH = OFF_C + 2 * SGU_WIDTH

LANES = 128
HEAD_PAIRS = ATTN_WIDTH // LANES
POOL_HALO = 8
ROW_TILE = 512
INPROJ_TILE = 1024
ATTN_TILE = 1024
Q_BLOCK = 128
K_BLOCK = Q_BLOCK + 2 * BAND_HALF
FF_CHUNK = 1024
SUB_ROWS = 256
VMEM_LIMIT = 56 * 1024 * 1024


def _const_spec(shape):
    nd = len(shape)
    return pl.BlockSpec(shape, lambda *_: (0,) * nd, pipeline_mode=pl.Buffered(1))


def _ada_kernel(c_ref, w_ref, b_ref, o_ref):
    c = c_ref[...]
    act = c * (1.0 / (1.0 + jnp.exp(-c)))
    o_ref[...] = jnp.dot(act.astype(BF16), w_ref[...].astype(BF16),
                         preferred_element_type=F32) + b_ref[...]


def _ada_mod(c_all, w_ada, b_ada):
    rows = c_all.shape[0]
    return pl.pallas_call(
        _ada_kernel,
        grid=(DEPTH, N_MOD),
        in_specs=[
            pl.BlockSpec((rows, D_MODEL), lambda l, j: (0, 0)),
            pl.BlockSpec((None, D_MODEL, D_MODEL), lambda l, j: (l, 0, j)),
            pl.BlockSpec((None, 1, D_MODEL), lambda l, j: (l, 0, j)),
        ],
        out_specs=pl.BlockSpec((None, rows, D_MODEL), lambda l, j: (l, 0, j)),
        out_shape=jax.ShapeDtypeStruct((DEPTH, rows, N_MOD * D_MODEL), F32),
        compiler_params=pltpu.CompilerParams(vmem_limit_bytes=VMEM_LIMIT),
        name="ada_mod",
    )(c_all, w_ada, b_ada.reshape(DEPTH, 1, N_MOD * D_MODEL))


def _rope_cols(z, cos, sin, low_half):
    up = pltpu.roll(z, LANES - ROT_DIM // 2, axis=1)
    down = pltpu.roll(z, ROT_DIM // 2, axis=1)
    return z * cos + jnp.where(low_half, up, down) * sin


def _group_sums(v, ones_bd):
    hi = v.astype(BF16)
    lo = (v - hi.astype(F32)).astype(BF16)
    return (jnp.dot(hi, ones_bd, preferred_element_type=F32)
            + jnp.dot(lo, ones_bd, preferred_element_type=F32))


def _inproj_kernel(x_ref, mod_ref, g_ref, cos_ref, sin_ref, w_ref, sguw_ref, sgub_ref,
                   za_ref, qkv_ref, qkvf_ref, yc_ref):
    rows = x_ref.shape[0]
    lane = lax.broadcasted_iota(jnp.int32, (1, LANES), 1)
    low_half = (lane % HEAD_DIM) < (ROT_DIM // 2)
    gi = lax.broadcasted_iota(jnp.int32, (SGU_WIDTH, SGU_WIDTH), 0) // SGU_GROUP_DIM
    gj = lax.broadcasted_iota(jnp.int32, (SGU_WIDTH, SGU_WIDTH), 1) // SGU_GROUP_DIM
    ones_bd = jnp.where(gi == gj, 1.0, 0.0).astype(BF16)
    lane_group = lax.broadcasted_iota(jnp.int32, (1, SGU_WIDTH), 1) // SGU_GROUP_DIM
    bias = sgub_ref[...]
    gain = g_ref[...] * (1.0 + mod_ref[1:2, :])
    shift = mod_ref[0:1, :]
    out_row = lax.broadcasted_iota(jnp.int32, (SUB_ROWS, SUB_ROWS), 0)
    in_row = lax.broadcasted_iota(jnp.int32, (SUB_ROWS, SUB_ROWS), 1)
    per = SUB_ROWS // FOLD
    fold_perm = jnp.where(in_row == (out_row % per) * FOLD + out_row // per, 1.0, 0.0).astype(BF16)

    for st in range(rows // SUB_ROWS):
        row0 = st * SUB_ROWS
        rs = slice(row0, row0 + SUB_ROWS)
        x = x_ref[rs, :]
        xn = x * lax.rsqrt(jnp.mean(x * x, axis=-1, keepdims=True) + EPS)
        h = (xn * gain + shift).astype(BF16)

        cos = cos_ref[rs, :]
        sin = sin_ref[rs, :]
        gate = jax.nn.gelu(jnp.dot(h, w_ref[:, OFF_C:PROJ_WIDTH], preferred_element_type=F32))
        zq = jnp.dot(h, w_ref[:, OFF_Q:OFF_K], preferred_element_type=F32)
        u = gate[:, :SGU_WIDTH]
        v = gate[:, SGU_WIDTH:]
        dv = v - _group_sums(v, ones_bd) * (1.0 / SGU_GROUP_DIM)
        zk = jnp.dot(h, w_ref[:, OFF_K:OFF_V], preferred_element_type=F32)
        var = _group_sums(dv * dv, ones_bd) * (1.0 / SGU_GROUP_DIM)
        zv = jnp.dot(h, w_ref[:, OFF_V:OFF_C], preferred_element_type=F32)
        vn = (dv * lax.rsqrt(var + EPS)).astype(BF16)
        za_ref[rs, :] = jnp.dot(h, w_ref[:, 0:OFF_Q], preferred_element_type=F32)

        for ch in range(SUB_ROWS // SGU_CHUNK):
            sl = slice(ch * SGU_CHUNK, (ch + 1) * SGU_CHUNK)
            mixed = jnp.dot(sguw_ref[...], vn[sl], preferred_element_type=F32)
            vm = mixed[0:SGU_CHUNK]
            for g in range(1, SGU_GROUPS):
                vm = jnp.where(lane_group == g, mixed[g * SGU_CHUNK:(g + 1) * SGU_CHUNK], vm)
            yc_ref[row0 + ch * SGU_CHUNK:row0 + (ch + 1) * SGU_CHUNK, :] = (
                u[sl] * (vm + bias)).astype(BF16)

        blocks = lambda z, f: jnp.concatenate(
            [f(z[:, c * LANES:(c + 1) * LANES]) for c in range(HEAD_PAIRS)], axis=1).astype(BF16)
        rope = lambda zc: _rope_cols(zc, cos, sin, low_half)
        for j, val in enumerate((blocks(zq, lambda zc: rope(zc) * Q_SCALE), blocks(zk, rope),
                                 zv.astype(BF16))):
            cols = slice(j * ATTN_WIDTH, (j + 1) * ATTN_WIDTH)
            qkv_ref[rs, cols] = val
            folded = jnp.dot(fold_perm, val, preferred_element_type=F32).astype(BF16)
            per_res = SUB_ROWS // FOLD
            for r in range(FOLD):
                qkvf_ref[r, row0 // FOLD:row0 // FOLD + per_res, cols] = (
                    folded[r * per_res:(r + 1) * per_res])


def _inproj(x, mod, g_mix, cos_t, sin_t, w_in, sgu_w, sgu_b):
    B, S, _ = x.shape
    tm = INPROJ_TILE
    assert S % tm == 0 and tm % SUB_ROWS == 0 and SUB_ROWS % (FOLD * 16) == 0
    grid = (B, S // tm)
    tok = lambda width: pl.BlockSpec((None, tm, width), lambda b, i: (b, i, 0))
    qkv_width = 3 * ATTN_WIDTH
    folded = pl.BlockSpec((None, FOLD, tm // FOLD, qkv_width), lambda b, i: (b, 0, i, 0))
    nat_shape = jax.ShapeDtypeStruct((B, S, qkv_width), BF16)
    fold_shape = jax.ShapeDtypeStruct((B, FOLD, S // FOLD, qkv_width), BF16)
    return pl.pallas_call(
        _inproj_kernel,
        grid=grid,
        in_specs=[
            tok(D_MODEL),
            pl.BlockSpec((None, N_MOD, D_MODEL), lambda b, i: (b, 0, 0)),
            _const_spec((1, D_MODEL)),
            pl.BlockSpec((tm, LANES), lambda b, i: (i, 0)),
            pl.BlockSpec((tm, LANES), lambda b, i: (i, 0)),
            _const_spec((D_MODEL, PROJ_WIDTH)),
            _const_spec((SGU_GROUPS * SGU_CHUNK, SGU_CHUNK)),
            _const_spec((SGU_CHUNK, SGU_WIDTH)),
        ],
        out_specs=[tok(POOL_WIDTH), tok(qkv_width), folded, tok(SGU_WIDTH)],
        out_shape=[jax.ShapeDtypeStruct((B, S, POOL_WIDTH), F32),
                   nat_shape, fold_shape,
                   jax.ShapeDtypeStruct((B, S, SGU_WIDTH), BF16)],
        compiler_params=pltpu.CompilerParams(
            dimension_semantics=("parallel", "parallel"), vmem_limit_bytes=VMEM_LIMIT),
        name="inproj",
    )(x, mod, g_mix, cos_t, sin_t, w_in, sgu_w, sgu_b)


def _attention_tile(q_ref, k_refs, v_refs, state_in, outs, s_scr, order_scr, *,
                    pieces, first, last, seq_len, tile):
    T = pieces
    kp_ref, kc_ref, kn_ref = k_refs
    vp_ref, vc_ref, vn_ref = v_refs
    if not first:
        o_in, lse_in = state_in
    if last:
        (y_out,) = outs
        o_nat, lse_nat, o_mid, lse_mid = order_scr
    else:
        o_out, lse_out = outs
    tl = q_ref.shape[1]
    halo = BAND_HALF // T
    pq = Q_BLOCK // T
    pk = K_BLOCK // T

    def prepare():
        if not last:
            return
        per_res = tl // FOLD
        for c in range(HEAD_PAIRS):
            cols = slice(c * LANES, (c + 1) * LANES)
            for src, mid, dst in ((o_in, o_mid, o_nat), (lse_in, lse_mid, lse_nat)):
                for r1 in range(4):
                    for r2 in range(4):
                        mid[c, r1, pl.ds(r2, per_res, stride=4), :] = (
                            src[4 * r2 + r1, :, cols].astype(F32))
                for r1 in range(4):
                    dst[c, pl.ds(r1, 4 * per_res, stride=4), :] = mid[c, r1]

    row = lax.broadcasted_iota(jnp.int32, (Q_BLOCK, K_BLOCK), 0)
    col = lax.broadcasted_iota(jnp.int32, (Q_BLOCK, K_BLOCK), 1)
    rel = T * (col % pk - halo - row % pq) + (col // pk - row // pq)
    band = jnp.abs(rel) <= BAND_HALF
    lane = lax.broadcasted_iota(jnp.int32, (1, LANES), 1)
    first_head = lane < HEAD_DIM
    zero = jnp.zeros((), BF16)
    ones_block = jnp.ones((K_BLOCK, LANES), BF16)

    def gather(ref, start, size, cols):
        parts = [ref[t, start:start + size, cols] for t in range(T)]
        return parts[0] if T == 1 else jnp.concatenate(parts, axis=0)

    def window(prev_ref, cur_ref, next_ref, la, cols):
        lo, hi = la - halo, la + pq + halo
        parts = []
        for t in range(T):
            if lo < 0:
                parts.append(prev_ref[t, :, cols])
            parts.append(cur_ref[t, max(lo, 0):min(hi, tl), cols])
            if hi > tl:
                parts.append(next_ref[t, :, cols])
        return parts[0] if len(parts) == 1 else jnp.concatenate(parts, axis=0)

    def block_bias(a):
        kidx = tile * tl + a * pq - halo + col % pk
        valid = band & (kidx >= 0) & (kidx < seq_len)
        return jnp.where(valid, 0.0, MASK_VALUE)

    def scores(a, c, bias):
        la = a * pq
        cols = slice(c * LANES, (c + 1) * LANES)
        qp = gather(q_ref, la, pq, cols)
        kp = window(kp_ref, kc_ref, kn_ref, la, cols)
        q2 = jnp.concatenate([jnp.where(first_head, qp, zero),
                              jnp.where(first_head, zero, qp)], axis=0)
        s = lax.dot_general(q2, kp, (((1,), (1,)), ((), ())), preferred_element_type=F32)
        s = s + jnp.concatenate([bias, bias], axis=0)
        s_scr[a % 2, c] = s
        return jnp.max(s, axis=-1, keepdims=True)

    def values(a, c, row_max):
        la = a * pq
        cols = slice(c * LANES, (c + 1) * LANES)
        vp = window(vp_ref, vc_ref, vn_ref, la, cols)
        p = jnp.exp2(s_scr[a % 2, c] - row_max).astype(BF16)
        res = jnp.dot(p, jnp.concatenate([vp, ones_block], axis=1),
                      preferred_element_type=F32)
        pv = jnp.where(first_head, res[:Q_BLOCK, :LANES], res[Q_BLOCK:, :LANES])
        den = jnp.where(first_head, res[:Q_BLOCK, LANES:], res[Q_BLOCK:, LANES:])
        m = jnp.where(first_head, row_max[:Q_BLOCK], row_max[Q_BLOCK:])
        if first:
            o_new = pv / den
            lse_new = m + jnp.log2(den)
        else:
            if last:
                o_prev = o_nat[c, la:la + pq, :]
                lse_prev = lse_nat[c, la:la + pq, :]
            else:
                o_prev = gather(o_in, la, pq, cols).astype(F32)
                lse_prev = gather(lse_in, la, pq, cols)
            top = jnp.maximum(lse_prev, m)
            w_prev = jnp.exp2(lse_prev - top)
            w_cur = jnp.exp2(m - top)
            total = w_prev + w_cur * den
            o_new = (w_prev * o_prev + w_cur * pv) / total
            if not last:
                lse_new = top + jnp.log2(total)
        if last:
            y_out[la:la + pq, cols] = o_new.astype(BF16)
        else:
            for t in range(T):
                o_out[t, la:la + pq, cols] = o_new[t * pq:(t + 1) * pq].astype(BF16)
                lse_out[t, la:la + pq, cols] = lse_new[t * pq:(t + 1) * pq]

    return prepare, block_bias, scores, values, tl // pq


class _PieceView:
    def __init__(self, ref, t):
        self.ref, self.t = ref, t
        self.shape = (1,) + tuple(ref.shape[1:])

    def __getitem__(self, idx):
        _, rows, cols = idx
        return self.ref[self.t, rows, cols]

    def __setitem__(self, idx, value):
        _, rows, cols = idx
        self.ref[self.t, rows, cols] = value


class _RowWindow:
    def __init__(self, ref, start, size):
        self.ref, self.start, self.size = ref, start, size
        self.shape = (ref.shape[0], size, ref.shape[2])

    def __getitem__(self, idx):
        t, rows, cols = idx
        lo = self.start + (rows.start or 0)
        hi = self.start + (self.size if rows.stop is None else rows.stop)
        return self.ref[t, lo:hi, cols]


def _folded_kernel(q_ref, kp_ref, kc_ref, kn_ref, vp_ref, vc_ref, vn_ref, o_out, lse_out,
                   o_mid, lse_mid, s_scr, *, seq_len):
    T = FOLD // MID_DILATION
    tile = pl.program_id(2)
    tl = q_ref.shape[1]
    halo4 = BAND_HALF // T
    wide = []
    for t in range(T):
        piece = lambda ref: _PieceView(ref, t)
        wide.append(_attention_tile(
            piece(q_ref), tuple(map(piece, (kp_ref, kc_ref, kn_ref))),
            tuple(map(piece, (vp_ref, vc_ref, vn_ref))), (), (piece(o_mid), piece(lse_mid)),
            s_scr.at[pl.ds(2 * (t % 2), 2)], (), pieces=1, first=True, last=False, seq_len=seq_len,
            tile=tile))
    inner = lambda prev_ref, nxt_ref: (
        _RowWindow(prev_ref, BAND_HALF - halo4, halo4), None, _RowWindow(nxt_ref, 0, halo4))
    k4 = inner(kp_ref, kn_ref)
    v4 = inner(vp_ref, vn_ref)
    _, bias4, scores4, values4, n4 = _attention_tile(
        q_ref, (k4[0], kc_ref, k4[2]), (v4[0], vc_ref, v4[2]), (o_mid, lse_mid),
        (o_out, lse_out), s_scr.at[pl.ds(4, 2)], (), pieces=T, first=False, last=False,
        seq_len=seq_len, tile=tile)

    def block_steps(bias_fn, scores_fn, values_fn, a):
        maxima = {}

        def score_step():
            bias = bias_fn(a)
            for c in range(HEAD_PAIRS):
                maxima[c] = scores_fn(a, c, bias)

        def value_step():
            for c in range(HEAD_PAIRS):
                values_fn(a, c, maxima[c])

        return [score_step, value_step]

    n16 = tl // Q_BLOCK
    per_group = n4 // n16

    def wide_group(j):
        pairs = []
        for t in range(T):
            _, bias16, scores16, values16, _ = wide[t]
            pairs.append(block_steps(bias16, scores16, values16, j))
        steps = [pairs[0][0]]
        for t in range(T):
            if t + 1 < T:
                steps.append(pairs[t + 1][0])
            steps.append(pairs[t][1])
        return steps

    def narrow_group(j):
        steps = []
        for a in range(j * per_group, (j + 1) * per_group):
            steps += block_steps(bias4, scores4, values4, a)
        return steps

    for j in range(n16):
        for step in wide_group(j) + narrow_group(j):
            step()


def _folded_attention(folded):
    B, _, L, W3 = folded.shape
    W = W3 // 3
    T = FOLD // MID_DILATION
    tl = min(ATTN_TILE // T, L)
    assert L % tl == 0 and tl % Q_BLOCK == 0
    halo_per_tile = tl // BAND_HALF
    n_halo = L // BAND_HALF
    qkv = folded.reshape(B, T, MID_DILATION, L, W3)
    cur = lambda j=0: pl.BlockSpec((None, T, None, tl, W), lambda b, r, i: (b, 0, r, i, j))
    prev = lambda j: pl.BlockSpec(
        (None, T, None, BAND_HALF, W),
        lambda b, r, i: (b, 0, r, jnp.maximum(i * halo_per_tile - 1, 0), j))
    nxt = lambda j: pl.BlockSpec(
        (None, T, None, BAND_HALF, W),
        lambda b, r, i: (b, 0, r, jnp.minimum((i + 1) * halo_per_tile, n_halo - 1), j))
    o, lse = pl.pallas_call(
        functools.partial(_folded_kernel, seq_len=L),
        grid=(B, MID_DILATION, L // tl),
        in_specs=[cur(0), prev(1), cur(1), nxt(1), prev(2), cur(2), nxt(2)],
        out_specs=[cur(), cur()],
        out_shape=[jax.ShapeDtypeStruct((B, T, MID_DILATION, L, W), BF16),
                   jax.ShapeDtypeStruct((B, T, MID_DILATION, L, W), F32)],
        scratch_shapes=[pltpu.VMEM((T, tl, W), BF16), pltpu.VMEM((T, tl, W), F32),
                        pltpu.VMEM((6, HEAD_PAIRS, 2 * Q_BLOCK, K_BLOCK), F32)],
        compiler_params=pltpu.CompilerParams(
            dimension_semantics=("parallel", "parallel", "parallel"),
            vmem_limit_bytes=VMEM_LIMIT),
        name="folded_attn",
    )(*([qkv] * 7))
    return o.reshape(B, FOLD, L, W), lse.reshape(B, FOLD, L, W)


def _pool_mixer(za_ref, zprev_ref, znext_ref, tile, n_tiles, seq_len):
    rows = za_ref.shape[0]
    za = za_ref[...]
    before = jnp.where(tile > 0, zprev_ref[...], 0.0)
    after = jnp.where(tile < n_tiles - 1, znext_ref[...], 0.0)
    ext = jnp.concatenate([before, za, after], axis=0)
    n_ext = rows + 2 * POOL_HALO
    back = lambda a, s: pltpu.roll(a, s, axis=0)
    fwd = lambda a, s: pltpu.roll(a, n_ext - s, axis=0)
    sums = [ext + back(ext, 1)]
    for step in (1, 2, 4):
        sums.append(back(sums[-1], step) + fwd(sums[-1], step))
    lane_group = lax.broadcasted_iota(jnp.int32, (1, POOL_WIDTH), 1) // POOL_GROUP_DIM
    win_sum = sums[0][POOL_HALO:POOL_HALO + rows]
    half = jnp.full((1, POOL_WIDTH), POOL_WINDOWS[0] // 2, jnp.int32)
    for g in range(1, len(POOL_WINDOWS)):
        win_sum = jnp.where(lane_group == g, sums[g][POOL_HALO:POOL_HALO + rows], win_sum)
        half = jnp.where(lane_group == g, POOL_WINDOWS[g] // 2, half)

    def inv_count(row0):
        pos = tile * rows + row0 + lax.broadcasted_iota(jnp.int32, (POOL_HALO, 1), 0)
        cnt = jnp.minimum(pos + half, seq_len) - jnp.maximum(pos - half, 0)
        return 1.0 / cnt.astype(F32)

    edge = POOL_HALO
    mean = jnp.concatenate([
        win_sum[:edge] * inv_count(0),
        win_sum[edge:rows - edge] * (0.5 / half.astype(F32)),
        win_sum[rows - edge:] * inv_count(rows - edge)], axis=0)
    return mean - za


def _tail_kernel(q_ref, kp_ref, kc_ref, kn_ref, vp_ref, vc_ref, vn_ref, o_in, lse_in,
                 x_ref, mod_ref, za_ref, zprev_ref, znext_ref, yc_ref,
                 poolw_ref, pools_ref, wout_ref, gmlp_ref, wup_ref, wdown_ref, gfin_ref,
                 o_ref, yb_scr, s_scr, o_nat, lse_nat, o_mid, lse_mid, *, seq_len, final):
    tile = pl.program_id(1)
    n_tiles = pl.num_programs(1)
    prepare, block_bias, scores, values, n_blocks = _attention_tile(
        q_ref, (kp_ref, kc_ref, kn_ref), (vp_ref, vc_ref, vn_ref), (o_in, lse_in), (yb_scr,),
        s_scr, (o_nat, lse_nat, o_mid, lse_mid),
        pieces=1, first=False, last=True, seq_len=seq_len, tile=tile)
    n_sub = x_ref.shape[0] // SUB_ROWS
    blocks_per_sub = n_blocks // n_sub

    def attention_steps(st):
        steps = []
        for a in range(st * blocks_per_sub, (st + 1) * blocks_per_sub):
            maxima = {}

            def score_step(a=a, maxima=maxima):
                bias = block_bias(a)
                for c in range(HEAD_PAIRS):
                    maxima[c] = scores(a, c, bias)

            def value_step(a=a, maxima=maxima):
                for c in range(HEAD_PAIRS):
                    values(a, c, maxima[c])

            steps += [score_step, value_step]
        return steps

    o1 = POOL_WIDTH
    o2 = o1 + ATTN_WIDTH
    gain = gmlp_ref[...] * (1.0 + mod_ref[4:5, :])
    n_chunks = D_FF // FF_CHUNK
    chunk = lambda c: slice(c * FF_CHUNK, (c + 1) * FF_CHUNK)

    prepare()
    for step in attention_steps(0):
        step()
    p = _pool_mixer(za_ref, zprev_ref, znext_ref, tile, n_tiles, seq_len).astype(BF16)

    def mixer_residual(st):
        rs = slice(st * SUB_ROWS, (st + 1) * SUB_ROWS)
        y = jnp.dot(yb_scr[rs, :], wout_ref[o1:o2, :], preferred_element_type=F32)
        y = y + jnp.dot(yc_ref[rs, :], wout_ref[o2:, :], preferred_element_type=F32)
        ya = jnp.dot(p[rs], poolw_ref[...], preferred_element_type=F32) * pools_ref[...]
        y = y + jnp.dot(ya.astype(BF16), wout_ref[0:o1, :], preferred_element_type=F32)
        return x_ref[rs, :] + mod_ref[2:3, :] * y

    x1 = mixer_residual(0)
    for st in range(n_sub):
        rs = slice(st * SUB_ROWS, (st + 1) * SUB_ROWS)
        pending = iter(attention_steps(st + 1) if st + 1 < n_sub else [])
        run_next = lambda: next(pending, lambda: None)()
        run_next()

        xn = x1 * lax.rsqrt(jnp.mean(x1 * x1, axis=-1, keepdims=True) + EPS)
        h = (xn * gain + mod_ref[3:4, :]).astype(BF16)
        acc = None
        x1_next = None
        for c in range(n_chunks):
            up = jnp.dot(h, wup_ref[:, chunk(c)], preferred_element_type=F32)
            run_next()
            if c == n_chunks - 1:
                for step in pending:
                    step()
                if st + 1 < n_sub:
                    x1_next = mixer_residual(st + 1)
            act = jnp.square(jnp.maximum(up, 0.0)).astype(BF16)
            part = jnp.dot(act, wdown_ref[chunk(c), :], preferred_element_type=F32)
            acc = part if acc is None else acc + part
        x2 = x1 + mod_ref[5:6, :] * acc
        if final:
            x2 = x2 * lax.rsqrt(jnp.mean(x2 * x2, axis=-1, keepdims=True) + EPS) * gfin_ref[...]
        o_ref[rs, :] = x2
        x1 = x1_next


def _tail(qkv, state, x, mod, za, yc, pool_w, pool_scale, w_out, g_mlp, w_up, w_down, g_final,
          final):
    B, S, _ = x.shape
    W = ATTN_WIDTH
    tm = ROW_TILE
    assert S % tm == 0 and tm % SUB_ROWS == 0 and SUB_ROWS % Q_BLOCK == 0
    n_tiles = S // tm
    pool_per_tile = tm // POOL_HALO
    n_pool = S // POOL_HALO
    band_per_tile = tm // BAND_HALF
    n_band = S // BAND_HALF
    tok = lambda width: pl.BlockSpec((None, tm, width), lambda b, i: (b, i, 0))
    cur = lambda j: pl.BlockSpec((None, 1, tm, W), lambda b, i: (b, 0, i, j))
    prev = lambda j: pl.BlockSpec(
        (None, 1, BAND_HALF, W), lambda b, i: (b, 0, jnp.maximum(i * band_per_tile - 1, 0), j))
    nxt = lambda j: pl.BlockSpec(
        (None, 1, BAND_HALF, W),
        lambda b, i: (b, 0, jnp.minimum((i + 1) * band_per_tile, n_band - 1), j))
    st = lambda: pl.BlockSpec((None, FOLD, tm // FOLD, W), lambda b, i: (b, 0, i, 0))
    qkv4 = qkv.reshape(B, 1, S, 3 * W)
    return pl.pallas_call(
        functools.partial(_tail_kernel, seq_len=S, final=final),
        grid=(B, n_tiles),
        in_specs=[
            cur(0), prev(1), cur(1), nxt(1), prev(2), cur(2), nxt(2), st(), st(),
            tok(D_MODEL),
            pl.BlockSpec((None, N_MOD, D_MODEL), lambda b, i: (b, 0, 0)),
            tok(POOL_WIDTH),
            pl.BlockSpec((None, POOL_HALO, POOL_WIDTH),
                         lambda b, i: (b, jnp.maximum(i * pool_per_tile - 1, 0), 0)),
            pl.BlockSpec((None, POOL_HALO, POOL_WIDTH),
                         lambda b, i: (b, jnp.minimum((i + 1) * pool_per_tile, n_pool - 1), 0)),
            tok(SGU_WIDTH),
            _const_spec((POOL_WIDTH, POOL_WIDTH)),
            _const_spec((1, POOL_WIDTH)),
            _const_spec((D_MODEL, D_MODEL)),
            _const_spec((1, D_MODEL)),
            _const_spec((D_MODEL, D_FF)),
            _const_spec((D_FF, D_MODEL)),
            _const_spec((1, D_MODEL)),
        ],
        out_specs=tok(D_MODEL),
        out_shape=jax.ShapeDtypeStruct((B, S, D_MODEL), F32),
        scratch_shapes=[
            pltpu.VMEM((tm, W), BF16),
            pltpu.VMEM((2, HEAD_PAIRS, 2 * Q_BLOCK, K_BLOCK), F32),
            pltpu.VMEM((HEAD_PAIRS, tm, LANES), F32), pltpu.VMEM((HEAD_PAIRS, tm, LANES), F32),
            pltpu.VMEM((HEAD_PAIRS, 4, tm // 4, LANES), F32),
            pltpu.VMEM((HEAD_PAIRS, 4, tm // 4, LANES), F32)],
        compiler_params=pltpu.CompilerParams(
            dimension_semantics=("parallel", "parallel"), vmem_limit_bytes=VMEM_LIMIT),
        name="tail",
    )(qkv4, qkv4, qkv4, qkv4, qkv4, qkv4, qkv4, state[0], state[1],
      x, mod, za, za, za, yc, pool_w, pool_scale, w_out, g_mlp, w_up, w_down, g_final)


def _rope_tables(seq_len):
    inv_freq = ROPE_THETA ** (-jnp.arange(0, ROT_DIM, 2, dtype=F32) / ROT_DIM)
    ang = jnp.arange(seq_len).astype(F32)[:, None] * inv_freq[None, :]
    cos, sin = jnp.cos(ang), jnp.sin(ang)
    pad = HEAD_DIM - ROT_DIM
    cos_head = jnp.concatenate([cos, cos, jnp.ones((seq_len, pad), F32)], axis=1)
    sin_head = jnp.concatenate([-sin, sin, jnp.zeros((seq_len, pad), F32)], axis=1)
    reps = LANES // HEAD_DIM
    return jnp.tile(cos_head, (1, reps)), jnp.tile(sin_head, (1, reps))


def _block_diag(blocks):
    g, n, m = blocks.shape
    eye = jnp.eye(g, dtype=blocks.dtype)
    return (eye[:, None, :, None] * blocks[:, :, None, :]).reshape(g * n, g * m)


def _prepare_layer(l, g_mix, g_mlp, w_in, pool_w, pool_scale, sgu_w, sgu_b, w_out, w_up, w_down):
    return dict(
        g_mix=g_mix[l].reshape(1, D_MODEL),
        g_mlp=g_mlp[l].reshape(1, D_MODEL),
        w_in=w_in[l].astype(BF16),
        pool_w=_block_diag(pool_w[l]).astype(BF16),
        pool_scale=pool_scale[l].reshape(1, POOL_WIDTH),
        sgu_w=sgu_w[l].reshape(SGU_GROUPS * SGU_CHUNK, SGU_CHUNK).astype(BF16),
        sgu_b=jnp.repeat(jnp.transpose(sgu_b[l]), SGU_GROUP_DIM, axis=1),
        w_out=w_out[l].astype(BF16),
        w_up=w_up[l].astype(BF16),
        w_down=w_down[l].astype(BF16),
    )


def _trunk(x, mods, layers, g_final):
    B, S, _ = x.shape
    cos_t, sin_t = _rope_tables(S)
    for l, lw in enumerate(layers):
        mod = mods[l].reshape(B, N_MOD, D_MODEL)
        za, qkv, qkv_folded, yc = _inproj(x, mod, lw["g_mix"], cos_t, sin_t, lw["w_in"],
                                          lw["sgu_w"], lw["sgu_b"])
        state = _folded_attention(qkv_folded)
        x = _tail(qkv, state, x, mod, za, yc, lw["pool_w"], lw["pool_scale"], lw["w_out"],
                  lw["g_mlp"], lw["w_up"], lw["w_down"], g_final, final=(l == DEPTH - 1))
    return x


def kernel(x_prompt, x_sample, c_prompt, c_sample, w_ada, b_ada, g_mix, g_mlp, w_in, pool_w,
           pool_scale, sgu_w, sgu_b, w_out, w_up, w_down, g_final):
    nb_p = c_prompt.shape[0]
    nb_s = c_sample.shape[0]
    rows = -(-(nb_p + nb_s) // 8) * 8
    c_all = jnp.concatenate(
        [c_prompt, c_sample, jnp.zeros((rows - nb_p - nb_s, D_MODEL), F32)], axis=0)
    mods = _ada_mod(c_all, w_ada, b_ada)
    layers = [_prepare_layer(l, g_mix, g_mlp, w_in, pool_w, pool_scale, sgu_w, sgu_b,
                             w_out, w_up, w_down) for l in range(DEPTH)]
    gf = g_final.reshape(1, D_MODEL)
    y_prompt = _trunk(x_prompt, mods[:, :nb_p], layers, gf)
    y_sample = _trunk(x_sample, mods[:, nb_p:nb_p + nb_s], layers, gf)
    return (y_prompt, y_sample)
```

```python
import functools

import jax
import jax.numpy as jnp
from jax import lax
from jax.experimental import pallas as pl
from jax.experimental.pallas import tpu as pltpu

F32 = jnp.float32
BF16 = jnp.bfloat16

D_MODEL = 1024
DEPTH = 2
HEAD_DIM = 64
POOL_WINDOWS = (2, 4, 8, 16)
POOL_WIDTH = D_MODEL // 4
POOL_GROUP_DIM = POOL_WIDTH // len(POOL_WINDOWS)
ATTN_WIDTH = D_MODEL // 2
BAND_HALF = 64
FOLD = 16
MID_DILATION = 4
ROT_DIM = HEAD_DIM // 4
ROPE_THETA = 500000.0
SGU_WIDTH = D_MODEL // 4
SGU_GROUPS = 4
SGU_GROUP_DIM = SGU_WIDTH // SGU_GROUPS
SGU_CHUNK = 128
D_FF = 4 * D_MODEL
N_MOD = 6
EPS = 1e-6
MASK_VALUE = -1e30
Q_SCALE = HEAD_DIM ** -0.5 * 1.4426950408889634

OFF_Q = POOL_WIDTH
OFF_K = OFF_Q + ATTN_WIDTH
OFF_V = OFF_K + ATTN_WIDTH
OFF_C = OFF_V + ATTN_WIDTH
PROJ_WIDTH = OFF_C + 2 * SGU_WIDTH

LANES = 128
HEAD_PAIRS = ATTN_WIDTH // LANES
POOL_HALO = 8
ROW_TILE = 512
INPROJ_TILE = 1024
ATTN_TILE = 1024
Q_BLOCK = 128
K_BLOCK = Q_BLOCK + 2 * BAND_HALF
FF_CHUNK = 1024
SUB_ROWS = 256
VMEM_LIMIT = 56 * 1024 * 1024


def _const_spec(shape):
    nd = len(shape)
    return pl.BlockSpec(shape, lambda *_: (0,) * nd, pipeline_mode=pl.Buffered(1))


def _ada_kernel(c_ref, w_ref, b_ref, o_ref):
    c = c_ref[...]
    act = c * (1.0 / (1.0 + jnp.exp(-c)))
    o_ref[...] = jnp.dot(act.astype(BF16), w_ref[...].astype(BF16),
                         preferred_element_type=F32) + b_ref[...]


def _ada_mod(c_all, w_ada, b_ada):
    rows = c_all.shape[0]
    return pl.pallas_call(
        _ada_kernel,
        grid=(DEPTH, N_MOD),
        in_specs=[
            pl.BlockSpec((rows, D_MODEL), lambda l, j: (0, 0)),
            pl.BlockSpec((None, D_MODEL, D_MODEL), lambda l, j: (l, 0, j)),
            pl.BlockSpec((None, 1, D_MODEL), lambda l, j: (l, 0, j)),
        ],
        out_specs=pl.BlockSpec((None, rows, D_MODEL), lambda l, j: (l, 0, j)),
        out_shape=jax.ShapeDtypeStruct((DEPTH, rows, N_MOD * D_MODEL), F32),
        compiler_params=pltpu.CompilerParams(vmem_limit_bytes=VMEM_LIMIT),
        name="ada_mod",
    )(c_all, w_ada, b_ada.reshape(DEPTH, 1, N_MOD * D_MODEL))


def _rope_cols(z, cos, sin, low_half):
    up = pltpu.roll(z, LANES - ROT_DIM // 2, axis=1)
    down = pltpu.roll(z, ROT_DIM // 2, axis=1)
    return z * cos + jnp.where(low_half, up, down) * sin


def _group_sums(v, ones_bd):
    hi = v.astype(BF16)
    lo = (v - hi.astype(F32)).astype(BF16)
    return (jnp.dot(hi, ones_bd, preferred_element_type=F32)
            + jnp.dot(lo, ones_bd, preferred_element_type=F32))


def _inproj_kernel(x_ref, mod_ref, g_ref, cos_ref, sin_ref, w_ref, sguw_ref, sgub_ref,
                   za_ref, qkv_ref, qkvf_ref, yc_ref):
    rows = x_ref.shape[0]
    lane = lax.broadcasted_iota(jnp.int32, (1, LANES), 1)
    low_half = (lane % HEAD_DIM) < (ROT_DIM // 2)
    gi = lax.broadcasted_iota(jnp.int32, (SGU_WIDTH, SGU_WIDTH), 0) // SGU_GROUP_DIM
    gj = lax.broadcasted_iota(jnp.int32, (SGU_WIDTH, SGU_WIDTH), 1) // SGU_GROUP_DIM
    ones_bd = jnp.where(gi == gj, 1.0, 0.0).astype(BF16)
    lane_group = lax.broadcasted_iota(jnp.int32, (1, SGU_WIDTH), 1) // SGU_GROUP_DIM
    bias = sgub_ref[...]
    gain = g_ref[...] * (1.0 + mod_ref[1:2, :])
    shift = mod_ref[0:1, :]
    out_row = lax.broadcasted_iota(jnp.int32, (SUB_ROWS, SUB_ROWS), 0)
    in_row = lax.broadcasted_iota(jnp.int32, (SUB_ROWS, SUB_ROWS), 1)
    per = SUB_ROWS // FOLD
    fold_perm = jnp.where(in_row == (out_row % per) * FOLD + out_row // per, 1.0, 0.0).astype(BF16)

    for st in range(rows // SUB_ROWS):
        row0 = st * SUB_ROWS
        rs = slice(row0, row0 + SUB_ROWS)
        x = x_ref[rs, :]
        xn = x * lax.rsqrt(jnp.mean(x * x, axis=-1, keepdims=True) + EPS)
        h = (xn * gain + shift).astype(BF16)

        cos = cos_ref[rs, :]
        sin = sin_ref[rs, :]
        gate = jax.nn.gelu(jnp.dot(h, w_ref[:, OFF_C:PROJ_WIDTH], preferred_element_type=F32))
        zq = jnp.dot(h, w_ref[:, OFF_Q:OFF_K], preferred_element_type=F32)
        u = gate[:, :SGU_WIDTH]
        v = gate[:, SGU_WIDTH:]
        dv = v - _group_sums(v, ones_bd) * (1.0 / SGU_GROUP_DIM)
        zk = jnp.dot(h, w_ref[:, OFF_K:OFF_V], preferred_element_type=F32)
        var = _group_sums(dv * dv, ones_bd) * (1.0 / SGU_GROUP_DIM)
        zv = jnp.dot(h, w_ref[:, OFF_V:OFF_C], preferred_element_type=F32)
        vn = (dv * lax.rsqrt(var + EPS)).astype(BF16)
        za_ref[rs, :] = jnp.dot(h, w_ref[:, 0:OFF_Q], preferred_element_type=F32)

        for ch in range(SUB_ROWS // SGU_CHUNK):
            sl = slice(ch * SGU_CHUNK, (ch + 1) * SGU_CHUNK)
            mixed = jnp.dot(sguw_ref[...], vn[sl], preferred_element_type=F32)
            vm = mixed[0:SGU_CHUNK]
            for g in range(1, SGU_GROUPS):
                vm = jnp.where(lane_group == g, mixed[g * SGU_CHUNK:(g + 1) * SGU_CHUNK], vm)
            yc_ref[row0 + ch * SGU_CHUNK:row0 + (ch + 1) * SGU_CHUNK, :] = (
                u[sl] * (vm + bias)).astype(BF16)

        blocks = lambda z, f: jnp.concatenate(
            [f(z[:, c * LANES:(c + 1) * LANES]) for c in range(HEAD_PAIRS)], axis=1).astype(BF16)
        rope = lambda zc: _rope_cols(zc, cos, sin, low_half)
        for j, val in enumerate((blocks(zq, lambda zc: rope(zc) * Q_SCALE), blocks(zk, rope),
                                 zv.astype(BF16))):
            cols = slice(j * ATTN_WIDTH, (j + 1) * ATTN_WIDTH)
            qkv_ref[rs, cols] = val
            folded = jnp.dot(fold_perm, val, preferred_element_type=F32).astype(BF16)
            per_res = SUB_ROWS // FOLD
            for r in range(FOLD):
                qkvf_ref[r, row0 // FOLD:row0 // FOLD + per_res, cols] = (
                    folded[r * per_res:(r + 1) * per_res])


def _inproj(x, mod, g_mix, cos_t, sin_t, w_in, sgu_w, sgu_b):
    B, S, _ = x.shape
    tm = INPROJ_TILE
    assert S % tm == 0 and tm % SUB_ROWS == 0 and SUB_ROWS % (FOLD * 16) == 0
    grid = (B, S // tm)
    tok = lambda width: pl.BlockSpec((None, tm, width), lambda b, i: (b, i, 0))
    qkv_width = 3 * ATTN_WIDTH
    folded = pl.BlockSpec((None, FOLD, tm // FOLD, qkv_width), lambda b, i: (b, 0, i, 0))
    nat_shape = jax.ShapeDtypeStruct((B, S, qkv_width), BF16)
    fold_shape = jax.ShapeDtypeStruct((B, FOLD, S // FOLD, qkv_width), BF16)
    return pl.pallas_call(
        _inproj_kernel,
        grid=grid,
        in_specs=[
            tok(D_MODEL),
            pl.BlockSpec((None, N_MOD, D_MODEL), lambda b, i: (b, 0, 0)),
            _const_spec((1, D_MODEL)),
            pl.BlockSpec((tm, LANES), lambda b, i: (i, 0)),
            pl.BlockSpec((tm, LANES), lambda b, i: (i, 0)),
            _const_spec((D_MODEL, PROJ_WIDTH)),
            _const_spec((SGU_GROUPS * SGU_CHUNK, SGU_CHUNK)),
            _const_spec((SGU_CHUNK, SGU_WIDTH)),
        ],
        out_specs=[tok(POOL_WIDTH), tok(qkv_width), folded, tok(SGU_WIDTH)],
        out_shape=[jax.ShapeDtypeStruct((B, S, POOL_WIDTH), F32),
                   nat_shape, fold_shape,
                   jax.ShapeDtypeStruct((B, S, SGU_WIDTH), BF16)],
        compiler_params=pltpu.CompilerParams(
            dimension_semantics=("parallel", "parallel"), vmem_limit_bytes=VMEM_LIMIT),
        name="inproj",
    )(x, mod, g_mix, cos_t, sin_t, w_in, sgu_w, sgu_b)


def _attention_tile(q_ref, k_refs, v_refs, state_in, outs, s_scr, order_scr, *,
                    pieces, first, last, seq_len, tile, raw_out=False, raw_in=False):
    T = pieces
    kp_ref, kc_ref, kn_ref = k_refs
    vp_ref, vc_ref, vn_ref = v_refs
    if not first and not raw_in:
        o_in, lse_in = state_in
    if last:
        (y_out,) = outs
        o_nat, lse_nat, o_mid, lse_mid = order_scr
    elif not raw_out:
        o_out, lse_out = outs
    tl = q_ref.shape[1]
    halo = BAND_HALF // T
    pq = Q_BLOCK // T
    pk = K_BLOCK // T

    def prepare():
        if not last:
            return
        per_res = tl // FOLD
        for c in range(HEAD_PAIRS):
            cols = slice(c * LANES, (c + 1) * LANES)
            for src, mid, dst in ((o_in, o_mid, o_nat), (lse_in, lse_mid, lse_nat)):
                for r1 in range(4):
                    for r2 in range(4):
                        mid[c, r1, pl.ds(r2, per_res, stride=4), :] = (
                            src[4 * r2 + r1, :, cols].astype(F32))
                for r1 in range(4):
                    dst[c, pl.ds(r1, 4 * per_res, stride=4), :] = mid[c, r1]

    row = lax.broadcasted_iota(jnp.int32, (Q_BLOCK, K_BLOCK), 0)
    col = lax.broadcasted_iota(jnp.int32, (Q_BLOCK, K_BLOCK), 1)
    rel = T * (col % pk - halo - row % pq) + (col // pk - row // pq)
    band = jnp.abs(rel) <= BAND_HALF
    lane = lax.broadcasted_iota(jnp.int32, (1, LANES), 1)
    first_head = lane < HEAD_DIM
    zero = jnp.zeros((), BF16)
    ones_block = jnp.ones((K_BLOCK, LANES), BF16)

    def gather(ref, start, size, cols):
        parts = [ref[t, start:start + size, cols] for t in range(T)]
        return parts[0] if T == 1 else jnp.concatenate(parts, axis=0)

    def window(prev_ref, cur_ref, next_ref, la, cols):
        lo, hi = la - halo, la + pq + halo
        parts = []
        for t in range(T):
            if lo < 0:
                parts.append(prev_ref[t, :, cols])
            parts.append(cur_ref[t, max(lo, 0):min(hi, tl), cols])
            if hi > tl:
                parts.append(next_ref[t, :, cols])
        return parts[0] if len(parts) == 1 else jnp.concatenate(parts, axis=0)

    def block_bias(a):
        kidx = tile * tl + a * pq - halo + col % pk
        valid = band & (kidx >= 0) & (kidx < seq_len)
        return jnp.where(valid, 0.0, MASK_VALUE)

    def scores(a, c, bias):
        la = a * pq
        cols = slice(c * LANES, (c + 1) * LANES)
        qp = gather(q_ref, la, pq, cols)
        kp = window(kp_ref, kc_ref, kn_ref, la, cols)
        q2 = jnp.concatenate([jnp.where(first_head, qp, zero),
                              jnp.where(first_head, zero, qp)], axis=0)
        s = lax.dot_general(q2, kp, (((1,), (1,)), ((), ())), preferred_element_type=F32)
        s = s + jnp.concatenate([bias, bias], axis=0)
        s_scr[a % 2, c] = s
        return jnp.max(s, axis=-1, keepdims=True)

    def values(a, c, row_max):
        la = a * pq
        cols = slice(c * LANES, (c + 1) * LANES)
        vp = window(vp_ref, vc_ref, vn_ref, la, cols)
        p = jnp.exp2(s_scr[a % 2, c] - row_max).astype(BF16)
        res = jnp.dot(p, jnp.concatenate([vp, ones_block], axis=1),
                      preferred_element_type=F32)
        pv = jnp.where(first_head, res[:Q_BLOCK, :LANES], res[Q_BLOCK:, :LANES])
        den = jnp.where(first_head, res[:Q_BLOCK, LANES:], res[Q_BLOCK:, LANES:])
        m = jnp.where(first_head, row_max[:Q_BLOCK], row_max[Q_BLOCK:])
        if raw_out:
            pv_out, den_out, max_out = outs
            for t in range(T):
                piece = slice(t * pq, (t + 1) * pq)
                pv_out[t, la:la + pq, cols] = pv[piece]
                den_out[t, la:la + pq, cols] = den[piece]
                max_out[t, la:la + pq, cols] = m[piece]
            return
        if first:
            o_new = pv / den
            lse_new = m + jnp.log2(den)
        else:
            if raw_in:
                pv_prev, den_prev, max_prev = (gather(r, la, pq, cols) for r in state_in)
                top = jnp.maximum(max_prev, m)
                w_prev = jnp.exp2(max_prev - top)
                w_cur = jnp.exp2(m - top)
                total = w_prev * den_prev + w_cur * den
                o_new = (w_prev * pv_prev + w_cur * pv) / total
            else:
                if last:
                    o_prev = o_nat[c, la:la + pq, :]
                    lse_prev = lse_nat[c, la:la + pq, :]
                else:
                    o_prev = gather(o_in, la, pq, cols).astype(F32)
                    lse_prev = gather(lse_in, la, pq, cols)
                top = jnp.maximum(lse_prev, m)
                w_prev = jnp.exp2(lse_prev - top)
                w_cur = jnp.exp2(m - top)
                total = w_prev + w_cur * den
                o_new = (w_prev * o_prev + w_cur * pv) / total
            if not last:
                lse_new = top + jnp.log2(total)
        if last:
            y_out[la:la + pq, cols] = o_new.astype(BF16)
        else:
            for t in range(T):
                o_out[t, la:la + pq, cols] = o_new[t * pq:(t + 1) * pq].astype(BF16)
                lse_out[t, la:la + pq, cols] = lse_new[t * pq:(t + 1) * pq]

    return prepare, block_bias, scores, values, tl // pq


class _PieceView:
    def __init__(self, ref, t):
        self.ref, self.t = ref, t
        self.shape = (1,) + tuple(ref.shape[1:])

    def __getitem__(self, idx):
        _, rows, cols = idx
        return self.ref[self.t, rows, cols]

    def __setitem__(self, idx, value):
        _, rows, cols = idx
        self.ref[self.t, rows, cols] = value


class _RowWindow:
    def __init__(self, ref, start, size):
        self.ref, self.start, self.size = ref, start, size
        self.shape = (ref.shape[0], size, ref.shape[2])

    def __getitem__(self, idx):
        t, rows, cols = idx
        lo = self.start + (rows.start or 0)
        hi = self.start + (self.size if rows.stop is None else rows.stop)
        return self.ref[t, lo:hi, cols]


def _folded_kernel(q_ref, kp_ref, kc_ref, kn_ref, vp_ref, vc_ref, vn_ref, o_out, lse_out,
                   pv_mid, den_mid, max_mid, s_scr, *, seq_len):
    T = FOLD // MID_DILATION
    tile = pl.program_id(2)
    tl = q_ref.shape[1]
    halo4 = BAND_HALF // T
    wide = []
    for t in range(T):
        piece = lambda ref: _PieceView(ref, t)
        wide.append(_attention_tile(
            piece(q_ref), tuple(map(piece, (kp_ref, kc_ref, kn_ref))),
            tuple(map(piece, (vp_ref, vc_ref, vn_ref))), (),
            (piece(pv_mid), piece(den_mid), piece(max_mid)),
            s_scr.at[pl.ds(2 * (t % 2), 2)], (), pieces=1, first=True, last=False, seq_len=seq_len,
            tile=tile, raw_out=True))
    inner = lambda prev_ref, nxt_ref: (
        _RowWindow(prev_ref, BAND_HALF - halo4, halo4), None, _RowWindow(nxt_ref, 0, halo4))
    k4 = inner(kp_ref, kn_ref)
    v4 = inner(vp_ref, vn_ref)
    _, bias4, scores4, values4, n4 = _attention_tile(
        q_ref, (k4[0], kc_ref, k4[2]), (v4[0], vc_ref, v4[2]), (pv_mid, den_mid, max_mid),
        (o_out, lse_out), s_scr.at[pl.ds(4, 2)], (), pieces=T, first=False, last=False,
        seq_len=seq_len, tile=tile, raw_in=True)

    def block_steps(bias_fn, scores_fn, values_fn, a):
        maxima = {}

        def score_step():
            bias = bias_fn(a)
            for c in range(HEAD_PAIRS):
                maxima[c] = scores_fn(a, c, bias)

        def value_step():
            for c in range(HEAD_PAIRS):
                values_fn(a, c, maxima[c])

        return [score_step, value_step]

    n16 = tl // Q_BLOCK
    per_group = n4 // n16

    def wide_group(j):
        pairs = []
        for t in range(T):
            _, bias16, scores16, values16, _ = wide[t]
            pairs.append(block_steps(bias16, scores16, values16, j))
        steps = [pairs[0][0]]
        for t in range(T):
            if t + 1 < T:
                steps.append(pairs[t + 1][0])
            steps.append(pairs[t][1])
        return steps

    def narrow_group(j):
        steps = []
        for a in range(j * per_group, (j + 1) * per_group):
            steps += block_steps(bias4, scores4, values4, a)
        return steps

    for j in range(n16):
        for step in wide_group(j) + narrow_group(j):
            step()


def _folded_attention(folded):
    B, _, L, W3 = folded.shape
    W = W3 // 3
    T = FOLD // MID_DILATION
    tl = min(ATTN_TILE // T, L)
    assert L % tl == 0 and tl % Q_BLOCK == 0
    halo_per_tile = tl // BAND_HALF
    n_halo = L // BAND_HALF
    qkv = folded.reshape(B, T, MID_DILATION, L, W3)
    cur = lambda j=0: pl.BlockSpec((None, T, None, tl, W), lambda b, r, i: (b, 0, r, i, j))
    prev = lambda j: pl.BlockSpec(
        (None, T, None, BAND_HALF, W),
        lambda b, r, i: (b, 0, r, jnp.maximum(i * halo_per_tile - 1, 0), j))
    nxt = lambda j: pl.BlockSpec(
        (None, T, None, BAND_HALF, W),
        lambda b, r, i: (b, 0, r, jnp.minimum((i + 1) * halo_per_tile, n_halo - 1), j))
    o, lse = pl.pallas_call(
        functools.partial(_folded_kernel, seq_len=L),
        grid=(B, MID_DILATION, L // tl),
        in_specs=[cur(0), prev(1), cur(1), nxt(1), prev(2), cur(2), nxt(2)],
        out_specs=[cur(), cur()],
        out_shape=[jax.ShapeDtypeStruct((B, T, MID_DILATION, L, W), BF16),
                   jax.ShapeDtypeStruct((B, T, MID_DILATION, L, W), F32)],
        scratch_shapes=[pltpu.VMEM((T, tl, W), F32)] * 3 + [
            pltpu.VMEM((6, HEAD_PAIRS, 2 * Q_BLOCK, K_BLOCK), F32)],
        compiler_params=pltpu.CompilerParams(
            dimension_semantics=("parallel", "parallel", "parallel"),
            vmem_limit_bytes=VMEM_LIMIT),
        name="folded_attn",
    )(*([qkv] * 7))
    return o.reshape(B, FOLD, L, W), lse.reshape(B, FOLD, L, W)


def _pool_mixer(za_ref, zprev_ref, znext_ref, tile, n_tiles, seq_len):
    rows = za_ref.shape[0]
    za = za_ref[...]
    before = jnp.where(tile > 0, zprev_ref[...], 0.0)
    after = jnp.where(tile < n_tiles - 1, znext_ref[...], 0.0)
    ext = jnp.concatenate([before, za, after], axis=0)
    n_ext = rows + 2 * POOL_HALO
    back = lambda a, s: pltpu.roll(a, s, axis=0)
    fwd = lambda a, s: pltpu.roll(a, n_ext - s, axis=0)
    sums = [ext + back(ext, 1)]
    for step in (1, 2, 4):
        sums.append(back(sums[-1], step) + fwd(sums[-1], step))
    lane_group = lax.broadcasted_iota(jnp.int32, (1, POOL_WIDTH), 1) // POOL_GROUP_DIM
    win_sum = sums[0][POOL_HALO:POOL_HALO + rows]
    half = jnp.full((1, POOL_WIDTH), POOL_WINDOWS[0] // 2, jnp.int32)
    for g in range(1, len(POOL_WINDOWS)):
        win_sum = jnp.where(lane_group == g, sums[g][POOL_HALO:POOL_HALO + rows], win_sum)
        half = jnp.where(lane_group == g, POOL_WINDOWS[g] // 2, half)

    def inv_count(row0):
        pos = tile * rows + row0 + lax.broadcasted_iota(jnp.int32, (POOL_HALO, 1), 0)
        cnt = jnp.minimum(pos + half, seq_len) - jnp.maximum(pos - half, 0)
        return 1.0 / cnt.astype(F32)

    edge = POOL_HALO
    mean = jnp.concatenate([
        win_sum[:edge] * inv_count(0),
        win_sum[edge:rows - edge] * (0.5 / half.astype(F32)),
        win_sum[rows - edge:] * inv_count(rows - edge)], axis=0)
    return mean - za


def _tail_kernel(q_ref, kp_ref, kc_ref, kn_ref, vp_ref, vc_ref, vn_ref, o_in, lse_in,
                 x_ref, mod_ref, za_ref, zprev_ref, znext_ref, yc_ref,
                 poolw_ref, pools_ref, wout_ref, gmlp_ref, wup_ref, wdown_ref, gfin_ref,
                 o_ref, yb_scr, s_scr, o_nat, lse_nat, o_mid, lse_mid, *, seq_len, final):
    tile = pl.program_id(1)
    n_tiles = pl.num_programs(1)
    prepare, block_bias, scores, values, n_blocks = _attention_tile(
        q_ref, (kp_ref, kc_ref, kn_ref), (vp_ref, vc_ref, vn_ref), (o_in, lse_in), (yb_scr,),
        s_scr, (o_nat, lse_nat, o_mid, lse_mid),
        pieces=1, first=False, last=True, seq_len=seq_len, tile=tile)
    n_sub = x_ref.shape[0] // SUB_ROWS
    blocks_per_sub = n_blocks // n_sub

    def attention_steps(st):
        steps = []
        for a in range(st * blocks_per_sub, (st + 1) * blocks_per_sub):
            maxima = {}

            def score_step(a=a, maxima=maxima):
                bias = block_bias(a)
                for c in range(HEAD_PAIRS):
                    maxima[c] = scores(a, c, bias)

            def value_step(a=a, maxima=maxima):
                for c in range(HEAD_PAIRS):
                    values(a, c, maxima[c])

            steps += [score_step, value_step]
        return steps

    o1 = POOL_WIDTH
    o2 = o1 + ATTN_WIDTH
    gain = gmlp_ref[...] * (1.0 + mod_ref[4:5, :])
    n_chunks = D_FF // FF_CHUNK
    chunk = lambda c: slice(c * FF_CHUNK, (c + 1) * FF_CHUNK)

    prepare()
    for step in attention_steps(0):
        step()
    p = _pool_mixer(za_ref, zprev_ref, znext_ref, tile, n_tiles, seq_len).astype(BF16)

    def mixer_residual(st):
        rs = slice(st * SUB_ROWS, (st + 1) * SUB_ROWS)
        y = jnp.dot(yb_scr[rs, :], wout_ref[o1:o2, :], preferred_element_type=F32)
        y = y + jnp.dot(yc_ref[rs, :], wout_ref[o2:, :], preferred_element_type=F32)
        ya = jnp.dot(p[rs], poolw_ref[...], preferred_element_type=F32) * pools_ref[...]
        y = y + jnp.dot(ya.astype(BF16), wout_ref[0:o1, :], preferred_element_type=F32)
        return x_ref[rs, :] + mod_ref[2:3, :] * y

    x1 = mixer_residual(0)
    for st in range(n_sub):
        rs = slice(st * SUB_ROWS, (st + 1) * SUB_ROWS)
        pending = iter(attention_steps(st + 1) if st + 1 < n_sub else [])
        run_next = lambda: next(pending, lambda: None)()
        run_next()

        xn = x1 * lax.rsqrt(jnp.mean(x1 * x1, axis=-1, keepdims=True) + EPS)
        h = (xn * gain + mod_ref[3:4, :]).astype(BF16)
        acc = None
        x1_next = None
        for c in range(n_chunks):
            up = jnp.dot(h, wup_ref[:, chunk(c)], preferred_element_type=F32)
            run_next()
            if c == n_chunks - 1:
                for step in pending:
                    step()
                if st + 1 < n_sub:
                    x1_next = mixer_residual(st + 1)
            act = jnp.square(jnp.maximum(up, 0.0)).astype(BF16)
            part = jnp.dot(act, wdown_ref[chunk(c), :], preferred_element_type=F32)
            acc = part if acc is None else acc + part
        x2 = x1 + mod_ref[5:6, :] * acc
        if final:
            x2 = x2 * lax.rsqrt(jnp.mean(x2 * x2, axis=-1, keepdims=True) + EPS) * gfin_ref[...]
        o_ref[rs, :] = x2
        x1 = x1_next


def _tail(qkv, state, x, mod, za, yc, pool_w, pool_scale, w_out, g_mlp, w_up, w_down, g_final,
          final):
    B, S, _ = x.shape
    W = ATTN_WIDTH
    tm = ROW_TILE
    assert S % tm == 0 and tm % SUB_ROWS == 0 and SUB_ROWS % Q_BLOCK == 0
    n_tiles = S // tm
    pool_per_tile = tm // POOL_HALO
    n_pool = S // POOL_HALO
    band_per_tile = tm // BAND_HALF
    n_band = S // BAND_HALF
    tok = lambda width: pl.BlockSpec((None, tm, width), lambda b, i: (b, i, 0))
    cur = lambda j: pl.BlockSpec((None, 1, tm, W), lambda b, i: (b, 0, i, j))
    prev = lambda j: pl.BlockSpec(
        (None, 1, BAND_HALF, W), lambda b, i: (b, 0, jnp.maximum(i * band_per_tile - 1, 0), j))
    nxt = lambda j: pl.BlockSpec(
        (None, 1, BAND_HALF, W),
        lambda b, i: (b, 0, jnp.minimum((i + 1) * band_per_tile, n_band - 1), j))
    st = lambda: pl.BlockSpec((None, FOLD, tm // FOLD, W), lambda b, i: (b, 0, i, 0))
    qkv4 = qkv.reshape(B, 1, S, 3 * W)
    return pl.pallas_call(
        functools.partial(_tail_kernel, seq_len=S, final=final),
        grid=(B, n_tiles),
        in_specs=[
            cur(0), prev(1), cur(1), nxt(1), prev(2), cur(2), nxt(2), st(), st(),
            tok(D_MODEL),
            pl.BlockSpec((None, N_MOD, D_MODEL), lambda b, i: (b, 0, 0)),
            tok(POOL_WIDTH),
            pl.BlockSpec((None, POOL_HALO, POOL_WIDTH),
                         lambda b, i: (b, jnp.maximum(i * pool_per_tile - 1, 0), 0)),
            pl.BlockSpec((None, POOL_HALO, POOL_WIDTH),
                         lambda b, i: (b, jnp.minimum((i + 1) * pool_per_tile, n_pool - 1), 0)),
            tok(SGU_WIDTH),
            _const_spec((POOL_WIDTH, POOL_WIDTH)),
            _const_spec((1, POOL_WIDTH)),
            _const_spec((D_MODEL, D_MODEL)),
            _const_spec((1, D_MODEL)),
            _const_spec((D_MODEL, D_FF)),
            _const_spec((D_FF, D_MODEL)),
            _const_spec((1, D_MODEL)),
        ],
        out_specs=tok(D_MODEL),
        out_shape=jax.ShapeDtypeStruct((B, S, D_MODEL), F32),
        scratch_shapes=[
            pltpu.VMEM((tm, W), BF16),
            pltpu.VMEM((2, HEAD_PAIRS, 2 * Q_BLOCK, K_BLOCK), F32),
            pltpu.VMEM((HEAD_PAIRS, tm, LANES), F32), pltpu.VMEM((HEAD_PAIRS, tm, LANES), F32),
            pltpu.VMEM((HEAD_PAIRS, 4, tm // 4, LANES), F32),
            pltpu.VMEM((HEAD_PAIRS, 4, tm // 4, LANES), F32)],
        compiler_params=pltpu.CompilerParams(
            dimension_semantics=("parallel", "parallel"), vmem_limit_bytes=VMEM_LIMIT),
        name="tail",
    )(qkv4, qkv4, qkv4, qkv4, qkv4, qkv4, qkv4, state[0], state[1],
      x, mod, za, za, za, yc, pool_w, pool_scale, w_out, g_mlp, w_up, w_down, g_final)


def _rope_tables(seq_len):
    inv_freq = ROPE_THETA ** (-jnp.arange(0, ROT_DIM, 2, dtype=F32) / ROT_DIM)
    ang = jnp.arange(seq_len).astype(F32)[:, None] * inv_freq[None, :]
    cos, sin = jnp.cos(ang), jnp.sin(ang)
    pad = HEAD_DIM - ROT_DIM
    cos_head = jnp.concatenate([cos, cos, jnp.ones((seq_len, pad), F32)], axis=1)
    sin_head = jnp.concatenate([-sin, sin, jnp.zeros((seq_len, pad), F32)], axis=1)
    reps = LANES // HEAD_DIM
    return jnp.tile(cos_head, (1, reps)), jnp.tile(sin_head, (1, reps))


def _block_diag(blocks):
    g, n, m = blocks.shape
    eye = jnp.eye(g, dtype=blocks.dtype)
    return (eye[:, None, :, None] * blocks[:, :, None, :]).reshape(g * n, g * m)


def _prepare_layer(l, g_mix, g_mlp, w_in, pool_w, pool_scale, sgu_w, sgu_b, w_out, w_up, w_down):
    return dict(
        g_mix=g_mix[l].reshape(1, D_MODEL),
        g_mlp=g_mlp[l].reshape(1, D_MODEL),
        w_in=w_in[l].astype(BF16),
        pool_w=_block_diag(pool_w[l]).astype(BF16),
        pool_scale=pool_scale[l].reshape(1, POOL_WIDTH),
        sgu_w=sgu_w[l].reshape(SGU_GROUPS * SGU_CHUNK, SGU_CHUNK).astype(BF16),
        sgu_b=jnp.repeat(jnp.transpose(sgu_b[l]), SGU_GROUP_DIM, axis=1),
        w_out=w_out[l].astype(BF16),
        w_up=w_up[l].astype(BF16),
        w_down=w_down[l].astype(BF16),
    )


def _trunk(x, mods, layers, g_final):
    B, S, _ = x.shape
    cos_t, sin_t = _rope_tables(S)
    for l, lw in enumerate(layers):
        mod = mods[l].reshape(B, N_MOD, D_MODEL)
        za, qkv, qkv_folded, yc = _inproj(x, mod, lw["g_mix"], cos_t, sin_t, lw["w_in"],
                                          lw["sgu_w"], lw["sgu_b"])
        state = _folded_attention(qkv_folded)
        x = _tail(qkv, state, x, mod, za, yc, lw["pool_w"], lw["pool_scale"], lw["w_out"],
                  lw["g_mlp"], lw["w_up"], lw["w_down"], g_final, final=(l == DEPTH - 1))
    return x


def kernel(x_prompt, x_sample, c_prompt, c_sample, w_ada, b_ada, g_mix, g_mlp, w_in, pool_w,
           pool_scale, sgu_w, sgu_b, w_out, w_up, w_down, g_final):
    nb_p = c_prompt.shape[0]
    nb_s = c_sample.shape[0]
    rows = -(-(nb_p + nb_s) // 8) * 8
    c_all = jnp.concatenate(
        [c_prompt, c_sample, jnp.zeros((rows - nb_p - nb_s, D_MODEL), F32)], axis=0)
    mods = _ada_mod(c_all, w_ada, b_ada)
    layers = [_prepare_layer(l, g_mix, g_mlp, w_in, pool_w, pool_scale, sgu_w, sgu_b,
                             w_out, w_up, w_down) for l in range(DEPTH)]
    gf = g_final.reshape(1, D_MODEL)
    y_prompt = _trunk(x_prompt, mods[:, :nb_p], layers, gf)
    y_sample = _trunk(x_sample, mods[:, nb_p:nb_p + nb_s], layers, gf)
    return (y_prompt, y_sample)
```

```python
import functools

import jax
import jax.numpy as jnp
from jax import lax
from jax.experimental import pallas as pl
from jax.experimental.pallas import tpu as pltpu

F32 = jnp.float32
BF16 = jnp.bfloat16

D_MODEL = 1024
DEPTH = 2
HEAD_DIM = 64
POOL_WINDOWS = (2, 4, 8, 16)
POOL_WIDTH = D_MODEL // 4
POOL_GROUP_DIM = POOL_WIDTH // len(POOL_WINDOWS)
ATTN_WIDTH = D_MODEL // 2
BAND_HALF = 64
FOLD = 16
MID_DILATION = 4
ROT_DIM = HEAD_DIM // 4
ROPE_THETA = 500000.0
SGU_WIDTH = D_MODEL // 4
SGU_GROUPS = 4
SGU_GROUP_DIM = SGU_WIDTH // SGU_GROUPS
SGU_CHUNK = 128
D_FF = 4 * D_MODEL
N_MOD = 6
EPS = 1e-6
MASK_VALUE = -1e30
Q_SCALE = HEAD_DIM ** -0.5 * 1.4426950408889634

OFF_Q = POOL_WIDTH
OFF_K = OFF_Q + ATTN_WIDTH
OFF_V = OFF_K + ATTN_WIDTH
OFF_C = OFF_V + ATTN_WIDTH
PROJ_WIDTH = OFF_C + 2 * SGU_WIDTH

LANES = 128
HEAD_PAIRS = ATTN_WIDTH // LANES
POOL_HALO = 8
ROW_TILE = 512
INPROJ_TILE = 1024
ATTN_TILE = 2048
Q_BLOCK = 128
K_BLOCK = Q_BLOCK + 2 * BAND_HALF
FF_CHUNK = 1024
SUB_ROWS = 256
VMEM_LIMIT = 56 * 1024 * 1024


def _const_spec(shape):
    nd = len(shape)
    return pl.BlockSpec(shape, lambda *_: (0,) * nd, pipeline_mode=pl.Buffered(1))


def _ada_kernel(c_ref, w_ref, b_ref, o_ref):
    c = c_ref[...]
    act = c * (1.0 / (1.0 + jnp.exp(-c)))
    o_ref[...] = jnp.dot(act.astype(BF16), w_ref[...].astype(BF16),
                         preferred_element_type=F32) + b_ref[...]


def _ada_mod(c_all, w_ada, b_ada):
    rows = c_all.shape[0]
    return pl.pallas_call(
        _ada_kernel,
        grid=(DEPTH, N_MOD),
        in_specs=[
            pl.BlockSpec((rows, D_MODEL), lambda l, j: (0, 0)),
            pl.BlockSpec((None, D_MODEL, D_MODEL), lambda l, j: (l, 0, j)),
            pl.BlockSpec((None, 1, D_MODEL), lambda l, j: (l, 0, j)),
        ],
        out_specs=pl.BlockSpec((None, rows, D_MODEL), lambda l, j: (l, 0, j)),
        out_shape=jax.ShapeDtypeStruct((DEPTH, rows, N_MOD * D_MODEL), F32),
        compiler_params=pltpu.CompilerParams(vmem_limit_bytes=VMEM_LIMIT),
        name="ada_mod",
    )(c_all, w_ada, b_ada.reshape(DEPTH, 1, N_MOD * D_MODEL))


def _rope_cols(z, cos, sin, low_half):
    up = pltpu.roll(z, LANES - ROT_DIM // 2, axis=1)
    down = pltpu.roll(z, ROT_DIM // 2, axis=1)
    return z * cos + jnp.where(low_half, up, down) * sin


def _group_sums(v, ones_bd):
    hi = v.astype(BF16)
    lo = (v - hi.astype(F32)).astype(BF16)
    return (jnp.dot(hi, ones_bd, preferred_element_type=F32)
            + jnp.dot(lo, ones_bd, preferred_element_type=F32))


def _inproj_kernel(x_ref, mod_ref, g_ref, cos_ref, sin_ref, w_ref, sguw_ref, sgub_ref,
                   za_ref, qkv_ref, qkvf_ref, yc_ref):
    rows = x_ref.shape[0]
    lane = lax.broadcasted_iota(jnp.int32, (1, LANES), 1)
    low_half = (lane % HEAD_DIM) < (ROT_DIM // 2)
    gi = lax.broadcasted_iota(jnp.int32, (SGU_WIDTH, SGU_WIDTH), 0) // SGU_GROUP_DIM
    gj = lax.broadcasted_iota(jnp.int32, (SGU_WIDTH, SGU_WIDTH), 1) // SGU_GROUP_DIM
    ones_bd = jnp.where(gi == gj, 1.0, 0.0).astype(BF16)
    lane_group = lax.broadcasted_iota(jnp.int32, (1, SGU_WIDTH), 1) // SGU_GROUP_DIM
    bias = sgub_ref[...]
    gain = g_ref[...] * (1.0 + mod_ref[1:2, :])
    shift = mod_ref[0:1, :]
    out_row = lax.broadcasted_iota(jnp.int32, (SUB_ROWS, SUB_ROWS), 0)
    in_row = lax.broadcasted_iota(jnp.int32, (SUB_ROWS, SUB_ROWS), 1)
    per = SUB_ROWS // FOLD
    fold_perm = jnp.where(in_row == (out_row % per) * FOLD + out_row // per, 1.0, 0.0).astype(BF16)

    for st in range(rows // SUB_ROWS):
        row0 = st * SUB_ROWS
        rs = slice(row0, row0 + SUB_ROWS)
        x = x_ref[rs, :]
        xn = x * lax.rsqrt(jnp.mean(x * x, axis=-1, keepdims=True) + EPS)
        h = (xn * gain + shift).astype(BF16)

        cos = cos_ref[rs, :]
        sin = sin_ref[rs, :]
        gate = jax.nn.gelu(jnp.dot(h, w_ref[:, OFF_C:PROJ_WIDTH], preferred_element_type=F32))
        zq = jnp.dot(h, w_ref[:, OFF_Q:OFF_K], preferred_element_type=F32)
        u = gate[:, :SGU_WIDTH]
        v = gate[:, SGU_WIDTH:]
        dv = v - _group_sums(v, ones_bd) * (1.0 / SGU_GROUP_DIM)
        zk = jnp.dot(h, w_ref[:, OFF_K:OFF_V], preferred_element_type=F32)
        var = _group_sums(dv * dv, ones_bd) * (1.0 / SGU_GROUP_DIM)
        zv = jnp.dot(h, w_ref[:, OFF_V:OFF_C], preferred_element_type=F32)
        vn = (dv * lax.rsqrt(var + EPS)).astype(BF16)
        za_ref[rs, :] = jnp.dot(h, w_ref[:, 0:OFF_Q], preferred_element_type=F32)

        for ch in range(SUB_ROWS // SGU_CHUNK):
            sl = slice(ch * SGU_CHUNK, (ch + 1) * SGU_CHUNK)
            mixed = jnp.dot(sguw_ref[...], vn[sl], preferred_element_type=F32)
            vm = mixed[0:SGU_CHUNK]
            for g in range(1, SGU_GROUPS):
                vm = jnp.where(lane_group == g, mixed[g * SGU_CHUNK:(g + 1) * SGU_CHUNK], vm)
            yc_ref[row0 + ch * SGU_CHUNK:row0 + (ch + 1) * SGU_CHUNK, :] = (
                u[sl] * (vm + bias)).astype(BF16)

        blocks = lambda z, f: jnp.concatenate(
            [f(z[:, c * LANES:(c + 1) * LANES]) for c in range(HEAD_PAIRS)], axis=1).astype(BF16)
        rope = lambda zc: _rope_cols(zc, cos, sin, low_half)
        for j, val in enumerate((blocks(zq, lambda zc: rope(zc) * Q_SCALE), blocks(zk, rope),
                                 zv.astype(BF16))):
            cols = slice(j * ATTN_WIDTH, (j + 1) * ATTN_WIDTH)
            qkv_ref[rs, cols] = val
            folded = jnp.dot(fold_perm, val, preferred_element_type=F32).astype(BF16)
            per_res = SUB_ROWS // FOLD
            for r in range(FOLD):
                qkvf_ref[r, row0 // FOLD:row0 // FOLD + per_res, cols] = (
                    folded[r * per_res:(r + 1) * per_res])


def _inproj(x, mod, g_mix, cos_t, sin_t, w_in, sgu_w, sgu_b):
    B, S, _ = x.shape
    tm = INPROJ_TILE
    assert S % tm == 0 and tm % SUB_ROWS == 0 and SUB_ROWS % (FOLD * 16) == 0
    grid = (B, S // tm)
    tok = lambda width: pl.BlockSpec((None, tm, width), lambda b, i: (b, i, 0))
    qkv_width = 3 * ATTN_WIDTH
    folded = pl.BlockSpec((None, FOLD, tm // FOLD, qkv_width), lambda b, i: (b, 0, i, 0))
    nat_shape = jax.ShapeDtypeStruct((B, S, qkv_width), BF16)
    fold_shape = jax.ShapeDtypeStruct((B, FOLD, S // FOLD, qkv_width), BF16)
    return pl.pallas_call(
        _inproj_kernel,
        grid=grid,
        in_specs=[
            tok(D_MODEL),
            pl.BlockSpec((None, N_MOD, D_MODEL), lambda b, i: (b, 0, 0)),
            _const_spec((1, D_MODEL)),
            pl.BlockSpec((tm, LANES), lambda b, i: (i, 0)),
            pl.BlockSpec((tm, LANES), lambda b, i: (i, 0)),
            _const_spec((D_MODEL, PROJ_WIDTH)),
            _const_spec((SGU_GROUPS * SGU_CHUNK, SGU_CHUNK)),
            _const_spec((SGU_CHUNK, SGU_WIDTH)),
        ],
        out_specs=[tok(POOL_WIDTH), tok(qkv_width), folded, tok(SGU_WIDTH)],
        out_shape=[jax.ShapeDtypeStruct((B, S, POOL_WIDTH), F32),
                   nat_shape, fold_shape,
                   jax.ShapeDtypeStruct((B, S, SGU_WIDTH), BF16)],
        compiler_params=pltpu.CompilerParams(
            dimension_semantics=("parallel", "parallel"), vmem_limit_bytes=VMEM_LIMIT),
        name="inproj",
    )(x, mod, g_mix, cos_t, sin_t, w_in, sgu_w, sgu_b)


def _attention_tile(q_ref, k_refs, v_refs, state_in, outs, s_scr, order_scr, *,
                    pieces, first, last, seq_len, tile, raw_out=False, raw_in=False):
    T = pieces
    kp_ref, kc_ref, kn_ref = k_refs
    vp_ref, vc_ref, vn_ref = v_refs
    if not first and not raw_in:
        o_in, lse_in = state_in
    if last:
        (y_out,) = outs
        o_nat, lse_nat, o_mid, lse_mid = order_scr
    elif not raw_out:
        o_out, lse_out = outs
    tl = q_ref.shape[1]
    halo = BAND_HALF // T
    pq = Q_BLOCK // T
    pk = K_BLOCK // T

    def prepare():
        if not last:
            return
        per_res = tl // FOLD
        for c in range(HEAD_PAIRS):
            cols = slice(c * LANES, (c + 1) * LANES)
            for src, mid, dst in ((o_in, o_mid, o_nat), (lse_in, lse_mid, lse_nat)):
                for r1 in range(4):
                    for r2 in range(4):
                        mid[c, r1, pl.ds(r2, per_res, stride=4), :] = (
                            src[4 * r2 + r1, :, cols].astype(F32))
                for r1 in range(4):
                    dst[c, pl.ds(r1, 4 * per_res, stride=4), :] = mid[c, r1]

    row = lax.broadcasted_iota(jnp.int32, (Q_BLOCK, K_BLOCK), 0)
    col = lax.broadcasted_iota(jnp.int32, (Q_BLOCK, K_BLOCK), 1)
    rel = T * (col % pk - halo - row % pq) + (col // pk - row // pq)
    band = jnp.abs(rel) <= BAND_HALF
    lane = lax.broadcasted_iota(jnp.int32, (1, LANES), 1)
    first_head = lane < HEAD_DIM
    zero = jnp.zeros((), BF16)
    ones_block = jnp.ones((K_BLOCK, LANES), BF16)
    ones_first = jnp.broadcast_to(jnp.where(first_head, 1.0, 0.0).astype(BF16), (K_BLOCK, LANES))
    ones_second = jnp.broadcast_to(jnp.where(first_head, 0.0, 1.0).astype(BF16), (K_BLOCK, LANES))

    def gather(ref, start, size, cols):
        parts = [ref[t, start:start + size, cols] for t in range(T)]
        return parts[0] if T == 1 else jnp.concatenate(parts, axis=0)

    def window(prev_ref, cur_ref, next_ref, la, cols):
        lo, hi = la - halo, la + pq + halo
        parts = []
        for t in range(T):
            if lo < 0:
                parts.append(prev_ref[t, :, cols])
            parts.append(cur_ref[t, max(lo, 0):min(hi, tl), cols])
            if hi > tl:
                parts.append(next_ref[t, :, cols])
        return parts[0] if len(parts) == 1 else jnp.concatenate(parts, axis=0)

    def block_bias(a):
        kidx = tile * tl + a * pq - halo + col % pk
        valid = band & (kidx >= 0) & (kidx < seq_len)
        return jnp.where(valid, 0.0, MASK_VALUE)

    def scores(a, c, bias):
        la = a * pq
        cols = slice(c * LANES, (c + 1) * LANES)
        qp = gather(q_ref, la, pq, cols)
        kp = window(kp_ref, kc_ref, kn_ref, la, cols)
        q2 = jnp.concatenate([jnp.where(first_head, qp, zero),
                              jnp.where(first_head, zero, qp)], axis=0)
        s = lax.dot_general(q2, kp, (((1,), (1,)), ((), ())), preferred_element_type=F32)
        s = s + jnp.concatenate([bias, bias], axis=0)
        s_scr[a % 2, c] = s
        return jnp.max(s, axis=-1, keepdims=True)

    def values(a, c, row_max):
        la = a * pq
        cols = slice(c * LANES, (c + 1) * LANES)
        vp = window(vp_ref, vc_ref, vn_ref, la, cols)
        p = jnp.exp2(s_scr[a % 2, c] - row_max).astype(BF16)
        if raw_in:
            rhs = jnp.concatenate([
                jnp.concatenate([jnp.where(first_head, vp, zero), ones_first], axis=1),
                jnp.concatenate([jnp.where(first_head, zero, vp), ones_second], axis=1)], axis=0)
            res = jnp.dot(jnp.concatenate([p[:Q_BLOCK], p[Q_BLOCK:]], axis=1), rhs,
                          preferred_element_type=F32)
            pv = res[:, :LANES]
            den = res[:, LANES:]
        else:
            res = jnp.dot(p, jnp.concatenate([vp, ones_block], axis=1),
                          preferred_element_type=F32)
            pv = jnp.where(first_head, res[:Q_BLOCK, :LANES], res[Q_BLOCK:, :LANES])
            den = jnp.where(first_head, res[:Q_BLOCK, LANES:], res[Q_BLOCK:, LANES:])
        m = jnp.where(first_head, row_max[:Q_BLOCK], row_max[Q_BLOCK:])
        if raw_out:
            pv_out, den_out, max_out = outs
            for t in range(T):
                piece = slice(t * pq, (t + 1) * pq)
                pv_out[t, la:la + pq, cols] = pv[piece]
                den_out[t, la:la + pq, cols] = den[piece]
                max_out[t, la:la + pq, cols] = m[piece]
            return
        if first:
            o_new = pv / den
            lse_new = m + jnp.log2(den)
        else:
            if raw_in:
                pv_prev, den_prev, max_prev = (gather(r, la, pq, cols) for r in state_in)
                top = jnp.maximum(max_prev, m)
                w_prev = jnp.exp2(max_prev - top)
                w_cur = jnp.exp2(m - top)
                total = w_prev * den_prev + w_cur * den
                o_new = (w_prev * pv_prev + w_cur * pv) / total
            else:
                if last:
                    o_prev = o_nat[c, la:la + pq, :]
                    lse_prev = lse_nat[c, la:la + pq, :]
                else:
                    o_prev = gather(o_in, la, pq, cols).astype(F32)
                    lse_prev = gather(lse_in, la, pq, cols)
                top = jnp.maximum(lse_prev, m)
                w_prev = jnp.exp2(lse_prev - top)
                w_cur = jnp.exp2(m - top)
                total = w_prev + w_cur * den
                o_new = (w_prev * o_prev + w_cur * pv) / total
            if not last:
                lse_new = top + jnp.log2(total)
        if last:
            y_out[la:la + pq, cols] = o_new.astype(BF16)
        else:
            for t in range(T):
                o_out[t, la:la + pq, cols] = o_new[t * pq:(t + 1) * pq].astype(BF16)
                lse_out[t, la:la + pq, cols] = lse_new[t * pq:(t + 1) * pq]

    return prepare, block_bias, scores, values, tl // pq


class _PieceView:
    def __init__(self, ref, t):
        self.ref, self.t = ref, t
        self.shape = (1,) + tuple(ref.shape[1:])

    def __getitem__(self, idx):
        _, rows, cols = idx
        return self.ref[self.t, rows, cols]

    def __setitem__(self, idx, value):
        _, rows, cols = idx
        self.ref[self.t, rows, cols] = value


class _RowWindow:
    def __init__(self, ref, start, size):
        self.ref, self.start, self.size = ref, start, size
        self.shape = (ref.shape[0], size, ref.shape[2])

    def __getitem__(self, idx):
        t, rows, cols = idx
        lo = self.start + (rows.start or 0)
        hi = self.start + (self.size if rows.stop is None else rows.stop)
        return self.ref[t, lo:hi, cols]


def _folded_kernel(q_ref, kp_ref, kc_ref, kn_ref, vp_ref, vc_ref, vn_ref, o_out, lse_out,
                   pv_mid, den_mid, max_mid, s_scr, *, seq_len):
    T = FOLD // MID_DILATION
    tile = pl.program_id(2)
    tl = q_ref.shape[1]
    halo4 = BAND_HALF // T
    wide = []
    for t in range(T):
        piece = lambda ref: _PieceView(ref, t)
        wide.append(_attention_tile(
            piece(q_ref), tuple(map(piece, (kp_ref, kc_ref, kn_ref))),
            tuple(map(piece, (vp_ref, vc_ref, vn_ref))), (),
            (piece(pv_mid), piece(den_mid), piece(max_mid)),
            s_scr.at[pl.ds(2 * (t % 2), 2)], (), pieces=1, first=True, last=False, seq_len=seq_len,
            tile=tile, raw_out=True))
    inner = lambda prev_ref, nxt_ref: (
        _RowWindow(prev_ref, BAND_HALF - halo4, halo4), None, _RowWindow(nxt_ref, 0, halo4))
    k4 = inner(kp_ref, kn_ref)
    v4 = inner(vp_ref, vn_ref)
    _, bias4, scores4, values4, n4 = _attention_tile(
        q_ref, (k4[0], kc_ref, k4[2]), (v4[0], vc_ref, v4[2]), (pv_mid, den_mid, max_mid),
        (o_out, lse_out), s_scr.at[pl.ds(4, 2)], (), pieces=T, first=False, last=False,
        seq_len=seq_len, tile=tile, raw_in=True)

    def block_steps(bias_fn, scores_fn, values_fn, a):
        maxima = {}

        def score_step():
            bias = bias_fn(a)
            for c in range(HEAD_PAIRS):
                maxima[c] = scores_fn(a, c, bias)

        def value_step():
            for c in range(HEAD_PAIRS):
                values_fn(a, c, maxima[c])

        return [score_step, value_step]

    n16 = tl // Q_BLOCK
    per_group = n4 // n16

    def wide_group(j):
        pairs = []
        for t in range(T):
            _, bias16, scores16, values16, _ = wide[t]
            pairs.append(block_steps(bias16, scores16, values16, j))
        steps = [pairs[0][0]]
        for t in range(T):
            if t + 1 < T:
                steps.append(pairs[t + 1][0])
            steps.append(pairs[t][1])
        return steps

    def narrow_group(j):
        steps = []
        for a in range(j * per_group, (j + 1) * per_group):
            steps += block_steps(bias4, scores4, values4, a)
        return steps

    for j in range(n16):
        for step in wide_group(j) + narrow_group(j):
            step()


def _folded_attention(folded):
    B, _, L, W3 = folded.shape
    W = W3 // 3
    T = FOLD // MID_DILATION
    tl = min(ATTN_TILE // T, L)
    assert L % tl == 0 and tl % Q_BLOCK == 0
    halo_per_tile = tl // BAND_HALF
    n_halo = L // BAND_HALF
    qkv = folded.reshape(B, T, MID_DILATION, L, W3)
    cur = lambda j=0: pl.BlockSpec((None, T, None, tl, W), lambda b, r, i: (b, 0, r, i, j))
    prev = lambda j: pl.BlockSpec(
        (None, T, None, BAND_HALF, W),
        lambda b, r, i: (b, 0, r, jnp.maximum(i * halo_per_tile - 1, 0), j))
    nxt = lambda j: pl.BlockSpec(
        (None, T, None, BAND_HALF, W),
        lambda b, r, i: (b, 0, r, jnp.minimum((i + 1) * halo_per_tile, n_halo - 1), j))
    o, lse = pl.pallas_call(
        functools.partial(_folded_kernel, seq_len=L),
        grid=(B, MID_DILATION, L // tl),
        in_specs=[cur(0), prev(1), cur(1), nxt(1), prev(2), cur(2), nxt(2)],
        out_specs=[cur(), cur()],
        out_shape=[jax.ShapeDtypeStruct((B, T, MID_DILATION, L, W), BF16),
                   jax.ShapeDtypeStruct((B, T, MID_DILATION, L, W), F32)],
        scratch_shapes=[pltpu.VMEM((T, tl, W), F32)] * 3 + [
            pltpu.VMEM((6, HEAD_PAIRS, 2 * Q_BLOCK, K_BLOCK), F32)],
        compiler_params=pltpu.CompilerParams(
            dimension_semantics=("parallel", "parallel", "parallel"),
            vmem_limit_bytes=VMEM_LIMIT),
        name="folded_attn",
    )(*([qkv] * 7))
    return o.reshape(B, FOLD, L, W), lse.reshape(B, FOLD, L, W)


def _pool_mixer(za_ref, zprev_ref, znext_ref, tile, n_tiles, seq_len):
    rows = za_ref.shape[0]
    za = za_ref[...]
    before = jnp.where(tile > 0, zprev_ref[...], 0.0)
    after = jnp.where(tile < n_tiles - 1, znext_ref[...], 0.0)
    ext = jnp.concatenate([before, za, after], axis=0)
    n_ext = rows + 2 * POOL_HALO
    back = lambda a, s: pltpu.roll(a, s, axis=0)
    fwd = lambda a, s: pltpu.roll(a, n_ext - s, axis=0)
    sums = [ext + back(ext, 1)]
    for step in (1, 2, 4):
        sums.append(back(sums[-1], step) + fwd(sums[-1], step))
    lane_group = lax.broadcasted_iota(jnp.int32, (1, POOL_WIDTH), 1) // POOL_GROUP_DIM
    win_sum = sums[0][POOL_HALO:POOL_HALO + rows]
    half = jnp.full((1, POOL_WIDTH), POOL_WINDOWS[0] // 2, jnp.int32)
    for g in range(1, len(POOL_WINDOWS)):
        win_sum = jnp.where(lane_group == g, sums[g][POOL_HALO:POOL_HALO + rows], win_sum)
        half = jnp.where(lane_group == g, POOL_WINDOWS[g] // 2, half)

    def inv_count(row0):
        pos = tile * rows + row0 + lax.broadcasted_iota(jnp.int32, (POOL_HALO, 1), 0)
        cnt = jnp.minimum(pos + half, seq_len) - jnp.maximum(pos - half, 0)
        return 1.0 / cnt.astype(F32)

    edge = POOL_HALO
    mean = jnp.concatenate([
        win_sum[:edge] * inv_count(0),
        win_sum[edge:rows - edge] * (0.5 / half.astype(F32)),
        win_sum[rows - edge:] * inv_count(rows - edge)], axis=0)
    return mean - za


def _tail_kernel(q_ref, kp_ref, kc_ref, kn_ref, vp_ref, vc_ref, vn_ref, o_in, lse_in,
                 x_ref, mod_ref, za_ref, zprev_ref, znext_ref, yc_ref,
                 poolw_ref, pools_ref, wout_ref, gmlp_ref, wup_ref, wdown_ref, gfin_ref,
                 o_ref, yb_scr, s_scr, o_nat, lse_nat, o_mid, lse_mid, *, seq_len, final):
    tile = pl.program_id(1)
    n_tiles = pl.num_programs(1)
    prepare, block_bias, scores, values, n_blocks = _attention_tile(
        q_ref, (kp_ref, kc_ref, kn_ref), (vp_ref, vc_ref, vn_ref), (o_in, lse_in), (yb_scr,),
        s_scr, (o_nat, lse_nat, o_mid, lse_mid),
        pieces=1, first=False, last=True, seq_len=seq_len, tile=tile)
    n_sub = x_ref.shape[0] // SUB_ROWS
    blocks_per_sub = n_blocks // n_sub

    def attention_steps(st):
        steps = []
        for a in range(st * blocks_per_sub, (st + 1) * blocks_per_sub):
            maxima = {}

            def score_step(a=a, maxima=maxima):
                bias = block_bias(a)
                for c in range(HEAD_PAIRS):
                    maxima[c] = scores(a, c, bias)

            def value_step(a=a, maxima=maxima):
                for c in range(HEAD_PAIRS):
                    values(a, c, maxima[c])

            steps += [score_step, value_step]
        return steps

    o1 = POOL_WIDTH
    o2 = o1 + ATTN_WIDTH
    gain = gmlp_ref[...] * (1.0 + mod_ref[4:5, :])
    n_chunks = D_FF // FF_CHUNK
    chunk = lambda c: slice(c * FF_CHUNK, (c + 1) * FF_CHUNK)

    prepare()
    for step in attention_steps(0):
        step()
    p = _pool_mixer(za_ref, zprev_ref, znext_ref, tile, n_tiles, seq_len).astype(BF16)

    def mixer_residual(st):
        rs = slice(st * SUB_ROWS, (st + 1) * SUB_ROWS)
        y = jnp.dot(yb_scr[rs, :], wout_ref[o1:o2, :], preferred_element_type=F32)
        y = y + jnp.dot(yc_ref[rs, :], wout_ref[o2:, :], preferred_element_type=F32)
        ya = jnp.dot(p[rs], poolw_ref[...], preferred_element_type=F32) * pools_ref[...]
        y = y + jnp.dot(ya.astype(BF16), wout_ref[0:o1, :], preferred_element_type=F32)
        return x_ref[rs, :] + mod_ref[2:3, :] * y

    x1 = mixer_residual(0)
    for st in range(n_sub):
        rs = slice(st * SUB_ROWS, (st + 1) * SUB_ROWS)
        pending = iter(attention_steps(st + 1) if st + 1 < n_sub else [])
        run_next = lambda: next(pending, lambda: None)()
        run_next()

        xn = x1 * lax.rsqrt(jnp.mean(x1 * x1, axis=-1, keepdims=True) + EPS)
        h = (xn * gain + mod_ref[3:4, :]).astype(BF16)
        acc = None
        x1_next = None
        for c in range(n_chunks):
            up = jnp.dot(h, wup_ref[:, chunk(c)], preferred_element_type=F32)
            run_next()
            if c == n_chunks - 1:
                for step in pending:
                    step()
                if st + 1 < n_sub:
                    x1_next = mixer_residual(st + 1)
            act = jnp.square(jnp.maximum(up, 0.0)).astype(BF16)
            part = jnp.dot(act, wdown_ref[chunk(c), :], preferred_element_type=F32)
            acc = part if acc is None else acc + part
        x2 = x1 + mod_ref[5:6, :] * acc
        if final:
            x2 = x2 * lax.rsqrt(jnp.mean(x2 * x2, axis=-1, keepdims=True) + EPS) * gfin_ref[...]
        o_ref[rs, :] = x2
        x1 = x1_next


def _tail(qkv, state, x, mod, za, yc, pool_w, pool_scale, w_out, g_mlp, w_up, w_down, g_final,
          final):
    B, S, _ = x.shape
    W = ATTN_WIDTH
    tm = ROW_TILE
    assert S % tm == 0 and tm % SUB_ROWS == 0 and SUB_ROWS % Q_BLOCK == 0
    n_tiles = S // tm
    pool_per_tile = tm // POOL_HALO
    n_pool = S // POOL_HALO
    band_per_tile = tm // BAND_HALF
    n_band = S // BAND_HALF
    tok = lambda width: pl.BlockSpec((None, tm, width), lambda b, i: (b, i, 0))
    cur = lambda j: pl.BlockSpec((None, 1, tm, W), lambda b, i: (b, 0, i, j))
    prev = lambda j: pl.BlockSpec(
        (None, 1, BAND_HALF, W), lambda b, i: (b, 0, jnp.maximum(i * band_per_tile - 1, 0), j))
    nxt = lambda j: pl.BlockSpec(
        (None, 1, BAND_HALF, W),
        lambda b, i: (b, 0, jnp.minimum((i + 1) * band_per_tile, n_band - 1), j))
    st = lambda: pl.BlockSpec((None, FOLD, tm // FOLD, W), lambda b, i: (b, 0, i, 0))
    qkv4 = qkv.reshape(B, 1, S, 3 * W)
    return pl.pallas_call(
        functools.partial(_tail_kernel, seq_len=S, final=final),
        grid=(B, n_tiles),
        in_specs=[
            cur(0), prev(1), cur(1), nxt(1), prev(2), cur(2), nxt(2), st(), st(),
            tok(D_MODEL),
            pl.BlockSpec((None, N_MOD, D_MODEL), lambda b, i: (b, 0, 0)),
            tok(POOL_WIDTH),
            pl.BlockSpec((None, POOL_HALO, POOL_WIDTH),
                         lambda b, i: (b, jnp.maximum(i * pool_per_tile - 1, 0), 0)),
            pl.BlockSpec((None, POOL_HALO, POOL_WIDTH),
                         lambda b, i: (b, jnp.minimum((i + 1) * pool_per_tile, n_pool - 1), 0)),
            tok(SGU_WIDTH),
            _const_spec((POOL_WIDTH, POOL_WIDTH)),
            _const_spec((1, POOL_WIDTH)),
            _const_spec((D_MODEL, D_MODEL)),
            _const_spec((1, D_MODEL)),
            _const_spec((D_MODEL, D_FF)),
            _const_spec((D_FF, D_MODEL)),
            _const_spec((1, D_MODEL)),
        ],
        out_specs=tok(D_MODEL),
        out_shape=jax.ShapeDtypeStruct((B, S, D_MODEL), F32),
        scratch_shapes=[
            pltpu.VMEM((tm, W), BF16),
            pltpu.VMEM((2, HEAD_PAIRS, 2 * Q_BLOCK, K_BLOCK), F32),
            pltpu.VMEM((HEAD_PAIRS, tm, LANES), F32), pltpu.VMEM((HEAD_PAIRS, tm, LANES), F32),
            pltpu.VMEM((HEAD_PAIRS, 4, tm // 4, LANES), F32),
            pltpu.VMEM((HEAD_PAIRS, 4, tm // 4, LANES), F32)],
        compiler_params=pltpu.CompilerParams(
            dimension_semantics=("parallel", "parallel"), vmem_limit_bytes=VMEM_LIMIT),
        name="tail",
    )(qkv4, qkv4, qkv4, qkv4, qkv4, qkv4, qkv4, state[0], state[1],
      x, mod, za, za, za, yc, pool_w, pool_scale, w_out, g_mlp, w_up, w_down, g_final)


def _rope_tables(seq_len):
    inv_freq = ROPE_THETA ** (-jnp.arange(0, ROT_DIM, 2, dtype=F32) / ROT_DIM)
    ang = jnp.arange(seq_len).astype(F32)[:, None] * inv_freq[None, :]
    cos, sin = jnp.cos(ang), jnp.sin(ang)
    pad = HEAD_DIM - ROT_DIM
    cos_head = jnp.concatenate([cos, cos, jnp.ones((seq_len, pad), F32)], axis=1)
    sin_head = jnp.concatenate([-sin, sin, jnp.zeros((seq_len, pad), F32)], axis=1)
    reps = LANES // HEAD_DIM
    return jnp.tile(cos_head, (1, reps)), jnp.tile(sin_head, (1, reps))


def _block_diag(blocks):
    g, n, m = blocks.shape
    eye = jnp.eye(g, dtype=blocks.dtype)
    return (eye[:, None, :, None] * blocks[:, :, None, :]).reshape(g * n, g * m)


def _prepare_layer(l, g_mix, g_mlp, w_in, pool_w, pool_scale, sgu_w, sgu_b, w_out, w_up, w_down):
    return dict(
        g_mix=g_mix[l].reshape(1, D_MODEL),
        g_mlp=g_mlp[l].reshape(1, D_MODEL),
        w_in=w_in[l].astype(BF16),
        pool_w=_block_diag(pool_w[l]).astype(BF16),
        pool_scale=pool_scale[l].reshape(1, POOL_WIDTH),
        sgu_w=sgu_w[l].reshape(SGU_GROUPS * SGU_CHUNK, SGU_CHUNK).astype(BF16),
        sgu_b=jnp.repeat(jnp.transpose(sgu_b[l]), SGU_GROUP_DIM, axis=1),
        w_out=w_out[l].astype(BF16),
        w_up=w_up[l].astype(BF16),
        w_down=w_down[l].astype(BF16),
    )


def _trunk(x, mods, layers, g_final):
    B, S, _ = x.shape
    cos_t, sin_t = _rope_tables(S)
    for l, lw in enumerate(layers):
        mod = mods[l].reshape(B, N_MOD, D_MODEL)
        za, qkv, qkv_folded, yc = _inproj(x, mod, lw["g_mix"], cos_t, sin_t, lw["w_in"],
                                          lw["sgu_w"], lw["sgu_b"])
        state = _folded_attention(qkv_folded)
        x = _tail(qkv, state, x, mod, za, yc, lw["pool_w"], lw["pool_scale"], lw["w_out"],
                  lw["g_mlp"], lw["w_up"], lw["w_down"], g_final, final=(l == DEPTH - 1))
    return x


def kernel(x_prompt, x_sample, c_prompt, c_sample, w_ada, b_ada, g_mix, g_mlp, w_in, pool_w,
           pool_scale, sgu_w, sgu_b, w_out, w_up, w_down, g_final):
    nb_p = c_prompt.shape[0]
    nb_s = c_sample.shape[0]
    rows = -(-(nb_p + nb_s) // 8) * 8
    c_all = jnp.concatenate(
        [c_prompt, c_sample, jnp.zeros((rows - nb_p - nb_s, D_MODEL), F32)], axis=0)
    mods = _ada_mod(c_all, w_ada, b_ada)
    layers = [_prepare_layer(l, g_mix, g_mlp, w_in, pool_w, pool_scale, sgu_w, sgu_b,
                             w_out, w_up, w_down) for l in range(DEPTH)]
    gf = g_final.reshape(1, D_MODEL)
    y_prompt = _trunk(x_prompt, mods[:, :nb_p], layers, gf)
    y_sample = _trunk(x_sample, mods[:, nb_p:nb_p + nb_s], layers, gf)
    return (y_prompt, y_sample)
```

```python
import functools

import jax
import jax.numpy as jnp
from jax import lax
from jax.experimental import pallas as pl
from jax.experimental.pallas import tpu as pltpu

F32 = jnp.float32
BF16 = jnp.bfloat16

D_MODEL = 1024
DEPTH = 2
HEAD_DIM = 64
POOL_WINDOWS = (2, 4, 8, 16)
POOL_WIDTH = D_MODEL // 4
POOL_GROUP_DIM = POOL_WIDTH // len(POOL_WINDOWS)
ATTN_WIDTH = D_MODEL // 2
BAND_HALF = 64
FOLD = 16
MID_DILATION = 4
ROT_DIM = HEAD_DIM // 4
ROPE_THETA = 500000.0
SGU_WIDTH = D_MODEL // 4
SGU_GROUPS = 4
SGU_GROUP_DIM = SGU_WIDTH // SGU_GROUPS
SGU_CHUNK = 128
D_FF = 4 * D_MODEL
N_MOD = 6
EPS = 1e-6
MASK_VALUE = -1e30
Q_SCALE = HEAD_DIM ** -0.5 * 1.4426950408889634

OFF_Q = POOL_WIDTH
OFF_K = OFF_Q + ATTN_WIDTH
OFF_V = OFF_K + ATTN_WIDTH
OFF_C = OFF_V + ATTN_WIDTH
PROJ_WIDTH = OFF_C + 2 * SGU_WIDTH

LANES = 128
HEAD_PAIRS = ATTN_WIDTH // LANES
POOL_HALO = 8
ROW_TILE = 512
INPROJ_TILE = 1024
ATTN_TILE = 2048
Q_BLOCK = 128
K_BLOCK = Q_BLOCK + 2 * BAND_HALF
FF_CHUNK = 1024
SUB_ROWS = 256
VMEM_LIMIT = 56 * 1024 * 1024


def _const_spec(shape):
    nd = len(shape)
    return pl.BlockSpec(shape, lambda *_: (0,) * nd, pipeline_mode=pl.Buffered(1))


def _ada_kernel(c_ref, w_ref, b_ref, o_ref):
    c = c_ref[...]
    act = c * (1.0 / (1.0 + jnp.exp(-c)))
    o_ref[...] = jnp.dot(act.astype(BF16), w_ref[...].astype(BF16),
                         preferred_element_type=F32) + b_ref[...]


def _ada_mod(c_all, w_ada, b_ada):
    rows = c_all.shape[0]
    return pl.pallas_call(
        _ada_kernel,
        grid=(DEPTH, N_MOD),
        in_specs=[
            pl.BlockSpec((rows, D_MODEL), lambda l, j: (0, 0)),
            pl.BlockSpec((None, D_MODEL, D_MODEL), lambda l, j: (l, 0, j)),
            pl.BlockSpec((None, 1, D_MODEL), lambda l, j: (l, 0, j)),
        ],
        out_specs=pl.BlockSpec((None, rows, D_MODEL), lambda l, j: (l, 0, j)),
        out_shape=jax.ShapeDtypeStruct((DEPTH, rows, N_MOD * D_MODEL), F32),
        compiler_params=pltpu.CompilerParams(vmem_limit_bytes=VMEM_LIMIT),
        name="ada_mod",
    )(c_all, w_ada, b_ada.reshape(DEPTH, 1, N_MOD * D_MODEL))


def _rope_cols(z, cos, sin, low_half):
    up = pltpu.roll(z, LANES - ROT_DIM // 2, axis=1)
    down = pltpu.roll(z, ROT_DIM // 2, axis=1)
    return z * cos + jnp.where(low_half, up, down) * sin


def _group_sums(v, ones_bd):
    hi = v.astype(BF16)
    lo = (v - hi.astype(F32)).astype(BF16)
    return (jnp.dot(hi, ones_bd, preferred_element_type=F32)
            + jnp.dot(lo, ones_bd, preferred_element_type=F32))


def _inproj_kernel(x_ref, mod_ref, g_ref, cos_ref, sin_ref, w_ref, sguw_ref, sgub_ref,
                   za_ref, qkv_ref, qkvf_ref, yc_ref):
    rows = x_ref.shape[0]
    lane = lax.broadcasted_iota(jnp.int32, (1, LANES), 1)
    low_half = (lane % HEAD_DIM) < (ROT_DIM // 2)
    gi = lax.broadcasted_iota(jnp.int32, (SGU_WIDTH, SGU_WIDTH), 0) // SGU_GROUP_DIM
    gj = lax.broadcasted_iota(jnp.int32, (SGU_WIDTH, SGU_WIDTH), 1) // SGU_GROUP_DIM
    ones_bd = jnp.where(gi == gj, 1.0, 0.0).astype(BF16)
    lane_group = lax.broadcasted_iota(jnp.int32, (1, SGU_WIDTH), 1) // SGU_GROUP_DIM
    bias = sgub_ref[...]
    gain = g_ref[...] * (1.0 + mod_ref[1:2, :])
    shift = mod_ref[0:1, :]
    out_row = lax.broadcasted_iota(jnp.int32, (SUB_ROWS, SUB_ROWS), 0)
    in_row = lax.broadcasted_iota(jnp.int32, (SUB_ROWS, SUB_ROWS), 1)
    per = SUB_ROWS // FOLD
    fold_perm = jnp.where(in_row == (out_row % per) * FOLD + out_row // per, 1.0, 0.0).astype(BF16)

    for st in range(rows // SUB_ROWS):
        row0 = st * SUB_ROWS
        rs = slice(row0, row0 + SUB_ROWS)
        x = x_ref[rs, :]
        xn = x * lax.rsqrt(jnp.mean(x * x, axis=-1, keepdims=True) + EPS)
        h = (xn * gain + shift).astype(BF16)

        cos = cos_ref[rs, :]
        sin = sin_ref[rs, :]
        gate = jax.nn.gelu(jnp.dot(h, w_ref[:, OFF_C:PROJ_WIDTH], preferred_element_type=F32))
        zq = jnp.dot(h, w_ref[:, OFF_Q:OFF_K], preferred_element_type=F32)
        u = gate[:, :SGU_WIDTH]
        v = gate[:, SGU_WIDTH:]
        dv = v - _group_sums(v, ones_bd) * (1.0 / SGU_GROUP_DIM)
        zk = jnp.dot(h, w_ref[:, OFF_K:OFF_V], preferred_element_type=F32)
        var = _group_sums(dv * dv, ones_bd) * (1.0 / SGU_GROUP_DIM)
        zv = jnp.dot(h, w_ref[:, OFF_V:OFF_C], preferred_element_type=F32)
        vn = (dv * lax.rsqrt(var + EPS)).astype(BF16)
        za_ref[rs, :] = jnp.dot(h, w_ref[:, 0:OFF_Q], preferred_element_type=F32)

        for ch in range(SUB_ROWS // SGU_CHUNK):
            sl = slice(ch * SGU_CHUNK, (ch + 1) * SGU_CHUNK)
            mixed = jnp.dot(sguw_ref[...], vn[sl], preferred_element_type=F32)
            vm = mixed[0:SGU_CHUNK]
            for g in range(1, SGU_GROUPS):
                vm = jnp.where(lane_group == g, mixed[g * SGU_CHUNK:(g + 1) * SGU_CHUNK], vm)
            yc_ref[row0 + ch * SGU_CHUNK:row0 + (ch + 1) * SGU_CHUNK, :] = (
                u[sl] * (vm + bias)).astype(BF16)

        blocks = lambda z, f: jnp.concatenate(
            [f(z[:, c * LANES:(c + 1) * LANES]) for c in range(HEAD_PAIRS)], axis=1).astype(BF16)
        rope = lambda zc: _rope_cols(zc, cos, sin, low_half)
        for j, val in enumerate((blocks(zq, lambda zc: rope(zc) * Q_SCALE), blocks(zk, rope),
                                 zv.astype(BF16))):
            cols = slice(j * ATTN_WIDTH, (j + 1) * ATTN_WIDTH)
            qkv_ref[rs, cols] = val
            folded = jnp.dot(fold_perm, val, preferred_element_type=F32).astype(BF16)
            per_res = SUB_ROWS // FOLD
            for r in range(FOLD):
                qkvf_ref[r, row0 // FOLD:row0 // FOLD + per_res, cols] = (
                    folded[r * per_res:(r + 1) * per_res])


def _inproj(x, mod, g_mix, cos_t, sin_t, w_in, sgu_w, sgu_b):
    B, S, _ = x.shape
    tm = INPROJ_TILE
    assert S % tm == 0 and tm % SUB_ROWS == 0 and SUB_ROWS % (FOLD * 16) == 0
    grid = (B, S // tm)
    tok = lambda width: pl.BlockSpec((None, tm, width), lambda b, i: (b, i, 0))
    qkv_width = 3 * ATTN_WIDTH
    folded = pl.BlockSpec((None, FOLD, tm // FOLD, qkv_width), lambda b, i: (b, 0, i, 0))
    nat_shape = jax.ShapeDtypeStruct((B, S, qkv_width), BF16)
    fold_shape = jax.ShapeDtypeStruct((B, FOLD, S // FOLD, qkv_width), BF16)
    return pl.pallas_call(
        _inproj_kernel,
        grid=grid,
        in_specs=[
            tok(D_MODEL),
            pl.BlockSpec((None, N_MOD, D_MODEL), lambda b, i: (b, 0, 0)),
            _const_spec((1, D_MODEL)),
            pl.BlockSpec((tm, LANES), lambda b, i: (i, 0)),
            pl.BlockSpec((tm, LANES), lambda b, i: (i, 0)),
            _const_spec((D_MODEL, PROJ_WIDTH)),
            _const_spec((SGU_GROUPS * SGU_CHUNK, SGU_CHUNK)),
            _const_spec((SGU_CHUNK, SGU_WIDTH)),
        ],
        out_specs=[tok(POOL_WIDTH), tok(qkv_width), folded, tok(SGU_WIDTH)],
        out_shape=[jax.ShapeDtypeStruct((B, S, POOL_WIDTH), F32),
                   nat_shape, fold_shape,
                   jax.ShapeDtypeStruct((B, S, SGU_WIDTH), BF16)],
        compiler_params=pltpu.CompilerParams(
            dimension_semantics=("parallel", "parallel"), vmem_limit_bytes=VMEM_LIMIT),
        name="inproj",
    )(x, mod, g_mix, cos_t, sin_t, w_in, sgu_w, sgu_b)


def _attention_tile(q_ref, k_refs, v_refs, state_in, outs, s_scr, order_scr, *,
                    pieces, first, last, seq_len, tile, raw_out=False, raw_in=False):
    T = pieces
    kp_ref, kc_ref, kn_ref = k_refs
    vp_ref, vc_ref, vn_ref = v_refs
    if not first and not raw_in:
        o_in, lse_in = state_in
    if last:
        (y_out,) = outs
        o_nat, lse_nat, o_mid, lse_mid = order_scr
    elif not raw_out:
        o_out, lse_out = outs
    tl = q_ref.shape[1]
    halo = BAND_HALF // T
    pq = Q_BLOCK // T
    pk = K_BLOCK // T

    def prepare():
        if not last:
            return
        per_res = tl // FOLD
        for c in range(HEAD_PAIRS):
            cols = slice(c * LANES, (c + 1) * LANES)
            for src, mid, dst in ((o_in, o_mid, o_nat), (lse_in, lse_mid, lse_nat)):
                for r1 in range(4):
                    for r2 in range(4):
                        mid[c, r1, pl.ds(r2, per_res, stride=4), :] = (
                            src[4 * r2 + r1, :, cols].astype(F32))
                for r1 in range(4):
                    dst[c, pl.ds(r1, 4 * per_res, stride=4), :] = mid[c, r1]

    row = lax.broadcasted_iota(jnp.int32, (Q_BLOCK, K_BLOCK), 0)
    col = lax.broadcasted_iota(jnp.int32, (Q_BLOCK, K_BLOCK), 1)
    rel = T * (col % pk - halo - row % pq) + (col // pk - row // pq)
    band = jnp.abs(rel) <= BAND_HALF
    lane = lax.broadcasted_iota(jnp.int32, (1, LANES), 1)
    first_head = lane < HEAD_DIM
    zero = jnp.zeros((), BF16)
    ones_block = jnp.ones((K_BLOCK, LANES), BF16)
    ones_first = jnp.broadcast_to(jnp.where(first_head, 1.0, 0.0).astype(BF16), (K_BLOCK, LANES))
    ones_second = jnp.broadcast_to(jnp.where(first_head, 0.0, 1.0).astype(BF16), (K_BLOCK, LANES))

    def gather(ref, start, size, cols):
        parts = [ref[t, start:start + size, cols] for t in range(T)]
        return parts[0] if T == 1 else jnp.concatenate(parts, axis=0)

    def window(prev_ref, cur_ref, next_ref, la, cols):
        lo, hi = la - halo, la + pq + halo
        parts = []
        for t in range(T):
            if lo < 0:
                parts.append(prev_ref[t, :, cols])
            parts.append(cur_ref[t, max(lo, 0):min(hi, tl), cols])
            if hi > tl:
                parts.append(next_ref[t, :, cols])
        return parts[0] if len(parts) == 1 else jnp.concatenate(parts, axis=0)

    def block_bias(a):
        kidx = tile * tl + a * pq - halo + col % pk
        valid = band & (kidx >= 0) & (kidx < seq_len)
        return jnp.where(valid, 0.0, MASK_VALUE)

    def scores(a, c, bias):
        la = a * pq
        cols = slice(c * LANES, (c + 1) * LANES)
        qp = gather(q_ref, la, pq, cols)
        kp = window(kp_ref, kc_ref, kn_ref, la, cols)
        q2 = jnp.concatenate([jnp.where(first_head, qp, zero),
                              jnp.where(first_head, zero, qp)], axis=0)
        s = lax.dot_general(q2, kp, (((1,), (1,)), ((), ())), preferred_element_type=F32)
        s = s + jnp.concatenate([bias, bias], axis=0)
        s_scr[a % 2, c] = s
        return jnp.max(s, axis=-1, keepdims=True)

    def values(a, c, row_max):
        la = a * pq
        cols = slice(c * LANES, (c + 1) * LANES)
        vp = window(vp_ref, vc_ref, vn_ref, la, cols)
        p = jnp.exp2(s_scr[a % 2, c] - row_max).astype(BF16)
        if T > 1:
            rhs = jnp.concatenate([
                jnp.concatenate([jnp.where(first_head, vp, zero), ones_first], axis=1),
                jnp.concatenate([jnp.where(first_head, zero, vp), ones_second], axis=1)], axis=0)
            res = jnp.dot(jnp.concatenate([p[:Q_BLOCK], p[Q_BLOCK:]], axis=1), rhs,
                          preferred_element_type=F32)
            pv = res[:, :LANES]
            den = res[:, LANES:]
        else:
            res = jnp.dot(p, jnp.concatenate([vp, ones_block], axis=1),
                          preferred_element_type=F32)
            pv = jnp.where(first_head, res[:Q_BLOCK, :LANES], res[Q_BLOCK:, :LANES])
            den = jnp.where(first_head, res[:Q_BLOCK, LANES:], res[Q_BLOCK:, LANES:])
        m = jnp.where(first_head, row_max[:Q_BLOCK], row_max[Q_BLOCK:])
        if raw_out:
            pv_out, den_out, max_out = outs
            for t in range(T):
                piece = slice(t * pq, (t + 1) * pq)
                pv_out[t, la:la + pq, cols] = pv[piece]
                den_out[t, la:la + pq, cols] = den[piece]
                max_out[t, la:la + pq, cols] = m[piece]
            return
        if first:
            o_new = pv / den
            lse_new = m + jnp.log2(den)
        else:
            if raw_in:
                pv_prev, den_prev, max_prev = (gather(r, la, pq, cols) for r in state_in)
                top = jnp.maximum(max_prev, m)
                w_prev = jnp.exp2(max_prev - top)
                w_cur = jnp.exp2(m - top)
                total = w_prev * den_prev + w_cur * den
                o_new = (w_prev * pv_prev + w_cur * pv) / total
            else:
                if last:
                    o_prev = o_nat[c, la:la + pq, :]
                    lse_prev = lse_nat[c, la:la + pq, :]
                else:
                    o_prev = gather(o_in, la, pq, cols).astype(F32)
                    lse_prev = gather(lse_in, la, pq, cols)
                top = jnp.maximum(lse_prev, m)
                w_prev = jnp.exp2(lse_prev - top)
                w_cur = jnp.exp2(m - top)
                total = w_prev + w_cur * den
                o_new = (w_prev * o_prev + w_cur * pv) / total
            if not last:
                lse_new = top + jnp.log2(total)
        if last:
            y_out[la:la + pq, cols] = o_new.astype(BF16)
        else:
            for t in range(T):
                o_out[t, la:la + pq, cols] = o_new[t * pq:(t + 1) * pq].astype(BF16)
                lse_out[t, la:la + pq, cols] = lse_new[t * pq:(t + 1) * pq]

    return prepare, block_bias, scores, values, tl // pq


class _PieceView:
    def __init__(self, ref, t):
        self.ref, self.t = ref, t
        self.shape = (1,) + tuple(ref.shape[1:])

    def __getitem__(self, idx):
        _, rows, cols = idx
        return self.ref[self.t, rows, cols]

    def __setitem__(self, idx, value):
        _, rows, cols = idx
        self.ref[self.t, rows, cols] = value


class _RowWindow:
    def __init__(self, ref, start, size):
        self.ref, self.start, self.size = ref, start, size
        self.shape = (ref.shape[0], size, ref.shape[2])

    def __getitem__(self, idx):
        t, rows, cols = idx
        lo = self.start + (rows.start or 0)
        hi = self.start + (self.size if rows.stop is None else rows.stop)
        return self.ref[t, lo:hi, cols]


def _folded_kernel(q_ref, kp_ref, kc_ref, kn_ref, vp_ref, vc_ref, vn_ref, o_out, lse_out,
                   pv_mid, den_mid, max_mid, s_scr, *, seq_len):
    T = FOLD // MID_DILATION
    tile = pl.program_id(2)
    tl = q_ref.shape[1]
    halo4 = BAND_HALF // T
    wide = []
    for t in range(T):
        piece = lambda ref: _PieceView(ref, t)
        wide.append(_attention_tile(
            piece(q_ref), tuple(map(piece, (kp_ref, kc_ref, kn_ref))),
            tuple(map(piece, (vp_ref, vc_ref, vn_ref))),
            (piece(pv_mid), piece(den_mid), piece(max_mid)), (piece(o_out), piece(lse_out)),
            s_scr.at[pl.ds(2 * (t % 2), 2)], (), pieces=1, first=False, last=False, seq_len=seq_len,
            tile=tile, raw_in=True))
    inner = lambda prev_ref, nxt_ref: (
        _RowWindow(prev_ref, BAND_HALF - halo4, halo4), None, _RowWindow(nxt_ref, 0, halo4))
    k4 = inner(kp_ref, kn_ref)
    v4 = inner(vp_ref, vn_ref)
    _, bias4, scores4, values4, n4 = _attention_tile(
        q_ref, (k4[0], kc_ref, k4[2]), (v4[0], vc_ref, v4[2]), (),
        (pv_mid, den_mid, max_mid), s_scr.at[pl.ds(4, 2)], (), pieces=T, first=True, last=False,
        seq_len=seq_len, tile=tile, raw_out=True)

    def block_steps(bias_fn, scores_fn, values_fn, a):
        maxima = {}

        def score_step():
            bias = bias_fn(a)
            for c in range(HEAD_PAIRS):
                maxima[c] = scores_fn(a, c, bias)

        def value_step():
            for c in range(HEAD_PAIRS):
                values_fn(a, c, maxima[c])

        return [score_step, value_step]

    n16 = tl // Q_BLOCK
    per_group = n4 // n16

    def wide_group(j):
        pairs = []
        for t in range(T):
            _, bias16, scores16, values16, _ = wide[t]
            pairs.append(block_steps(bias16, scores16, values16, j))
        return [step for pair in pairs for step in pair]

    def narrow_group(j):
        steps = []
        for a in range(j * per_group, (j + 1) * per_group):
            steps += block_steps(bias4, scores4, values4, a)
        return steps

    for j in range(n16):
        for step in narrow_group(j) + wide_group(j):
            step()


def _folded_attention(folded):
    B, _, L, W3 = folded.shape
    W = W3 // 3
    T = FOLD // MID_DILATION
    tl = min(ATTN_TILE // T, L)
    assert L % tl == 0 and tl % Q_BLOCK == 0
    halo_per_tile = tl // BAND_HALF
    n_halo = L // BAND_HALF
    qkv = folded.reshape(B, T, MID_DILATION, L, W3)
    cur = lambda j=0: pl.BlockSpec((None, T, None, tl, W), lambda b, r, i: (b, 0, r, i, j))
    prev = lambda j: pl.BlockSpec(
        (None, T, None, BAND_HALF, W),
        lambda b, r, i: (b, 0, r, jnp.maximum(i * halo_per_tile - 1, 0), j))
    nxt = lambda j: pl.BlockSpec(
        (None, T, None, BAND_HALF, W),
        lambda b, r, i: (b, 0, r, jnp.minimum((i + 1) * halo_per_tile, n_halo - 1), j))
    o, lse = pl.pallas_call(
        functools.partial(_folded_kernel, seq_len=L),
        grid=(B, MID_DILATION, L // tl),
        in_specs=[cur(0), prev(1), cur(1), nxt(1), prev(2), cur(2), nxt(2)],
        out_specs=[cur(), cur()],
        out_shape=[jax.ShapeDtypeStruct((B, T, MID_DILATION, L, W), BF16),
                   jax.ShapeDtypeStruct((B, T, MID_DILATION, L, W), F32)],
        scratch_shapes=[pltpu.VMEM((T, tl, W), F32)] * 3 + [
            pltpu.VMEM((6, HEAD_PAIRS, 2 * Q_BLOCK, K_BLOCK), F32)],
        compiler_params=pltpu.CompilerParams(
            dimension_semantics=("parallel", "parallel", "parallel"),
            vmem_limit_bytes=VMEM_LIMIT),
        name="folded_attn",
    )(*([qkv] * 7))
    return o.reshape(B, FOLD, L, W), lse.reshape(B, FOLD, L, W)


def _pool_mixer(za_ref, zprev_ref, znext_ref, tile, n_tiles, seq_len):
    rows = za_ref.shape[0]
    za = za_ref[...]
    before = jnp.where(tile > 0, zprev_ref[...], 0.0)
    after = jnp.where(tile < n_tiles - 1, znext_ref[...], 0.0)
    ext = jnp.concatenate([before, za, after], axis=0)
    n_ext = rows + 2 * POOL_HALO
    back = lambda a, s: pltpu.roll(a, s, axis=0)
    fwd = lambda a, s: pltpu.roll(a, n_ext - s, axis=0)
    sums = [ext + back(ext, 1)]
    for step in (1, 2, 4):
        sums.append(back(sums[-1], step) + fwd(sums[-1], step))
    lane_group = lax.broadcasted_iota(jnp.int32, (1, POOL_WIDTH), 1) // POOL_GROUP_DIM
    win_sum = sums[0][POOL_HALO:POOL_HALO + rows]
    half = jnp.full((1, POOL_WIDTH), POOL_WINDOWS[0] // 2, jnp.int32)
    for g in range(1, len(POOL_WINDOWS)):
        win_sum = jnp.where(lane_group == g, sums[g][POOL_HALO:POOL_HALO + rows], win_sum)
        half = jnp.where(lane_group == g, POOL_WINDOWS[g] // 2, half)

    def inv_count(row0):
        pos = tile * rows + row0 + lax.broadcasted_iota(jnp.int32, (POOL_HALO, 1), 0)
        cnt = jnp.minimum(pos + half, seq_len) - jnp.maximum(pos - half, 0)
        return 1.0 / cnt.astype(F32)

    edge = POOL_HALO
    mean = jnp.concatenate([
        win_sum[:edge] * inv_count(0),
        win_sum[edge:rows - edge] * (0.5 / half.astype(F32)),
        win_sum[rows - edge:] * inv_count(rows - edge)], axis=0)
    return mean - za


def _tail_kernel(q_ref, kp_ref, kc_ref, kn_ref, vp_ref, vc_ref, vn_ref, o_in, lse_in,
                 x_ref, mod_ref, za_ref, zprev_ref, znext_ref, yc_ref,
                 poolw_ref, pools_ref, wout_ref, gmlp_ref, wup_ref, wdown_ref, gfin_ref,
                 o_ref, yb_scr, s_scr, o_nat, lse_nat, o_mid, lse_mid, *, seq_len, final):
    tile = pl.program_id(1)
    n_tiles = pl.num_programs(1)
    prepare, block_bias, scores, values, n_blocks = _attention_tile(
        q_ref, (kp_ref, kc_ref, kn_ref), (vp_ref, vc_ref, vn_ref), (o_in, lse_in), (yb_scr,),
        s_scr, (o_nat, lse_nat, o_mid, lse_mid),
        pieces=1, first=False, last=True, seq_len=seq_len, tile=tile)
    n_sub = x_ref.shape[0] // SUB_ROWS
    blocks_per_sub = n_blocks // n_sub

    def attention_steps(st):
        steps = []
        for a in range(st * blocks_per_sub, (st + 1) * blocks_per_sub):
            maxima = {}

            def score_step(a=a, maxima=maxima):
                bias = block_bias(a)
                for c in range(HEAD_PAIRS):
                    maxima[c] = scores(a, c, bias)

            def value_step(a=a, maxima=maxima):
                for c in range(HEAD_PAIRS):
                    values(a, c, maxima[c])

            steps += [score_step, value_step]
        return steps

    o1 = POOL_WIDTH
    o2 = o1 + ATTN_WIDTH
    gain = gmlp_ref[...] * (1.0 + mod_ref[4:5, :])
    n_chunks = D_FF // FF_CHUNK
    chunk = lambda c: slice(c * FF_CHUNK, (c + 1) * FF_CHUNK)

    prepare()
    for step in attention_steps(0):
        step()
    p = _pool_mixer(za_ref, zprev_ref, znext_ref, tile, n_tiles, seq_len).astype(BF16)

    def mixer_residual(st):
        rs = slice(st * SUB_ROWS, (st + 1) * SUB_ROWS)
        y = jnp.dot(yb_scr[rs, :], wout_ref[o1:o2, :], preferred_element_type=F32)
        y = y + jnp.dot(yc_ref[rs, :], wout_ref[o2:, :], preferred_element_type=F32)
        ya = jnp.dot(p[rs], poolw_ref[...], preferred_element_type=F32) * pools_ref[...]
        y = y + jnp.dot(ya.astype(BF16), wout_ref[0:o1, :], preferred_element_type=F32)
        return x_ref[rs, :] + mod_ref[2:3, :] * y

    x1 = mixer_residual(0)
    for st in range(n_sub):
        rs = slice(st * SUB_ROWS, (st + 1) * SUB_ROWS)
        pending = iter(attention_steps(st + 1) if st + 1 < n_sub else [])
        run_next = lambda: next(pending, lambda: None)()
        run_next()

        xn = x1 * lax.rsqrt(jnp.mean(x1 * x1, axis=-1, keepdims=True) + EPS)
        h = (xn * gain + mod_ref[3:4, :]).astype(BF16)
        acc = None
        x1_next = None
        for c in range(n_chunks):
            up = jnp.dot(h, wup_ref[:, chunk(c)], preferred_element_type=F32)
            run_next()
            if c == n_chunks - 1:
                for step in pending:
                    step()
                if st + 1 < n_sub:
                    x1_next = mixer_residual(st + 1)
            act = jnp.square(jnp.maximum(up, 0.0)).astype(BF16)
            part = jnp.dot(act, wdown_ref[chunk(c), :], preferred_element_type=F32)
            acc = part if acc is None else acc + part
        x2 = x1 + mod_ref[5:6, :] * acc
        if final:
            x2 = x2 * lax.rsqrt(jnp.mean(x2 * x2, axis=-1, keepdims=True) + EPS) * gfin_ref[...]
        o_ref[rs, :] = x2
        x1 = x1_next


def _tail(qkv, state, x, mod, za, yc, pool_w, pool_scale, w_out, g_mlp, w_up, w_down, g_final,
          final):
    B, S, _ = x.shape
    W = ATTN_WIDTH
    tm = ROW_TILE
    assert S % tm == 0 and tm % SUB_ROWS == 0 and SUB_ROWS % Q_BLOCK == 0
    n_tiles = S // tm
    pool_per_tile = tm // POOL_HALO
    n_pool = S // POOL_HALO
    band_per_tile = tm // BAND_HALF
    n_band = S // BAND_HALF
    tok = lambda width: pl.BlockSpec((None, tm, width), lambda b, i: (b, i, 0))
    cur = lambda j: pl.BlockSpec((None, 1, tm, W), lambda b, i: (b, 0, i, j))
    prev = lambda j: pl.BlockSpec(
        (None, 1, BAND_HALF, W), lambda b, i: (b, 0, jnp.maximum(i * band_per_tile - 1, 0), j))
    nxt = lambda j: pl.BlockSpec(
        (None, 1, BAND_HALF, W),
        lambda b, i: (b, 0, jnp.minimum((i + 1) * band_per_tile, n_band - 1), j))
    st = lambda: pl.BlockSpec((None, FOLD, tm // FOLD, W), lambda b, i: (b, 0, i, 0))
    qkv4 = qkv.reshape(B, 1, S, 3 * W)
    return pl.pallas_call(
        functools.partial(_tail_kernel, seq_len=S, final=final),
        grid=(B, n_tiles),
        in_specs=[
            cur(0), prev(1), cur(1), nxt(1), prev(2), cur(2), nxt(2), st(), st(),
            tok(D_MODEL),
            pl.BlockSpec((None, N_MOD, D_MODEL), lambda b, i: (b, 0, 0)),
            tok(POOL_WIDTH),
            pl.BlockSpec((None, POOL_HALO, POOL_WIDTH),
                         lambda b, i: (b, jnp.maximum(i * pool_per_tile - 1, 0), 0)),
            pl.BlockSpec((None, POOL_HALO, POOL_WIDTH),
                         lambda b, i: (b, jnp.minimum((i + 1) * pool_per_tile, n_pool - 1), 0)),
            tok(SGU_WIDTH),
            _const_spec((POOL_WIDTH, POOL_WIDTH)),
            _const_spec((1, POOL_WIDTH)),
            _const_spec((D_MODEL, D_MODEL)),
            _const_spec((1, D_MODEL)),
            _const_spec((D_MODEL, D_FF)),
            _const_spec((D_FF, D_MODEL)),
            _const_spec((1, D_MODEL)),
        ],
        out_specs=tok(D_MODEL),
        out_shape=jax.ShapeDtypeStruct((B, S, D_MODEL), F32),
        scratch_shapes=[
            pltpu.VMEM((tm, W), BF16),
            pltpu.VMEM((2, HEAD_PAIRS, 2 * Q_BLOCK, K_BLOCK), F32),
            pltpu.VMEM((HEAD_PAIRS, tm, LANES), F32), pltpu.VMEM((HEAD_PAIRS, tm, LANES), F32),
            pltpu.VMEM((HEAD_PAIRS, 4, tm // 4, LANES), F32),
            pltpu.VMEM((HEAD_PAIRS, 4, tm // 4, LANES), F32)],
        compiler_params=pltpu.CompilerParams(
            dimension_semantics=("parallel", "parallel"), vmem_limit_bytes=VMEM_LIMIT),
        name="tail",
    )(qkv4, qkv4, qkv4, qkv4, qkv4, qkv4, qkv4, state[0], state[1],
      x, mod, za, za, za, yc, pool_w, pool_scale, w_out, g_mlp, w_up, w_down, g_final)


def _rope_tables(seq_len):
    inv_freq = ROPE_THETA ** (-jnp.arange(0, ROT_DIM, 2, dtype=F32) / ROT_DIM)
    ang = jnp.arange(seq_len).astype(F32)[:, None] * inv_freq[None, :]
    cos, sin = jnp.cos(ang), jnp.sin(ang)
    pad = HEAD_DIM - ROT_DIM
    cos_head = jnp.concatenate([cos, cos, jnp.ones((seq_len, pad), F32)], axis=1)
    sin_head = jnp.concatenate([-sin, sin, jnp.zeros((seq_len, pad), F32)], axis=1)
    reps = LANES // HEAD_DIM
    return jnp.tile(cos_head, (1, reps)), jnp.tile(sin_head, (1, reps))


def _block_diag(blocks):
    g, n, m = blocks.shape
    eye = jnp.eye(g, dtype=blocks.dtype)
    return (eye[:, None, :, None] * blocks[:, :, None, :]).reshape(g * n, g * m)


def _prepare_layer(l, g_mix, g_mlp, w_in, pool_w, pool_scale, sgu_w, sgu_b, w_out, w_up, w_down):
    return dict(
        g_mix=g_mix[l].reshape(1, D_MODEL),
        g_mlp=g_mlp[l].reshape(1, D_MODEL),
        w_in=w_in[l].astype(BF16),
        pool_w=_block_diag(pool_w[l]).astype(BF16),
        pool_scale=pool_scale[l].reshape(1, POOL_WIDTH),
        sgu_w=sgu_w[l].reshape(SGU_GROUPS * SGU_CHUNK, SGU_CHUNK).astype(BF16),
        sgu_b=jnp.repeat(jnp.transpose(sgu_b[l]), SGU_GROUP_DIM, axis=1),
        w_out=w_out[l].astype(BF16),
        w_up=w_up[l].astype(BF16),
        w_down=w_down[l].astype(BF16),
    )


def _trunk(x, mods, layers, g_final):
    B, S, _ = x.shape
    cos_t, sin_t = _rope_tables(S)
    for l, lw in enumerate(layers):
        mod = mods[l].reshape(B, N_MOD, D_MODEL)
        za, qkv, qkv_folded, yc = _inproj(x, mod, lw["g_mix"], cos_t, sin_t, lw["w_in"],
                                          lw["sgu_w"], lw["sgu_b"])
        state = _folded_attention(qkv_folded)
        x = _tail(qkv, state, x, mod, za, yc, lw["pool_w"], lw["pool_scale"], lw["w_out"],
                  lw["g_mlp"], lw["w_up"], lw["w_down"], g_final, final=(l == DEPTH - 1))
    return x


def kernel(x_prompt, x_sample, c_prompt, c_sample, w_ada, b_ada, g_mix, g_mlp, w_in, pool_w,
           pool_scale, sgu_w, sgu_b, w_out, w_up, w_down, g_final):
    nb_p = c_prompt.shape[0]
    nb_s = c_sample.shape[0]
    rows = -(-(nb_p + nb_s) // 8) * 8
    c_all = jnp.concatenate(
        [c_prompt, c_sample, jnp.zeros((rows - nb_p - nb_s, D_MODEL), F32)], axis=0)
    mods = _ada_mod(c_all, w_ada, b_ada)
    layers = [_prepare_layer(l, g_mix, g_mlp, w_in, pool_w, pool_scale, sgu_w, sgu_b,
                             w_out, w_up, w_down) for l in range(DEPTH)]
    gf = g_final.reshape(1, D_MODEL)
    y_prompt = _trunk(x_prompt, mods[:, :nb_p], layers, gf)
    y_sample = _trunk(x_sample, mods[:, nb_p:nb_p + nb_s], layers, gf)
    return (y_prompt, y_sample)
```
